```python
import functools
import numpy as np
import jax
import jax.numpy as jnp
from jax import lax

D_MODEL = 2048
BATCH = 2
SEQ = 8192
DEPTH = 4

CTX_LEN = 256
GRID_W = 64
MIX_W = D_MODEL
GROUP_W = MIX_W // 4
CF_KERNEL = 31
SC_KERNEL = 3
GLA_HEADS = 4
GLA_DK = GROUP_W // (2 * GLA_HEADS)
GLA_DV = GROUP_W // GLA_HEADS
GLA_RANK = 16
GLA_TAU = 16.0
RET_HEADS = 4
RET_DK = GROUP_W // (2 * RET_HEADS)
RET_DV = GROUP_W // RET_HEADS
CHUNK = 64
ROPE_BASE = 10000.0
N_GROUPS = 4
EXPERTS_PER_GROUP = 4
N_EXPERTS = N_GROUPS * EXPERTS_PER_GROUP
TOP_K = 2
EXPERT_HIDDEN = D_MODEL // 2
MOE_BLOCK = 128
N_MOD = 6
EPS = 1e-6
IN_SIZES = (GROUP_W, GROUP_W,
            GROUP_W, GROUP_W, GROUP_W,
            GLA_HEADS * GLA_DK, GLA_HEADS * GLA_DK, GLA_HEADS * GLA_DV, GROUP_W, 2 * GLA_RANK,
            RET_HEADS * RET_DK, RET_HEADS * RET_DK, RET_HEADS * RET_DV, GROUP_W)
IN_W = sum(IN_SIZES)

kernel_name = 'hybrid_parallel_mixers_hmoe_dit'

F32 = jnp.float32


def rms_norm(x, g):
    xf = x.astype(F32)
    y = xf * lax.rsqrt(jnp.mean(xf * xf, axis=-1, keepdims=True) + EPS)
    return (y * g.astype(F32)).astype(x.dtype)


def layer_norm(x, g, b):
    xf = x.astype(F32)
    xc = xf - jnp.mean(xf, axis=-1, keepdims=True)
    y = xc * lax.rsqrt(jnp.mean(xc * xc, axis=-1, keepdims=True) + EPS)
    return (y * g.astype(F32) + b.astype(F32)).astype(x.dtype)


def head_norm(o, g, center):
    of = o.astype(F32)
    if center:
        of = of - jnp.mean(of, axis=-1, keepdims=True)
    y = of * lax.rsqrt(jnp.mean(of * of, axis=-1, keepdims=True) + EPS)
    return y.reshape(o.shape[:2] + (-1,)) * g.astype(F32)


def modulate(h, shift, scale):
    return h * (1.0 + scale) + shift


def split_in(z):
    pts = np.cumsum(IN_SIZES)[:-1].tolist()
    return jnp.split(z, pts, axis=-1)


def split_heads(t, n_heads):
    return t.reshape(t.shape[:-1] + (n_heads, t.shape[-1] // n_heads))


def dw_conv2d(x, w):
    kh, kw, ch = w.shape
    return lax.conv_general_dilated(
        x, w[:, :, None, :].astype(x.dtype), window_strides=(1, 1),
        padding=((kh // 2, kh // 2), (kw // 2, kw // 2)),
        dimension_numbers=('NHWC', 'HWIO', 'NHWC'), feature_group_count=ch)


def conformer_conv(a, gt, w, b, ln_g, ln_b, rows):
    bsz, length, ch = a.shape
    u = (a * jax.nn.sigmoid(gt)).reshape(bsz, rows, length // rows, ch)
    y = dw_conv2d(u, w.reshape(1, CF_KERNEL, ch)).reshape(bsz, length, ch) + b.astype(a.dtype)
    return jax.nn.silu(layer_norm(y, ln_g, ln_b))


def short_conv(bg, cg, v, w, rows, vertical):
    bsz, length, ch = v.shape
    u = (cg * v).reshape(bsz, rows, length // rows, ch)
    k = w.reshape(SC_KERNEL, 1, ch) if vertical else w.reshape(1, SC_KERNEL, ch)
    return bg * dw_conv2d(u, k).reshape(bsz, length, ch)


def gla_log_gates(lr, w2, b2):
    bsz, length, _ = lr.shape
    z = jnp.einsum('blsr,srk->blsk', lr.reshape(bsz, length, 2, GLA_RANK), w2) + b2
    la = (jax.nn.log_sigmoid(z.astype(F32)) / GLA_TAU).reshape(bsz, length, 2, GLA_HEADS, GLA_DK)
    return la[:, :, 0], la[:, :, 1]


def axial_rope(x):
    seq, dk = x.shape[1], x.shape[-1]
    n_freq = dk // 4
    t = jnp.arange(seq)
    inv = ROPE_BASE ** (-jnp.arange(n_freq, dtype=F32) / n_freq)
    ang = jnp.concatenate([(t // GRID_W).astype(F32)[:, None] * inv,
                           (t % GRID_W).astype(F32)[:, None] * inv], axis=-1)
    cos = jnp.cos(ang)[None, :, None, :]
    sin = jnp.sin(ang)[None, :, None, :]
    xf = x.astype(F32)
    x1, x2 = xf[..., :dk // 2], xf[..., dk // 2:]
    return jnp.concatenate([x1 * cos - x2 * sin, x1 * sin + x2 * cos], axis=-1).astype(x.dtype)


def to_chunks(t, n):
    bsz = t.shape[0]
    return t.reshape((bsz, n, CHUNK) + t.shape[2:]).transpose(1, 0, 3, 2, 4).astype(F32)


def from_chunks(o):
    n, bsz, h, cl, d = o.shape
    return o.transpose(1, 0, 3, 2, 4).reshape(bsz, n * cl, h, d)


def gla_chunked(q, k, v, log_a, s0):
    n = q.shape[1] // CHUNK
    qc, kc, vc, gc = (to_chunks(t, n) for t in (q, k, v, log_a))
    b = jnp.cumsum(gc, axis=3)
    b_last = b[:, :, :, -1:, :]
    q_in = qc * jnp.exp(b)
    k_in = kc * jnp.exp(-b)
    k_st = kc * jnp.exp(b_last - b)
    a_chunk = jnp.exp(b_last[:, :, :, 0, :])[..., None]
    lower = jnp.tril(jnp.ones((CHUNK, CHUNK), bool))
    scores = jnp.where(lower, jnp.einsum('nbhid,nbhjd->nbhij', q_in, k_in), 0.0)
    o_intra = jnp.einsum('nbhij,nbhjv->nbhiv', scores, vc)

    def step(s, inp):
        q_i, k_i, v_i, a_i = inp
        o_inter = jnp.einsum('bhid,bhdv->bhiv', q_i, s)
        s = s * a_i + jnp.einsum('bhjd,bhjv->bhdv', k_i, v_i)
        return s, o_inter

    s_fin, o_inter = lax.scan(step, s0, (q_in, k_st, vc, a_chunk))
    return from_chunks(o_intra + o_inter).astype(q.dtype), s_fin


def retention_chunked(q, k, v, s0, log_gamma):
    n = q.shape[1] // CHUNK
    qc, kc, vc = (to_chunks(t, n) for t in (q, k, v))
    pos = jnp.arange(CHUNK, dtype=F32)
    rel = pos[:, None] - pos[None, :]
    decay = jnp.exp(jnp.where(rel >= 0, rel * log_gamma[:, None, None], -jnp.inf))
    q_decay = jnp.exp((pos + 1.0) * log_gamma[:, None])[:, :, None]
    k_decay = jnp.exp((CHUNK - 1.0 - pos) * log_gamma[:, None])[:, :, None]
    chunk_decay = jnp.exp(CHUNK * log_gamma)[:, None, None]
    scores = jnp.einsum('nbhid,nbhjd->nbhij', qc, kc) * decay
    o_intra = jnp.einsum('nbhij,nbhjv->nbhiv', scores, vc)

    def step(s, inp):
        q_i, k_i, v_i = inp
        o_inter = jnp.einsum('bhid,bhdv->bhiv', q_i, s) * q_decay
        s = s * chunk_decay + jnp.einsum('bhjd,bhjv->bhdv', k_i * k_decay, v_i)
        return s, o_inter

    s_fin, o_inter = lax.scan(step, s0, (qc, kc, vc))
    return from_chunks(o_intra + o_inter).astype(q.dtype), s_fin


def prefix_bidirectional(scan_f, scan_b, ctx_f, ctx_b, lat_f, lat_b, s0):
    def flip(seq):
        return tuple(jnp.flip(t, axis=1) for t in seq)
    o_cf, s_f = scan_f(*ctx_f, s0)
    o_cb, s_b = scan_b(*flip(ctx_b), s0)
    o_lf, _ = scan_f(*lat_f, s_f)
    o_lb, _ = scan_b(*flip(lat_b), s_b)
    return o_cf + jnp.flip(o_cb, axis=1), o_lf + jnp.flip(o_lb, axis=1)


def hierarchical_moe(h, w_grp, b_grp, w_rt, b_rt, w1, w3, w2):
    n_tok, d = h.shape
    grp_prob = jax.nn.softmax((h @ w_grp).astype(F32) + b_grp.astype(F32), axis=-1)
    g_w, g_idx = lax.top_k(grp_prob, 1)
    ex_logits = (h @ w_rt).astype(F32) + b_rt.astype(F32)
    cand = g_idx * EXPERTS_PER_GROUP + jnp.arange(EXPERTS_PER_GROUP)[None, :]
    ex_prob = jax.nn.softmax(jnp.take_along_axis(ex_logits, cand, axis=1), axis=-1)
    e_w, e_idx = lax.top_k(ex_prob, TOP_K)
    gate = g_w * (e_w / jnp.sum(e_w, axis=-1, keepdims=True))
    expert = g_idx * EXPERTS_PER_GROUP + e_idx
    n_asg = n_tok * TOP_K
    flat_e = expert.reshape(n_asg)
    flat_t = jnp.repeat(jnp.arange(n_tok, dtype=jnp.int32), TOP_K)
    flat_w = gate.reshape(n_asg)
    order = jnp.argsort(flat_e)
    se = flat_e[order]
    counts = jnp.bincount(flat_e, length=N_EXPERTS)
    padded = (counts + MOE_BLOCK - 1) // MOE_BLOCK * MOE_BLOCK
    pad_end = jnp.cumsum(padded)
    pad_start = pad_end - padded
    start = jnp.cumsum(counts) - counts
    dest = pad_start[se] + jnp.arange(n_asg) - start[se]
    n_slots = (n_asg + MOE_BLOCK - 1) // MOE_BLOCK * MOE_BLOCK + N_EXPERTS * MOE_BLOCK
    n_blocks = n_slots // MOE_BLOCK
    slot_tok = jnp.full((n_slots,), n_tok, jnp.int32).at[dest].set(flat_t[order])
    slot_w = jnp.zeros((n_slots,), F32).at[dest].set(flat_w[order])
    blk_e = jnp.minimum(jnp.searchsorted(pad_end, jnp.arange(n_blocks) * MOE_BLOCK, side='right'),
                        N_EXPERTS - 1)
    h_pad = jnp.concatenate([h, jnp.zeros((1, d), h.dtype)], axis=0)
    xb = h_pad[slot_tok].reshape(n_blocks, MOE_BLOCK, d)

    def run_block(args):
        xb_i, e = args
        return (jax.nn.silu(xb_i @ w1[e]) * (xb_i @ w3[e])) @ w2[e]

    yb = lax.map(run_block, (xb, blk_e)).reshape(n_slots, d)
    out = jnp.zeros((n_tok + 1, d), F32).at[slot_tok].add(yb.astype(F32) * slot_w[:, None])
    return out[:n_tok].astype(h.dtype)


def token_mixers(h_lat, h_ctx, w_in, cf_dw, cf_b, cf_ln_g, cf_ln_b, sc_dw, gla_w2, gla_b2, gla_ng,
                 ret_logit, ret_ng, w_out, with_ctx_out):
    bsz, seq, _ = h_lat.shape
    rows = seq // GRID_W
    dt = h_lat.dtype
    (cfa_l, cfg_l, scb_l, scc_l, scv_l, gq_l, gk_l, gv_l, gr_l, glr_l,
     rq_l, rk_l, rv_l, rg_l) = split_in(h_lat @ w_in)
    (cfa_c, cfg_c, scb_c, scc_c, scv_c, gq_c, gk_c, gv_c, gr_c, glr_c,
     rq_c, rk_c, rv_c, rg_c) = split_in(h_ctx @ w_in)

    def gla_seqs(q, k, v, lr):
        qh = split_heads(q, GLA_HEADS) * GLA_DK ** -0.5
        kh = split_heads(k, GLA_HEADS)
        vh = split_heads(v, GLA_HEADS)
        la_f, la_b = gla_log_gates(lr, gla_w2, gla_b2)
        return (qh, kh, vh, la_f), (qh, kh, vh, la_b)

    gc_f, gc_b = gla_seqs(gq_c, gk_c, gv_c, glr_c)
    gl_f, gl_b = gla_seqs(gq_l, gk_l, gv_l, glr_l)
    s0_g = jnp.zeros((bsz, GLA_HEADS, GLA_DK, GLA_DV), F32)
    o_gc, o_gl = prefix_bidirectional(gla_chunked, gla_chunked, gc_f, gc_b, gl_f, gl_b, s0_g)

    log_gamma = jax.nn.log_sigmoid(ret_logit.astype(F32))
    ret_f = functools.partial(retention_chunked, log_gamma=log_gamma[0])
    ret_b = functools.partial(retention_chunked, log_gamma=log_gamma[1])
    rc = (split_heads(rq_c, RET_HEADS), split_heads(rk_c, RET_HEADS) * RET_DK ** -0.5,
          split_heads(rv_c, RET_HEADS))
    rl = (axial_rope(split_heads(rq_l, RET_HEADS)),
          axial_rope(split_heads(rk_l, RET_HEADS)) * RET_DK ** -0.5,
          split_heads(rv_l, RET_HEADS))
    s0_r = jnp.zeros((bsz, RET_HEADS, RET_DK, RET_DV), F32)
    o_rc, o_rl = prefix_bidirectional(ret_f, ret_b, rc, rc, rl, rl, s0_r)

    def merge(cf, sc, o_g, gr, o_r, rg):
        gla = head_norm(o_g, gla_ng, False) * jax.nn.silu(gr.astype(F32))
        ret = head_norm(o_r, ret_ng, True) * jax.nn.silu(rg.astype(F32))
        return jnp.concatenate([cf, sc, gla.astype(dt), ret.astype(dt)], axis=-1) @ w_out

    y_lat = merge(conformer_conv(cfa_l, cfg_l, cf_dw, cf_b, cf_ln_g, cf_ln_b, rows),
                  short_conv(scb_l, scc_l, scv_l, sc_dw, rows, True),
                  o_gl, gr_l, o_rl, rg_l)
    if not with_ctx_out:
        return y_lat, None
    y_ctx = merge(conformer_conv(cfa_c, cfg_c, cf_dw, cf_b, cf_ln_g, cf_ln_b, 1),
                  short_conv(scb_c, scc_c, scv_c, sc_dw, 1, False),
                  o_gc, gr_c, o_rc, rg_c)
    return y_lat, y_ctx


def setup_inputs(seed: int = 0) -> dict:
    key = jax.random.key(seed)
    ks = iter(jax.random.split(key, 32))

    def nrm(shape, scale):
        return jax.random.normal(next(ks), shape, F32) * scale

    d = D_MODEL
    ret_base = jnp.log(2.0 ** (5.0 + jnp.arange(RET_HEADS, dtype=F32)) - 1.0)
    return {
        'x': nrm((BATCH, SEQ, d), 1.0),
        'c': nrm((BATCH, d), 1.0),
        'ctx': nrm((BATCH, CTX_LEN, d), 1.0),
        'c_ctx': nrm((d,), 1.0),
        'norm1_g': 1.0 + nrm((DEPTH, d), 0.02),
        'norm2_g': 1.0 + nrm((DEPTH, d), 0.02),
        'ada_w': nrm((DEPTH, d, N_MOD * d), 0.5 * d ** -0.5),
        'ada_b': nrm((DEPTH, N_MOD * d), 0.02),
        'w_in': nrm((DEPTH, d, IN_W), d ** -0.5),
        'cf_dw': nrm((DEPTH, CF_KERNEL, GROUP_W), CF_KERNEL ** -0.5),
        'cf_b': nrm((DEPTH, GROUP_W), 0.02),
        'cf_ln_g': 1.0 + nrm((DEPTH, GROUP_W), 0.02),
        'cf_ln_b': nrm((DEPTH, GROUP_W), 0.02),
        'sc_dw': nrm((DEPTH, SC_KERNEL, GROUP_W), SC_KERNEL ** -0.5),
        'gla_w2': nrm((DEPTH, 2, GLA_RANK, GLA_HEADS * GLA_DK), GLA_RANK ** -0.5),
        'gla_b2': nrm((DEPTH, 2, GLA_HEADS * GLA_DK), 0.1),
        'gla_ng': 1.0 + nrm((DEPTH, GROUP_W), 0.02),
        'ret_logit': ret_base + nrm((DEPTH, 2, RET_HEADS), 0.01),
        'ret_ng': 1.0 + nrm((DEPTH, GROUP_W), 0.02),
        'w_out': nrm((DEPTH, MIX_W, d), MIX_W ** -0.5),
        'w_grp': nrm((DEPTH, d, N_GROUPS), d ** -0.5),
        'b_grp': nrm((DEPTH, N_GROUPS), 0.01),
        'w_rt': nrm((DEPTH, d, N_EXPERTS), d ** -0.5),
        'b_rt': nrm((DEPTH, N_EXPERTS), 0.01),
        'e_w1': nrm((DEPTH, N_EXPERTS, d, EXPERT_HIDDEN), d ** -0.5),
        'e_w3': nrm((DEPTH, N_EXPERTS, d, EXPERT_HIDDEN), d ** -0.5),
        'e_w2': nrm((DEPTH, N_EXPERTS, EXPERT_HIDDEN, d), EXPERT_HIDDEN ** -0.5),
        'final_g': 1.0 + nrm((d,), 0.02),
    }


def reference(x, c, ctx, c_ctx, norm1_g, norm2_g, ada_w, ada_b, w_in, cf_dw, cf_b, cf_ln_g, cf_ln_b,
              sc_dw, gla_w2, gla_b2, gla_ng, ret_logit, ret_ng, w_out, w_grp, b_grp, w_rt, b_rt,
              e_w1, e_w3, e_w2, final_g):
    bsz, seq, d = x.shape
    n_ctx_tok = bsz * ctx.shape[1]
    s_lat = jax.nn.silu(c)
    s_ctx = jax.nn.silu(c_ctx)
    x_lat, x_ctx = x, ctx
    for l in range(DEPTH):
        last = l == DEPTH - 1
        m_lat = jnp.split((s_lat @ ada_w[l] + ada_b[l])[:, None, :], N_MOD, axis=-1)
        m_ctx = jnp.split(s_ctx @ ada_w[l] + ada_b[l], N_MOD, axis=-1)
        h_lat = modulate(rms_norm(x_lat, norm1_g[l]), m_lat[0], m_lat[1])
        h_ctx = modulate(rms_norm(x_ctx, norm1_g[l]), m_ctx[0], m_ctx[1])
        y_lat, y_ctx = token_mixers(h_lat, h_ctx, w_in[l], cf_dw[l], cf_b[l], cf_ln_g[l], cf_ln_b[l],
                                    sc_dw[l], gla_w2[l], gla_b2[l], gla_ng[l], ret_logit[l], ret_ng[l],
                                    w_out[l], not last)
        x_lat = x_lat + m_lat[2] * y_lat
        h_lat = modulate(rms_norm(x_lat, norm2_g[l]), m_lat[3], m_lat[4])
        if last:
            f_lat = hierarchical_moe(h_lat.reshape(-1, d), w_grp[l], b_grp[l], w_rt[l], b_rt[l],
                                     e_w1[l], e_w3[l], e_w2[l]).reshape(bsz, seq, d)
        else:
            x_ctx = x_ctx + m_ctx[2] * y_ctx
            h_ctx = modulate(rms_norm(x_ctx, norm2_g[l]), m_ctx[3], m_ctx[4])
            tokens = jnp.concatenate([h_ctx.reshape(-1, d), h_lat.reshape(-1, d)], axis=0)
            f_all = hierarchical_moe(tokens, w_grp[l], b_grp[l], w_rt[l], b_rt[l],
                                     e_w1[l], e_w3[l], e_w2[l])
            x_ctx = x_ctx + m_ctx[5] * f_all[:n_ctx_tok].reshape(x_ctx.shape)
            f_lat = f_all[n_ctx_tok:].reshape(bsz, seq, d)
        x_lat = x_lat + m_lat[5] * f_lat
    return rms_norm(x_lat, final_g)
```

```python
import functools

import jax
import jax.numpy as jnp
from jax import lax
from jax.experimental import pallas as pl
from jax.experimental.pallas import tpu as pltpu

F32 = jnp.float32
BF16 = jnp.bfloat16

D_MODEL = 2048
GRID_W = 64
GROUP_W = D_MODEL // 4
CF_KERNEL = 31
SC_KERNEL = 3
N_HEADS = 4
HEAD_DK = 64
HEAD_DV = 128
GLA_RANK = 16
GLA_TAU = 16.0
CHUNK = 64
ROPE_BASE = 10000.0
N_GROUPS = 4
EXPERTS_PER_GROUP = 4
N_EXPERTS = N_GROUPS * EXPERTS_PER_GROUP
TOP_K = 2
EXPERT_HIDDEN = D_MODEL // 2
N_MOD = 6
EPS = 1e-6

TM = 256
LANES = 128
ADA_TN = 1024
MOE_TB = 512
MOE_HC = 256
VMEM_LIMIT = 56 * 1024 * 1024

ZC_W = 5 * GROUP_W
ZG_W = 3 * GROUP_W + LANES
ZR_W = 3 * GROUP_W
Z_W = ZC_W + ZG_W + ZR_W


def _cparams(sem):
    return pltpu.CompilerParams(dimension_semantics=sem, vmem_limit_bytes=VMEM_LIMIT)


def _sigmoid(x):
    return 1.0 / (1.0 + jnp.exp(-x))


def _silu(x):
    return x * _sigmoid(x)


def _log_sigmoid(x):
    return jnp.minimum(x, 0.0) - jnp.log1p(jnp.exp(-jnp.abs(x)))


def _dot(a, b):
    return jnp.dot(a, b, preferred_element_type=F32)


def _dot_nt(a, b):
    return lax.dot_general(a, b, (((1,), (1,)), ((), ())), preferred_element_type=F32)


def _dot_tn(a, b):
    return lax.dot_general(a, b, (((0,), (0,)), ((), ())), preferred_element_type=F32)


def _mod_row(i, tpb):
    return jnp.where(i % tpb == 0, 2, i // tpb)


def _ada_kernel(s_ref, w_ref, b_ref, o_ref):
    a = _silu(s_ref[...]).astype(BF16)
    o_ref[0] = _dot(a, w_ref[0].astype(BF16)) + b_ref[0]


def _ada_call(s8, ada_w, ada_b):
    depth, d, nm = ada_w.shape
    return pl.pallas_call(
        _ada_kernel,
        grid=(depth, nm // ADA_TN),
        in_specs=[
            pl.BlockSpec((8, d), lambda l, j: (0, 0)),
            pl.BlockSpec((1, d, ADA_TN), lambda l, j: (l, 0, j)),
            pl.BlockSpec((1, 1, ADA_TN), lambda l, j: (l, 0, j)),
        ],
        out_specs=pl.BlockSpec((1, 8, ADA_TN), lambda l, j: (l, 0, j)),
        out_shape=jax.ShapeDtypeStruct((depth, 8, nm), F32),
        compiler_params=_cparams(("arbitrary", "arbitrary")),
        name="adaln",
    )(s8, ada_w, ada_b.reshape(depth, 1, nm))


def _inproj_kernel(x_ref, mod_ref, g_ref, w_ref, zc_ref, zg_ref, zr_ref):
    x = x_ref[...]
    y = x * lax.rsqrt(jnp.mean(x * x, axis=-1, keepdims=True) + EPS) * g_ref[...]
    h = (y * (1.0 + mod_ref[0, 1:2, :]) + mod_ref[0, 0:1, :]).astype(BF16)
    off = 0
    for ref, width in ((zc_ref, ZC_W), (zg_ref, ZG_W), (zr_ref, ZR_W)):
        c = 0
        while c < width:
            step = min(512, width - c)
            ref[:, c:c + step] = _dot(h, w_ref[:, off + c:off + c + step]).astype(BF16)
            c += step
        off += width


def _inproj_call(x, mods, g1, w_packed, tpb):
    n, d = x.shape
    nt = n // TM
    return pl.pallas_call(
        _inproj_kernel,
        grid=(nt,),
        in_specs=[
            pl.BlockSpec((TM, d), lambda i: (i, 0)),
            pl.BlockSpec((1, N_MOD, d), lambda i: (_mod_row(i, tpb), 0, 0)),
            pl.BlockSpec((1, d), lambda i: (0, 0)),
            pl.BlockSpec((d, Z_W), lambda i: (0, 0), pipeline_mode=pl.Buffered(1)),
        ],
        out_specs=[
            pl.BlockSpec((TM, ZC_W), lambda i: (i, 0)),
            pl.BlockSpec((TM, ZG_W), lambda i: (i, 0)),
            pl.BlockSpec((TM, ZR_W), lambda i: (i, 0)),
        ],
        out_shape=[
            jax.ShapeDtypeStruct((n, ZC_W), BF16),
            jax.ShapeDtypeStruct((n, ZG_W), BF16),
            jax.ShapeDtypeStruct((n, ZR_W), BF16),
        ],
        compiler_params=_cparams(("arbitrary",)),
        name="inproj",
    )(x, mods, g1, w_packed)


PAD_LEAD = 16
SEG = GRID_W
LAT_STRIDE = SEG + PAD_LEAD
PAD_ROWS = (TM // SEG) * LAT_STRIDE + PAD_LEAD


def _conv_kernel(zc_ref, prev_ref, next_ref, cfw_ref, cfb_ref, lng_ref, lnb_ref, scw_ref, o_ref, pad_ref,
                 *, tpb):
    j = pl.program_id(0) % tpb
    nseg = TM // SEG
    half = CF_KERNEL // 2
    zeros_lead = jnp.zeros((PAD_LEAD, GROUP_W), F32)

    def glu():
        cfa = zc_ref[:, 3 * GROUP_W:4 * GROUP_W].astype(F32)
        cfg = zc_ref[:, 4 * GROUP_W:5 * GROUP_W].astype(F32)
        return cfa * _sigmoid(cfg)

    def finish_cf(acc, s):
        y = acc + cfb_ref[...]
        yc = y - jnp.mean(y, axis=-1, keepdims=True)
        yn = yc * lax.rsqrt(jnp.mean(yc * yc, axis=-1, keepdims=True) + EPS)
        o_ref[s * SEG:(s + 1) * SEG, 0:GROUP_W] = _silu(yn * lng_ref[...] + lnb_ref[...]).astype(BF16)

    def conformer(stride):
        for s in range(nseg):
            base = s * stride + PAD_LEAD - half
            acc = jnp.zeros((SEG, GROUP_W), F32)
            for k in range(CF_KERNEL):
                acc = acc + cfw_ref[k:k + 1, :] * pad_ref[base + k:base + k + SEG, :]
            finish_cf(acc, s)

    def sc_products():
        scc = zc_ref[:, 0:GROUP_W].astype(F32)
        scv = zc_ref[:, GROUP_W:2 * GROUP_W].astype(F32)
        scb = zc_ref[:, 2 * GROUP_W:3 * GROUP_W].astype(F32)
        return scc * scv, scb

    @pl.when(j != 0)
    def _latent():
        u = glu()
        for s in range(nseg):
            pad_ref[s * LAT_STRIDE:s * LAT_STRIDE + PAD_LEAD, :] = zeros_lead
            pad_ref[s * LAT_STRIDE + PAD_LEAD:(s + 1) * LAT_STRIDE, :] = u[s * SEG:(s + 1) * SEG]
        pad_ref[nseg * LAT_STRIDE:nseg * LAT_STRIDE + PAD_LEAD, :] = zeros_lead
        conformer(LAT_STRIDE)
        usc, scb = sc_products()
        up = prev_ref[:, 0:GROUP_W].astype(F32) * prev_ref[:, GROUP_W:2 * GROUP_W].astype(F32)
        un = next_ref[:, 0:GROUP_W].astype(F32) * next_ref[:, GROUP_W:2 * GROUP_W].astype(F32)
        up = jnp.where(j == 1, 0.0, up)
        un = jnp.where(j == tpb - 1, 0.0, un)
        above = jnp.concatenate([up, usc[:TM - GRID_W]], axis=0)
        below = jnp.concatenate([usc[GRID_W:], un], axis=0)
        y = scb * (scw_ref[0:1, :] * above + scw_ref[1:2, :] * usc + scw_ref[2:3, :] * below)
        o_ref[:, GROUP_W:2 * GROUP_W] = y.astype(BF16)

    @pl.when(j == 0)
    def _context():
        u = glu()
        pad_ref[0:PAD_LEAD, :] = zeros_lead
        pad_ref[PAD_LEAD:PAD_LEAD + TM, :] = u
        pad_ref[PAD_LEAD + TM:2 * PAD_LEAD + TM, :] = zeros_lead
        conformer(SEG)
        usc, scb = sc_products()
        pad_ref[PAD_LEAD:PAD_LEAD + TM, :] = usc
        before = pad_ref[PAD_LEAD - 1:PAD_LEAD - 1 + TM, :]
        after = pad_ref[PAD_LEAD + 1:PAD_LEAD + 1 + TM, :]
        y = scb * (scw_ref[0:1, :] * before + scw_ref[1:2, :] * usc + scw_ref[2:3, :] * after)
        o_ref[:, GROUP_W:2 * GROUP_W] = y.astype(BF16)


def _conv_call(zc, cfw, cfb, lng, lnb, scw, tpb):
    n = zc.shape[0]
    nt = n // TM
    r = TM // GRID_W
    nhalo = n // GRID_W
    return pl.pallas_call(
        functools.partial(_conv_kernel, tpb=tpb),
        grid=(nt,),
        in_specs=[
            pl.BlockSpec((TM, ZC_W), lambda i: (i, 0)),
            pl.BlockSpec((GRID_W, 2 * GROUP_W), lambda i: (jnp.maximum(i * r - 1, 0), 0)),
            pl.BlockSpec((GRID_W, 2 * GROUP_W), lambda i: (jnp.minimum(i * r + r, nhalo - 1), 0)),
            pl.BlockSpec((CF_KERNEL, GROUP_W), lambda i: (0, 0)),
            pl.BlockSpec((1, GROUP_W), lambda i: (0, 0)),
            pl.BlockSpec((1, GROUP_W), lambda i: (0, 0)),
            pl.BlockSpec((1, GROUP_W), lambda i: (0, 0)),
            pl.BlockSpec((SC_KERNEL, GROUP_W), lambda i: (0, 0)),
        ],
        out_specs=pl.BlockSpec((TM, 2 * GROUP_W), lambda i: (i, 0)),
        out_shape=jax.ShapeDtypeStruct((n, 2 * GROUP_W), BF16),
        scratch_shapes=[pltpu.VMEM((PAD_ROWS, GROUP_W), F32)],
        compiler_params=_cparams(("arbitrary",)),
        name="convmix",
    )(zc, zc, zc, cfw, cfb, lng, lnb, scw)


def _scan_block(d, j, tpb):
    return jnp.where(d == 0, j, jnp.where(j == 0, 0, tpb - j))


def _tri(rev):
    ri = lax.broadcasted_iota(jnp.int32, (CHUNK, CHUNK), 0)
    ci = lax.broadcasted_iota(jnp.int32, (CHUNK, CHUNK), 1)
    return (ri <= ci) if rev else (ri >= ci)


def _gla_kernel(zg_ref, w2_ref, b2_ref, o_ref, st_ref):
    d = pl.program_id(1)

    @pl.when(pl.program_id(2) == 0)
    def _init():
        st_ref[...] = jnp.zeros_like(st_ref)

    def run(rev):
        nk = N_HEADS * HEAD_DK
        zz = _dot(zg_ref[:, 3 * GROUP_W:3 * GROUP_W + LANES], w2_ref[0].astype(BF16)) + b2_ref[0]
        la = _log_sigmoid(zz) * (1.0 / GLA_TAU)
        tri = _tri(rev)
        trib = tri.astype(BF16)
        order = range(TM // CHUNK - 1, -1, -1) if rev else range(TM // CHUNK)
        for c in order:
            rows = slice(c * CHUNK, (c + 1) * CHUNK)
            la_c = la[rows]
            hi = la_c.astype(BF16)
            lo = (la_c - hi.astype(F32)).astype(BF16)
            bc = _dot(trib, hi) + _dot(trib, lo)
            b_last = bc[0:1] if rev else bc[CHUNK - 1:CHUNK]
            q = zg_ref[rows, 0:nk].astype(F32) * (HEAD_DK ** -0.5)
            k = zg_ref[rows, nk:2 * nk].astype(F32)
            q_in = (q * jnp.exp(bc)).astype(BF16)
            k_in = (k * jnp.exp(-bc)).astype(BF16)
            k_st = (k * jnp.exp(b_last - bc)).astype(BF16)
            a_row = jnp.exp(b_last)
            for h in range(N_HEADS):
                hs = slice(h * HEAD_DK, (h + 1) * HEAD_DK)
                vs = slice(h * HEAD_DV, (h + 1) * HEAD_DV)
                vh = zg_ref[rows, GROUP_W + h * HEAD_DV:GROUP_W + (h + 1) * HEAD_DV]
                sc = jnp.where(tri, _dot_nt(q_in[:, hs], k_in[:, hs]), 0.0).astype(BF16)
                st = st_ref[h]
                o_ref[0, rows, vs] = _dot(sc, vh) + _dot_nt(q_in[:, hs], st.astype(BF16))
                st_ref[h] = st * a_row[:, hs] + _dot_tn(vh, k_st[:, hs])

    @pl.when(d == 0)
    def _fwd():
        run(False)

    @pl.when(d == 1)
    def _bwd():
        run(True)


def _gla_call(zg, w2pad, b2, bsz, tpb):
    n = zg.shape[0]
    nk = N_HEADS * HEAD_DK

    def rowblk(b, d, j):
        return b * tpb + _scan_block(d, j, tpb)

    return pl.pallas_call(
        _gla_kernel,
        grid=(bsz, 2, tpb),
        in_specs=[
            pl.BlockSpec((TM, ZG_W), lambda b, d, j: (rowblk(b, d, j), 0)),
            pl.BlockSpec((1, LANES, nk), lambda b, d, j: (d, 0, 0)),
            pl.BlockSpec((1, 1, nk), lambda b, d, j: (d, 0, 0)),
        ],
        out_specs=pl.BlockSpec((1, TM, GROUP_W), lambda b, d, j: (d, rowblk(b, d, j), 0)),
        out_shape=jax.ShapeDtypeStruct((2, n, GROUP_W), F32),
        scratch_shapes=[pltpu.VMEM((N_HEADS, HEAD_DV, HEAD_DK), F32)],
        compiler_params=_cparams(("arbitrary", "arbitrary", "arbitrary")),
        name="gla",
    )(zg, w2pad, b2)


def _ret_kernel(zr_ref, cos_ref, sin_ref, lg_ref, o_ref, st_ref):
    d = pl.program_id(1)

    @pl.when(pl.program_id(2) == 0)
    def _init():
        st_ref[...] = jnp.zeros_like(st_ref)

    def run(rev):
        nk = N_HEADS * HEAD_DK
        lane = lax.broadcasted_iota(jnp.int32, (TM, nk), 1)
        first_half = (lane % HEAD_DK) < (HEAD_DK // 2)
        cosf = cos_ref[...]
        sinf = sin_ref[...]

        def rope(x):
            swapped = jnp.where(first_half, pltpu.roll(x, nk - HEAD_DK // 2, 1), pltpu.roll(x, HEAD_DK // 2, 1))
            return x * cosf + swapped * sinf

        q = rope(zr_ref[:, 0:nk].astype(F32)).astype(BF16)
        k = rope(zr_ref[:, nk:2 * nk].astype(F32)) * (HEAD_DK ** -0.5)
        lg = _log_sigmoid(lg_ref[0])
        tri = _tri(rev)
        ri = lax.broadcasted_iota(jnp.int32, (CHUNK, CHUNK), 0)
        ci = lax.broadcasted_iota(jnp.int32, (CHUNK, CHUNK), 1)
        rel = ((ci - ri) if rev else (ri - ci)).astype(F32)
        pos = lax.broadcasted_iota(jnp.int32, (CHUNK, LANES), 0).astype(F32)
        q_steps = (CHUNK - pos) if rev else (pos + 1.0)
        k_steps = pos if rev else (CHUNK - 1.0 - pos)
        order = range(TM // CHUNK - 1, -1, -1) if rev else range(TM // CHUNK)
        for h in range(N_HEADS):
            hs = slice(h * HEAD_DK, (h + 1) * HEAD_DK)
            vs = slice(h * HEAD_DV, (h + 1) * HEAD_DV)
            lgh = lg[h:h + 1, :]
            decay = jnp.where(tri, jnp.exp(rel * lgh[:, 0:CHUNK]), 0.0)
            q_decay = jnp.exp(q_steps * lgh)
            k_decay = jnp.exp(k_steps[:, 0:HEAD_DK] * lgh[:, 0:HEAD_DK])
            chunk_decay = jnp.exp(float(CHUNK) * lgh[:, 0:HEAD_DK])
            for c in order:
                rows = slice(c * CHUNK, (c + 1) * CHUNK)
                qh = q[rows, hs]
                kh = k[rows, hs]
                vh = zr_ref[rows, GROUP_W + h * HEAD_DV:GROUP_W + (h + 1) * HEAD_DV]
                sc = (_dot_nt(qh, kh.astype(BF16)) * decay).astype(BF16)
                st = st_ref[h]
                o_ref[0, rows, vs] = _dot(sc, vh) + _dot_nt(qh, st.astype(BF16)) * q_decay
                st_ref[h] = st * chunk_decay + _dot_tn(vh, (kh * k_decay).astype(BF16))

    @pl.when(d == 0)
    def _fwd():
        run(False)

    @pl.when(d == 1)
    def _bwd():
        run(True)


def _ret_call(zr, cos_t, sin_t, lg, bsz, tpb):
    n = zr.shape[0]
    nk = N_HEADS * HEAD_DK

    def rowblk(b, d, j):
        return b * tpb + _scan_block(d, j, tpb)

    return pl.pallas_call(
        _ret_kernel,
        grid=(bsz, 2, tpb),
        in_specs=[
            pl.BlockSpec((TM, ZR_W), lambda b, d, j: (rowblk(b, d, j), 0)),
            pl.BlockSpec((TM, nk), lambda b, d, j: (_scan_block(d, j, tpb), 0)),
            pl.BlockSpec((TM, nk), lambda b, d, j: (_scan_block(d, j, tpb), 0)),
            pl.BlockSpec((1, N_HEADS, LANES), lambda b, d, j: (d, 0, 0)),
        ],
        out_specs=pl.BlockSpec((1, TM, GROUP_W), lambda b, d, j: (d, rowblk(b, d, j), 0)),
        out_shape=jax.ShapeDtypeStruct((2, n, GROUP_W), F32),
        scratch_shapes=[pltpu.VMEM((N_HEADS, HEAD_DV, HEAD_DK), F32)],
        compiler_params=_cparams(("arbitrary", "arbitrary", "arbitrary")),
        name="retention",
    )(zr, cos_t, sin_t, lg)


def _head_norm(o, g, center):
    outs = []
    for h in range(N_HEADS):
        oh = o[:, h * HEAD_DV:(h + 1) * HEAD_DV]
        if center:
            oh = oh - jnp.mean(oh, axis=-1, keepdims=True)
        outs.append(oh * lax.rsqrt(jnp.mean(oh * oh, axis=-1, keepdims=True) + EPS))
    return jnp.concatenate(outs, axis=1) * g


def _route(logits):
    lane = lax.broadcasted_iota(jnp.int32, logits.shape, 1).astype(F32)
    neg = jnp.float32(-jnp.inf)
    big = jnp.float32(LANES)
    gmask = (lane >= N_EXPERTS) & (lane < N_EXPERTS + N_GROUPS)
    gl = jnp.where(gmask, logits, neg)
    gmax = jnp.max(gl, axis=-1, keepdims=True)
    gidx = jnp.min(jnp.where(gl == gmax, lane - N_EXPERTS, big), axis=-1, keepdims=True)
    g_w = 1.0 / jnp.sum(jnp.where(gmask, jnp.exp(logits - gmax), 0.0), axis=-1, keepdims=True)
    emask = (lane >= gidx * EXPERTS_PER_GROUP) & (lane < (gidx + 1.0) * EXPERTS_PER_GROUP)
    el = jnp.where(emask, logits, neg)
    e1 = jnp.max(el, axis=-1, keepdims=True)
    i1 = jnp.min(jnp.where(el == e1, lane, big), axis=-1, keepdims=True)
    el2 = jnp.where(lane == i1, neg, el)
    e2 = jnp.max(el2, axis=-1, keepdims=True)
    i2 = jnp.min(jnp.where(el2 == e2, lane, big), axis=-1, keepdims=True)
    r = jnp.exp(e2 - e1)
    w1 = g_w / (1.0 + r)
    w2 = g_w * r / (1.0 + r)
    idx = jnp.where(lane == 0.0, i1, jnp.where(lane == 1.0, i2, 0.0)).astype(jnp.int32)
    gate = jnp.where(lane == 0.0, w1, jnp.where(lane == 1.0, w2, 0.0))
    return idx, gate


def _outproj_kernel(mixc_ref, og_ref, or_ref, gr_ref, rg_ref, gng_ref, rng_ref, wout_ref, x_ref, mod_ref, g2_ref,
                    wr_ref, br_ref, xo_ref, h_ref, idx_ref, gate_ref):
    gla = _head_norm(og_ref[0] + og_ref[1], gng_ref[...], False) * _silu(gr_ref[...].astype(F32))
    ret = _head_norm(or_ref[0] + or_ref[1], rng_ref[...], True) * _silu(rg_ref[...].astype(F32))
    y = (_dot(mixc_ref[...], wout_ref[0:2 * GROUP_W, :])
         + _dot(gla.astype(BF16), wout_ref[2 * GROUP_W:3 * GROUP_W, :])
         + _dot(ret.astype(BF16), wout_ref[3 * GROUP_W:4 * GROUP_W, :]))
    x = x_ref[...] + mod_ref[0, 2:3, :] * y
    xo_ref[...] = x
    hn = x * lax.rsqrt(jnp.mean(x * x, axis=-1, keepdims=True) + EPS) * g2_ref[...]
    h = hn * (1.0 + mod_ref[0, 4:5, :]) + mod_ref[0, 3:4, :]
    h_ref[...] = h
    hb = h.astype(BF16)
    w = wr_ref[...]
    w_hi = w.astype(BF16)
    w_lo = (w - w_hi.astype(F32)).astype(BF16)
    h_lo = (h - hb.astype(F32)).astype(BF16)
    logits = _dot(hb, w_hi) + _dot(hb, w_lo) + _dot(h_lo, w_hi) + br_ref[...]
    idx, gate = _route(logits)
    idx_ref[...] = idx
    gate_ref[...] = gate


def _outproj_call(mixc, og, orr, zg, zr, gng, rng, wout, x, mods, g2, wr, br, tpb):
    n, d = x.shape
    nt = n // TM
    return pl.pallas_call(
        _outproj_kernel,
        grid=(nt,),
        in_specs=[
            pl.BlockSpec((TM, 2 * GROUP_W), lambda i: (i, 0)),
            pl.BlockSpec((2, TM, GROUP_W), lambda i: (0, i, 0)),
            pl.BlockSpec((2, TM, GROUP_W), lambda i: (0, i, 0)),
            pl.BlockSpec((TM, GROUP_W), lambda i: (i, 2)),
            pl.BlockSpec((TM, GROUP_W), lambda i: (i, 2)),
            pl.BlockSpec((1, GROUP_W), lambda i: (0, 0)),
            pl.BlockSpec((1, GROUP_W), lambda i: (0, 0)),
            pl.BlockSpec((d, d), lambda i: (0, 0), pipeline_mode=pl.Buffered(1)),
            pl.BlockSpec((TM, d), lambda i: (i, 0)),
            pl.BlockSpec((1, N_MOD, d), lambda i: (_mod_row(i, tpb), 0, 0)),
            pl.BlockSpec((1, d), lambda i: (0, 0)),
            pl.BlockSpec((d, LANES), lambda i: (0, 0)),
            pl.BlockSpec((1, LANES), lambda i: (0, 0)),
        ],
        out_specs=[
            pl.BlockSpec((TM, d), lambda i: (i, 0)),
            pl.BlockSpec((TM, d), lambda i: (i, 0)),
            pl.BlockSpec((TM, LANES), lambda i: (i, 0)),
            pl.BlockSpec((TM, LANES), lambda i: (i, 0)),
        ],
        out_shape=[
            jax.ShapeDtypeStruct((n, d), F32),
            jax.ShapeDtypeStruct((n, d), F32),
            jax.ShapeDtypeStruct((n, LANES), jnp.int32),
            jax.ShapeDtypeStruct((n, LANES), F32),
        ],
        compiler_params=_cparams(("arbitrary",)),
        name="outproj",
    )(mixc, og, orr, zg, zr, gng, rng, wout, x, mods, g2, wr, br)


def _moe_kernel(blk_e_ref, nvalid_ref, a_ref, gate_ref, h_hbm, w1_ref, w3_ref, w2_ref, y_hbm,
                xg_ref, xb_ref, acc_ref, sem_in, sem_out):
    i = pl.program_id(0)
    c = pl.program_id(1)
    nc = pl.num_programs(1)
    nv = nvalid_ref[i]
    active = nv > 0

    def gather_copy(r, tok):
        return pltpu.make_async_copy(h_hbm.at[pl.ds(tok, 1)], xg_ref.at[pl.ds(r, 1)], sem_in)

    def scatter_copy(r, row):
        return pltpu.make_async_copy(xg_ref.at[pl.ds(r, 1)], y_hbm.at[pl.ds(row, 1)], sem_out)

    @pl.when((i == 0) & (c == 0))
    def _init():
        xg_ref[...] = jnp.zeros_like(xg_ref)

    @pl.when(active & (c == 0))
    def _gather():
        def start(r, carry):
            tok = lax.shift_right_logical(a_ref[0, 0, r], 1)
            gather_copy(r, tok).start()
            return carry

        def wait(r, carry):
            gather_copy(r, 0).wait()
            return carry

        lax.fori_loop(0, nv, start, 0)
        lax.fori_loop(0, nv, wait, 0)
        xb_ref[...] = xg_ref[...].astype(BF16)

    @pl.when(active)
    def _compute():
        x = xb_ref[...]
        h1 = _dot(x, w1_ref[0].astype(BF16))
        h3 = _dot(x, w3_ref[0].astype(BF16))
        part = _dot((_silu(h1) * h3).astype(BF16), w2_ref[0].astype(BF16))

        @pl.when(c == 0)
        def _first():
            acc_ref[...] = part

        @pl.when(c != 0)
        def _rest():
            acc_ref[...] += part

    @pl.when(active & (c == nc - 1))
    def _scatter():
        xg_ref[...] = acc_ref[...] * gate_ref[0]

        def start(r, carry):
            scatter_copy(r, a_ref[0, 0, r]).start()
            return carry

        def wait(r, carry):
            scatter_copy(r, 0).wait()
            return carry

        lax.fori_loop(0, nv, start, 0)
        lax.fori_loop(0, nv, wait, 0)


def _moe_call(blk_e, nvalid, slot_a, slot_w, h, w1, w3, w2):
    n, d = h.shape
    nb = slot_a.shape[0]
    hidden = w1.shape[-1]
    nc = hidden // MOE_HC
    def chunk(i, c, nv):
        return jnp.where(nv[i] > 0, c, nc - 1)

    grid_spec = pltpu.PrefetchScalarGridSpec(
        num_scalar_prefetch=2,
        grid=(nb, nc),
        in_specs=[
            pl.BlockSpec((1, 1, MOE_TB), lambda i, c, be, nv: (i, 0, 0), memory_space=pltpu.SMEM),
            pl.BlockSpec((1, MOE_TB, 1), lambda i, c, be, nv: (i, 0, 0)),
            pl.BlockSpec(memory_space=pl.ANY),
            pl.BlockSpec((1, d, MOE_HC), lambda i, c, be, nv: (be[i], 0, chunk(i, c, nv))),
            pl.BlockSpec((1, d, MOE_HC), lambda i, c, be, nv: (be[i], 0, chunk(i, c, nv))),
            pl.BlockSpec((1, MOE_HC, d), lambda i, c, be, nv: (be[i], chunk(i, c, nv), 0)),
        ],
        out_specs=pl.BlockSpec(memory_space=pl.ANY),
        scratch_shapes=[
            pltpu.VMEM((MOE_TB, d), F32),
            pltpu.VMEM((MOE_TB, d), BF16),
            pltpu.VMEM((MOE_TB, d), F32),
            pltpu.SemaphoreType.DMA(()),
            pltpu.SemaphoreType.DMA(()),
        ],
    )
    return pl.pallas_call(
        _moe_kernel,
        grid_spec=grid_spec,
        out_shape=jax.ShapeDtypeStruct((TOP_K * n, d), F32),
        compiler_params=_cparams(("arbitrary", "arbitrary")),
        name="experts",
    )(blk_e, nvalid, slot_a, slot_w, h, w1, w3, w2)


def _slot_plan(idx, gate):
    n = idx.shape[0]
    n_asg = n * TOP_K
    flat_e = idx[:, :TOP_K].reshape(n_asg)
    flat_w = gate[:, :TOP_K].reshape(n_asg)
    order = jnp.argsort(flat_e).astype(jnp.int32)
    counts = jnp.sum(flat_e[:, None] == jnp.arange(N_EXPERTS, dtype=jnp.int32)[None, :], axis=0, dtype=jnp.int32)
    padded = (counts + MOE_TB - 1) // MOE_TB * MOE_TB
    pad_end = jnp.cumsum(padded)
    pad_start = pad_end - padded
    start = jnp.cumsum(counts) - counts
    n_slots = (n_asg + MOE_TB - 1) // MOE_TB * MOE_TB + N_EXPERTS * MOE_TB
    nb = n_slots // MOE_TB
    blk0 = jnp.arange(nb, dtype=jnp.int32) * MOE_TB
    blk_e = jnp.minimum(jnp.searchsorted(pad_end, blk0, side='right'), N_EXPERTS - 1).astype(jnp.int32)
    nvalid = jnp.clip(counts[blk_e] - (blk0 - pad_start[blk_e]), 0, MOE_TB).astype(jnp.int32)
    blk_w = jnp.where(nvalid > 0, blk_e, jnp.max(jnp.where(nvalid > 0, blk_e, 0)))
    slot = jnp.arange(n_slots, dtype=jnp.int32)
    e_s = blk_e[slot // MOE_TB]
    off = slot - pad_start[e_s]
    valid = off < counts[e_s]
    asg = order[jnp.clip(start[e_s] + off, 0, n_asg - 1)]
    slot_a = jnp.where(valid, asg, 0).astype(jnp.int32)
    slot_w = jnp.where(valid, flat_w[asg], 0.0)
    return blk_w, nvalid, slot_a.reshape(nb, 1, MOE_TB), slot_w.reshape(nb, MOE_TB, 1)


def _combine_kernel(x_ref, y_ref, mod_ref, o_ref):
    d = x_ref.shape[-1]
    o_ref[...] = x_ref[...] + mod_ref[0, 5:6, :] * (y_ref[:, 0:d] + y_ref[:, d:2 * d])


def _combine_call(x, y2, mods, tpb):
    n, d = x.shape
    return pl.pallas_call(
        _combine_kernel,
        grid=(n // TM,),
        in_specs=[
            pl.BlockSpec((TM, d), lambda i: (i, 0)),
            pl.BlockSpec((TM, TOP_K * d), lambda i: (i, 0)),
            pl.BlockSpec((1, N_MOD, d), lambda i: (_mod_row(i, tpb), 0, 0)),
        ],
        out_specs=pl.BlockSpec((TM, d), lambda i: (i, 0)),
        out_shape=jax.ShapeDtypeStruct((n, d), F32),
        compiler_params=_cparams(("arbitrary",)),
        name="combine",
    )(x, y2, mods)


def _final_kernel(x_ref, y_ref, mod_ref, g_ref, o_ref):
    d = x_ref.shape[-1]
    x = x_ref[...] + mod_ref[0, 5:6, :] * (y_ref[:, 0:d] + y_ref[:, d:2 * d])
    o_ref[0] = x * lax.rsqrt(jnp.mean(x * x, axis=-1, keepdims=True) + EPS) * g_ref[...]


def _final_call(x, y2, mods, gf, bsz, tpb):
    n, d = x.shape
    lat = tpb - 1
    return pl.pallas_call(
        _final_kernel,
        grid=(bsz, lat),
        in_specs=[
            pl.BlockSpec((TM, d), lambda b, j: (b * tpb + j + 1, 0)),
            pl.BlockSpec((TM, TOP_K * d), lambda b, j: (b * tpb + j + 1, 0)),
            pl.BlockSpec((1, N_MOD, d), lambda b, j: (b, 0, 0)),
            pl.BlockSpec((1, d), lambda b, j: (0, 0)),
        ],
        out_specs=pl.BlockSpec((1, TM, d), lambda b, j: (b, j, 0)),
        out_shape=jax.ShapeDtypeStruct((bsz, lat * TM, d), F32),
        compiler_params=_cparams(("arbitrary", "arbitrary")),
        name="final_norm",
    )(x, y2, mods, gf)


def _pack_w_in(w_in):
    g = GROUP_W
    nk = N_HEADS * HEAD_DK
    o = 0
    cols = {}
    for name, width in (("cfa", g), ("cfg", g), ("scb", g), ("scc", g), ("scv", g), ("gq", nk), ("gk", nk),
                        ("gv", g), ("gr", g), ("glr", 2 * GLA_RANK), ("rq", nk), ("rk", nk), ("rv", g), ("rg", g)):
        cols[name] = w_in[..., o:o + width]
        o += width
    lr_pad = jnp.zeros(w_in.shape[:-1] + (LANES - 2 * GLA_RANK,), w_in.dtype)
    packed = [cols[k] for k in ("scc", "scv", "scb", "cfa", "cfg", "gq", "gk", "gv", "gr", "glr")]
    packed += [lr_pad] + [cols[k] for k in ("rq", "rk", "rv", "rg")]
    return jnp.concatenate(packed, axis=-1).astype(BF16)


def _rope_tables(seq):
    n_freq = HEAD_DK // 4
    t = jnp.arange(seq)
    inv = ROPE_BASE ** (-jnp.arange(n_freq, dtype=F32) / n_freq)
    ang = jnp.concatenate([(t // GRID_W).astype(F32)[:, None] * inv, (t % GRID_W).astype(F32)[:, None] * inv], axis=-1)
    cos = jnp.concatenate([jnp.ones((TM, HEAD_DK // 2), F32), jnp.cos(ang)], axis=0)
    sin = jnp.concatenate([jnp.zeros((TM, HEAD_DK // 2), F32), jnp.sin(ang)], axis=0)
    cos_t = jnp.tile(jnp.concatenate([cos, cos], axis=-1), (1, N_HEADS))
    sin_t = jnp.tile(jnp.concatenate([-sin, sin], axis=-1), (1, N_HEADS))
    return cos_t, sin_t


def kernel(x, c, ctx, c_ctx, norm1_g, norm2_g, ada_w, ada_b, w_in, cf_dw, cf_b, cf_ln_g, cf_ln_b, sc_dw, gla_w2,
           gla_b2, gla_ng, ret_logit, ret_ng, w_out, w_grp, b_grp, w_rt, b_rt, e_w1, e_w3, e_w2, final_g):
    bsz, seq, d = x.shape
    depth = w_in.shape[0]
    assert d == D_MODEL and ctx.shape[1] == TM and seq % TM == 0 and bsz == 2
    tpb = 1 + seq // TM
    n = bsz * tpb * TM
    nk = N_HEADS * HEAD_DK

    s8 = jnp.concatenate([c, c_ctx[None, :], jnp.zeros((8 - bsz - 1, d), F32)], axis=0)
    mods_all = _ada_call(s8, ada_w, ada_b)[:, :bsz + 1, :].reshape(depth, bsz + 1, N_MOD, d)

    w_in_p = _pack_w_in(w_in)
    w_out_b = w_out.astype(BF16)
    cos_t, sin_t = _rope_tables(seq)
    w2pad = jnp.zeros((depth, 2, LANES, nk), F32)
    w2pad = w2pad.at[:, 0, 0:GLA_RANK, :].set(gla_w2[:, 0]).at[:, 1, GLA_RANK:2 * GLA_RANK, :].set(gla_w2[:, 1])
    wr_all = jnp.concatenate([w_rt, w_grp, jnp.zeros((depth, d, LANES - N_EXPERTS - N_GROUPS), F32)], axis=-1)
    br_all = jnp.concatenate([b_rt, b_grp, jnp.zeros((depth, LANES - N_EXPERTS - N_GROUPS), F32)], axis=-1)

    xa = jnp.concatenate([ctx, x], axis=1).reshape(n, d)
    out = None
    for l in range(depth):
        mods = mods_all[l]
        zc, zg, zr = _inproj_call(xa, mods, norm1_g[l][None, :], w_in_p[l], tpb)
        mixc = _conv_call(zc, cf_dw[l], cf_b[l][None, :], cf_ln_g[l][None, :], cf_ln_b[l][None, :], sc_dw[l], tpb)
        og = _gla_call(zg, w2pad[l], gla_b2[l][:, None, :], bsz, tpb)
        lg = jnp.broadcast_to(ret_logit[l][:, :, None], (2, N_HEADS, LANES))
        orr = _ret_call(zr, cos_t, sin_t, lg, bsz, tpb)
        xa, h2, idx, gate = _outproj_call(mixc, og, orr, zg, zr, gla_ng[l][None, :], ret_ng[l][None, :], w_out_b[l],
                                          xa, mods, norm2_g[l][None, :], wr_all[l], br_all[l][None, :], tpb)
        blk_e, nvalid, slot_a, slot_w = _slot_plan(idx, gate)
        y2 = _moe_call(blk_e, nvalid, slot_a, slot_w, h2, e_w1[l], e_w3[l], e_w2[l]).reshape(n, TOP_K * d)
        if l == depth - 1:
            out = _final_call(xa, y2, mods, final_g[None, :], bsz, tpb)
        else:
            xa = _combine_call(xa, y2, mods, tpb)
    return out
```

```python
import functools

import jax
import jax.numpy as jnp
from jax import lax
from jax.experimental import pallas as pl
from jax.experimental.pallas import tpu as pltpu
from jax.experimental.pallas import tpu_sc as plsc

F32 = jnp.float32
BF16 = jnp.bfloat16

D_MODEL = 2048
GRID_W = 64
GROUP_W = D_MODEL // 4
CF_KERNEL = 31
SC_KERNEL = 3
N_HEADS = 4
HEAD_DK = 64
HEAD_DV = 128
GLA_RANK = 16
GLA_TAU = 16.0
CHUNK = 64
ROPE_BASE = 10000.0
N_GROUPS = 4
EXPERTS_PER_GROUP = 4
N_EXPERTS = N_GROUPS * EXPERTS_PER_GROUP
TOP_K = 2
EXPERT_HIDDEN = D_MODEL // 2
N_MOD = 6
EPS = 1e-6

TM = 256
LANES = 128
ADA_TN = 1024
MOE_TB = 512
VMEM_LIMIT = 56 * 1024 * 1024
SC_WORKERS = 32
SC_WINDOW = 16

ZC_W = 5 * GROUP_W
ZG_W = 3 * GROUP_W + LANES
ZR_W = 3 * GROUP_W
Z_W = ZC_W + ZG_W + ZR_W
IN_W = Z_W - (LANES - 2 * GLA_RANK)
GLR_END = ZC_W + 3 * GROUP_W + 2 * GLA_RANK
PACK_MOVES = (
    (3 * GROUP_W, 0, 2 * GROUP_W),
    (2 * GROUP_W, 2 * GROUP_W, GROUP_W),
    (0, 3 * GROUP_W, 2 * GROUP_W),
    (ZC_W, ZC_W, GLR_END - ZC_W),
    (GLR_END, ZC_W + ZG_W, ZR_W),
)


def _cparams(sem):
    return pltpu.CompilerParams(dimension_semantics=sem, vmem_limit_bytes=VMEM_LIMIT)


def _sigmoid(x):
    return 1.0 / (1.0 + jnp.exp(-x))


def _silu(x):
    return x * _sigmoid(x)


def _log_sigmoid(x):
    return jnp.minimum(x, 0.0) - jnp.log1p(jnp.exp(-jnp.abs(x)))


def _dot(a, b):
    return jnp.dot(a, b, preferred_element_type=F32)


def _dot_nt(a, b):
    return lax.dot_general(a, b, (((1,), (1,)), ((), ())), preferred_element_type=F32)


def _dot_tn(a, b):
    return lax.dot_general(a, b, (((0,), (0,)), ((), ())), preferred_element_type=F32)


def _mod_row(i, tpb):
    return jnp.where(i % tpb == 0, 2, i // tpb)


def _pack_kernel(w_ref, o_ref):
    for src, dst, width in PACK_MOVES:
        o_ref[0, :, dst:dst + width] = w_ref[0, :, src:src + width].astype(BF16)
    o_ref[0, :, GLR_END:ZC_W + ZG_W] = jnp.zeros((o_ref.shape[1], ZC_W + ZG_W - GLR_END), BF16)


def _pack_w_in(w_in):
    depth, d, _ = w_in.shape
    return pl.pallas_call(
        _pack_kernel,
        grid=(depth, d // TM),
        in_specs=[pl.BlockSpec((1, TM, IN_W), lambda l, i: (l, i, 0))],
        out_specs=pl.BlockSpec((1, TM, Z_W), lambda l, i: (l, i, 0)),
        out_shape=jax.ShapeDtypeStruct((depth, d, Z_W), BF16),
        compiler_params=_cparams(("arbitrary", "arbitrary")),
        name="pack_w_in",
    )(w_in)


def _cast_kernel(w_ref, o_ref):
    o_ref[...] = w_ref[...].astype(BF16)


def _cast_w_out(w_out):
    depth, k, d = w_out.shape
    return pl.pallas_call(
        _cast_kernel,
        grid=(depth, k // TM),
        in_specs=[pl.BlockSpec((1, TM, d), lambda l, i: (l, i, 0))],
        out_specs=pl.BlockSpec((1, TM, d), lambda l, i: (l, i, 0)),
        out_shape=jax.ShapeDtypeStruct((depth, k, d), BF16),
        compiler_params=_cparams(("arbitrary", "arbitrary")),
        name="cast_w_out",
    )(w_out)


def _ada_kernel(s_ref, w_ref, b_ref, o_ref):
    a = _silu(s_ref[...]).astype(BF16)
    o_ref[0] = _dot(a, w_ref[0].astype(BF16)) + b_ref[0]


def _ada_call(s8, ada_w, ada_b):
    depth, d, nm = ada_w.shape
    return pl.pallas_call(
        _ada_kernel,
        grid=(depth, nm // ADA_TN),
        in_specs=[
            pl.BlockSpec((8, d), lambda l, j: (0, 0)),
            pl.BlockSpec((1, d, ADA_TN), lambda l, j: (l, 0, j)),
            pl.BlockSpec((1, 1, ADA_TN), lambda l, j: (l, 0, j)),
        ],
        out_specs=pl.BlockSpec((1, 8, ADA_TN), lambda l, j: (l, 0, j)),
        out_shape=jax.ShapeDtypeStruct((depth, 8, nm), F32),
        compiler_params=_cparams(("arbitrary", "arbitrary")),
        name="adaln",
    )(s8, ada_w, ada_b.reshape(depth, 1, nm))


def _inproj_kernel(x_ref, mod_ref, g_ref, w_ref, zc_ref, zg_ref, zr_ref):
    x = x_ref[...]
    y = x * lax.rsqrt(jnp.mean(x * x, axis=-1, keepdims=True) + EPS) * g_ref[...]
    h = (y * (1.0 + mod_ref[0, 1:2, :]) + mod_ref[0, 0:1, :]).astype(BF16)
    off = 0
    for ref, width in ((zc_ref, ZC_W), (zg_ref, ZG_W), (zr_ref, ZR_W)):
        c = 0
        while c < width:
            step = min(512, width - c)
            ref[:, c:c + step] = _dot(h, w_ref[0, :, off + c:off + c + step]).astype(BF16)
            c += step
        off += width


def _inproj_call(x, mods, g1, w_packed, l, tpb):
    n, d = x.shape
    nt = n // TM
    return pl.pallas_call(
        _inproj_kernel,
        grid=(nt,),
        in_specs=[
            pl.BlockSpec((TM, d), lambda i: (i, 0)),
            pl.BlockSpec((1, N_MOD, d), lambda i: (_mod_row(i, tpb), 0, 0)),
            pl.BlockSpec((1, d), lambda i: (0, 0)),
            pl.BlockSpec((1, d, Z_W), lambda i: (l, 0, 0), pipeline_mode=pl.Buffered(1)),
        ],
        out_specs=[
            pl.BlockSpec((TM, ZC_W), lambda i: (i, 0)),
            pl.BlockSpec((TM, ZG_W), lambda i: (i, 0)),
            pl.BlockSpec((TM, ZR_W), lambda i: (i, 0)),
        ],
        out_shape=[
            jax.ShapeDtypeStruct((n, ZC_W), BF16),
            jax.ShapeDtypeStruct((n, ZG_W), BF16),
            jax.ShapeDtypeStruct((n, ZR_W), BF16),
        ],
        compiler_params=_cparams(("arbitrary",)),
        name="inproj",
    )(x, mods, g1, w_packed)


PAD_LEAD = 16
SEG = GRID_W
LAT_STRIDE = SEG + PAD_LEAD
PAD_ROWS = (TM // SEG) * LAT_STRIDE + PAD_LEAD


def _conv_kernel(zc_ref, prev_ref, next_ref, cfw_ref, cfb_ref, lng_ref, lnb_ref, scw_ref, o_ref, pad_ref,
                 *, tpb):
    j = pl.program_id(0) % tpb
    nseg = TM // SEG
    half = CF_KERNEL // 2
    zeros_lead = jnp.zeros((PAD_LEAD, GROUP_W), F32)

    def glu():
        cfa = zc_ref[:, 3 * GROUP_W:4 * GROUP_W].astype(F32)
        cfg = zc_ref[:, 4 * GROUP_W:5 * GROUP_W].astype(F32)
        return cfa * _sigmoid(cfg)

    def finish_cf(acc, s):
        y = acc + cfb_ref[...]
        yc = y - jnp.mean(y, axis=-1, keepdims=True)
        yn = yc * lax.rsqrt(jnp.mean(yc * yc, axis=-1, keepdims=True) + EPS)
        o_ref[s * SEG:(s + 1) * SEG, 0:GROUP_W] = _silu(yn * lng_ref[...] + lnb_ref[...]).astype(BF16)

    def conformer(stride):
        for s in range(nseg):
            base = s * stride + PAD_LEAD - half
            acc = jnp.zeros((SEG, GROUP_W), F32)
            for k in range(CF_KERNEL):
                acc = acc + cfw_ref[k:k + 1, :] * pad_ref[base + k:base + k + SEG, :]
            finish_cf(acc, s)

    def sc_products():
        scc = zc_ref[:, 0:GROUP_W].astype(F32)
        scv = zc_ref[:, GROUP_W:2 * GROUP_W].astype(F32)
        scb = zc_ref[:, 2 * GROUP_W:3 * GROUP_W].astype(F32)
        return scc * scv, scb

    @pl.when(j != 0)
    def _latent():
        u = glu()
        for s in range(nseg):
            pad_ref[s * LAT_STRIDE:s * LAT_STRIDE + PAD_LEAD, :] = zeros_lead
            pad_ref[s * LAT_STRIDE + PAD_LEAD:(s + 1) * LAT_STRIDE, :] = u[s * SEG:(s + 1) * SEG]
        pad_ref[nseg * LAT_STRIDE:nseg * LAT_STRIDE + PAD_LEAD, :] = zeros_lead
        conformer(LAT_STRIDE)
        usc, scb = sc_products()
        up = prev_ref[:, 0:GROUP_W].astype(F32) * prev_ref[:, GROUP_W:2 * GROUP_W].astype(F32)
        un = next_ref[:, 0:GROUP_W].astype(F32) * next_ref[:, GROUP_W:2 * GROUP_W].astype(F32)
        up = jnp.where(j == 1, 0.0, up)
        un = jnp.where(j == tpb - 1, 0.0, un)
        above = jnp.concatenate([up, usc[:TM - GRID_W]], axis=0)
        below = jnp.concatenate([usc[GRID_W:], un], axis=0)
        y = scb * (scw_ref[0:1, :] * above + scw_ref[1:2, :] * usc + scw_ref[2:3, :] * below)
        o_ref[:, GROUP_W:2 * GROUP_W] = y.astype(BF16)

    @pl.when(j == 0)
    def _context():
        u = glu()
        pad_ref[0:PAD_LEAD, :] = zeros_lead
        pad_ref[PAD_LEAD:PAD_LEAD + TM, :] = u
        pad_ref[PAD_LEAD + TM:2 * PAD_LEAD + TM, :] = zeros_lead
        conformer(SEG)
        usc, scb = sc_products()
        pad_ref[PAD_LEAD:PAD_LEAD + TM, :] = usc
        before = pad_ref[PAD_LEAD - 1:PAD_LEAD - 1 + TM, :]
        after = pad_ref[PAD_LEAD + 1:PAD_LEAD + 1 + TM, :]
        y = scb * (scw_ref[0:1, :] * before + scw_ref[1:2, :] * usc + scw_ref[2:3, :] * after)
        o_ref[:, GROUP_W:2 * GROUP_W] = y.astype(BF16)


def _conv_call(zc, cfw, cfb, lng, lnb, scw, tpb):
    n = zc.shape[0]
    nt = n // TM
    r = TM // GRID_W
    nhalo = n // GRID_W
    return pl.pallas_call(
        functools.partial(_conv_kernel, tpb=tpb),
        grid=(nt,),
        in_specs=[
            pl.BlockSpec((TM, ZC_W), lambda i: (i, 0)),
            pl.BlockSpec((GRID_W, 2 * GROUP_W), lambda i: (jnp.maximum(i * r - 1, 0), 0)),
            pl.BlockSpec((GRID_W, 2 * GROUP_W), lambda i: (jnp.minimum(i * r + r, nhalo - 1), 0)),
            pl.BlockSpec((CF_KERNEL, GROUP_W), lambda i: (0, 0)),
            pl.BlockSpec((1, GROUP_W), lambda i: (0, 0)),
            pl.BlockSpec((1, GROUP_W), lambda i: (0, 0)),
            pl.BlockSpec((1, GROUP_W), lambda i: (0, 0)),
            pl.BlockSpec((SC_KERNEL, GROUP_W), lambda i: (0, 0)),
        ],
        out_specs=pl.BlockSpec((TM, 2 * GROUP_W), lambda i: (i, 0)),
        out_shape=jax.ShapeDtypeStruct((n, 2 * GROUP_W), BF16),
        scratch_shapes=[pltpu.VMEM((PAD_ROWS, GROUP_W), F32)],
        compiler_params=_cparams(("arbitrary",)),
        name="convmix",
    )(zc, zc, zc, cfw, cfb, lng, lnb, scw)


def _scan_block(d, j, tpb):
    return jnp.where(d == 0, j, jnp.where(j == 0, 0, tpb - j))


def _tri(rev):
    ri = lax.broadcasted_iota(jnp.int32, (CHUNK, CHUNK), 0)
    ci = lax.broadcasted_iota(jnp.int32, (CHUNK, CHUNK), 1)
    return (ri <= ci) if rev else (ri >= ci)


def _gla_kernel(zg_ref, w2_ref, b2_ref, o_ref, st_ref):
    d = pl.program_id(1)

    @pl.when(pl.program_id(2) == 0)
    def _init():
        st_ref[...] = jnp.zeros_like(st_ref)

    def run(rev):
        nk = N_HEADS * HEAD_DK
        zz = _dot(zg_ref[:, 3 * GROUP_W:3 * GROUP_W + LANES], w2_ref[0].astype(BF16)) + b2_ref[0]
        la = _log_sigmoid(zz) * (1.0 / GLA_TAU)
        tri = _tri(rev)
        trib = tri.astype(BF16)
        order = range(TM // CHUNK - 1, -1, -1) if rev else range(TM // CHUNK)
        for c in order:
            rows = slice(c * CHUNK, (c + 1) * CHUNK)
            la_c = la[rows]
            hi = la_c.astype(BF16)
            lo = (la_c - hi.astype(F32)).astype(BF16)
            bc = _dot(trib, hi) + _dot(trib, lo)
            b_last = bc[0:1] if rev else bc[CHUNK - 1:CHUNK]
            q = zg_ref[rows, 0:nk].astype(F32) * (HEAD_DK ** -0.5)
            k = zg_ref[rows, nk:2 * nk].astype(F32)
            q_in = (q * jnp.exp(bc)).astype(BF16)
            k_in = (k * jnp.exp(-bc)).astype(BF16)
            k_st = (k * jnp.exp(b_last - bc)).astype(BF16)
            a_row = jnp.exp(b_last)
            for h in range(N_HEADS):
                hs = slice(h * HEAD_DK, (h + 1) * HEAD_DK)
                vs = slice(h * HEAD_DV, (h + 1) * HEAD_DV)
                vh = zg_ref[rows, GROUP_W + h * HEAD_DV:GROUP_W + (h + 1) * HEAD_DV]
                sc = jnp.where(tri, _dot_nt(q_in[:, hs], k_in[:, hs]), 0.0).astype(BF16)
                st = st_ref[h]
                o_ref[0, rows, vs] = _dot(sc, vh) + _dot_nt(q_in[:, hs], st.astype(BF16))
                st_ref[h] = st * a_row[:, hs] + _dot_tn(vh, k_st[:, hs])

    @pl.when(d == 0)
    def _fwd():
        run(False)

    @pl.when(d == 1)
    def _bwd():
        run(True)


def _gla_call(zg, w2pad, b2, bsz, tpb):
    n = zg.shape[0]
    nk = N_HEADS * HEAD_DK

    def rowblk(b, d, j):
        return b * tpb + _scan_block(d, j, tpb)

    return pl.pallas_call(
        _gla_kernel,
        grid=(bsz, 2, tpb),
        in_specs=[
            pl.BlockSpec((TM, ZG_W), lambda b, d, j: (rowblk(b, d, j), 0)),
            pl.BlockSpec((1, LANES, nk), lambda b, d, j: (d, 0, 0)),
            pl.BlockSpec((1, 1, nk), lambda b, d, j: (d, 0, 0)),
        ],
        out_specs=pl.BlockSpec((1, TM, GROUP_W), lambda b, d, j: (d, rowblk(b, d, j), 0)),
        out_shape=jax.ShapeDtypeStruct((2, n, GROUP_W), F32),
        scratch_shapes=[pltpu.VMEM((N_HEADS, HEAD_DV, HEAD_DK), F32)],
        compiler_params=_cparams(("arbitrary", "arbitrary", "arbitrary")),
        name="gla",
    )(zg, w2pad, b2)


def _ret_kernel(zr_ref, cos_ref, sin_ref, lg_ref, o_ref, st_ref):
    d = pl.program_id(1)

    @pl.when(pl.program_id(2) == 0)
    def _init():
        st_ref[...] = jnp.zeros_like(st_ref)

    def run(rev):
        nk = N_HEADS * HEAD_DK
        lane = lax.broadcasted_iota(jnp.int32, (TM, nk), 1)
        first_half = (lane % HEAD_DK) < (HEAD_DK // 2)
        cosf = cos_ref[...]
        sinf = sin_ref[...]

        def rope(x):
            swapped = jnp.where(first_half, pltpu.roll(x, nk - HEAD_DK // 2, 1), pltpu.roll(x, HEAD_DK // 2, 1))
            return x * cosf + swapped * sinf

        q = rope(zr_ref[:, 0:nk].astype(F32)).astype(BF16)
        k = rope(zr_ref[:, nk:2 * nk].astype(F32)) * (HEAD_DK ** -0.5)
        lg = _log_sigmoid(lg_ref[0])
        tri = _tri(rev)
        ri = lax.broadcasted_iota(jnp.int32, (CHUNK, CHUNK), 0)
        ci = lax.broadcasted_iota(jnp.int32, (CHUNK, CHUNK), 1)
        rel = ((ci - ri) if rev else (ri - ci)).astype(F32)
        pos = lax.broadcasted_iota(jnp.int32, (CHUNK, LANES), 0).astype(F32)
        q_steps = (CHUNK - pos) if rev else (pos + 1.0)
        k_steps = pos if rev else (CHUNK - 1.0 - pos)
        order = range(TM // CHUNK - 1, -1, -1) if rev else range(TM // CHUNK)
        for h in range(N_HEADS):
            hs = slice(h * HEAD_DK, (h + 1) * HEAD_DK)
            vs = slice(h * HEAD_DV, (h + 1) * HEAD_DV)
            lgh = lg[h:h + 1, :]
            decay = jnp.where(tri, jnp.exp(rel * lgh[:, 0:CHUNK]), 0.0)
            q_decay = jnp.exp(q_steps * lgh)
            k_decay = jnp.exp(k_steps[:, 0:HEAD_DK] * lgh[:, 0:HEAD_DK])
            chunk_decay = jnp.exp(float(CHUNK) * lgh[:, 0:HEAD_DK])
            for c in order:
                rows = slice(c * CHUNK, (c + 1) * CHUNK)
                qh = q[rows, hs]
                kh = k[rows, hs]
                vh = zr_ref[rows, GROUP_W + h * HEAD_DV:GROUP_W + (h + 1) * HEAD_DV]
                sc = (_dot_nt(qh, kh.astype(BF16)) * decay).astype(BF16)
                st = st_ref[h]
                o_ref[0, rows, vs] = _dot(sc, vh) + _dot_nt(qh, st.astype(BF16)) * q_decay
                st_ref[h] = st * chunk_decay + _dot_tn(vh, (kh * k_decay).astype(BF16))

    @pl.when(d == 0)
    def _fwd():
        run(False)

    @pl.when(d == 1)
    def _bwd():
        run(True)


def _ret_call(zr, cos_t, sin_t, lg, bsz, tpb):
    n = zr.shape[0]
    nk = N_HEADS * HEAD_DK

    def rowblk(b, d, j):
        return b * tpb + _scan_block(d, j, tpb)

    return pl.pallas_call(
        _ret_kernel,
        grid=(bsz, 2, tpb),
        in_specs=[
            pl.BlockSpec((TM, ZR_W), lambda b, d, j: (rowblk(b, d, j), 0)),
            pl.BlockSpec((TM, nk), lambda b, d, j: (_scan_block(d, j, tpb), 0)),
            pl.BlockSpec((TM, nk), lambda b, d, j: (_scan_block(d, j, tpb), 0)),
            pl.BlockSpec((1, N_HEADS, LANES), lambda b, d, j: (d, 0, 0)),
        ],
        out_specs=pl.BlockSpec((1, TM, GROUP_W), lambda b, d, j: (d, rowblk(b, d, j), 0)),
        out_shape=jax.ShapeDtypeStruct((2, n, GROUP_W), F32),
        scratch_shapes=[pltpu.VMEM((N_HEADS, HEAD_DV, HEAD_DK), F32)],
        compiler_params=_cparams(("arbitrary", "arbitrary", "arbitrary")),
        name="retention",
    )(zr, cos_t, sin_t, lg)


def _head_norm(o, g, center):
    outs = []
    for h in range(N_HEADS):
        oh = o[:, h * HEAD_DV:(h + 1) * HEAD_DV]
        if center:
            oh = oh - jnp.mean(oh, axis=-1, keepdims=True)
        outs.append(oh * lax.rsqrt(jnp.mean(oh * oh, axis=-1, keepdims=True) + EPS))
    return jnp.concatenate(outs, axis=1) * g


def _route(logits):
    lane = lax.broadcasted_iota(jnp.int32, logits.shape, 1).astype(F32)
    neg = jnp.float32(-jnp.inf)
    big = jnp.float32(LANES)
    gmask = (lane >= N_EXPERTS) & (lane < N_EXPERTS + N_GROUPS)
    gl = jnp.where(gmask, logits, neg)
    gmax = jnp.max(gl, axis=-1, keepdims=True)
    gidx = jnp.min(jnp.where(gl == gmax, lane - N_EXPERTS, big), axis=-1, keepdims=True)
    g_w = 1.0 / jnp.sum(jnp.where(gmask, jnp.exp(logits - gmax), 0.0), axis=-1, keepdims=True)
    emask = (lane >= gidx * EXPERTS_PER_GROUP) & (lane < (gidx + 1.0) * EXPERTS_PER_GROUP)
    el = jnp.where(emask, logits, neg)
    e1 = jnp.max(el, axis=-1, keepdims=True)
    i1 = jnp.min(jnp.where(el == e1, lane, big), axis=-1, keepdims=True)
    el2 = jnp.where(lane == i1, neg, el)
    e2 = jnp.max(el2, axis=-1, keepdims=True)
    i2 = jnp.min(jnp.where(el2 == e2, lane, big), axis=-1, keepdims=True)
    r = jnp.exp(e2 - e1)
    w1 = g_w / (1.0 + r)
    w2 = g_w * r / (1.0 + r)
    idx = jnp.where(lane == 0.0, i1, jnp.where(lane == 1.0, i2, 0.0)).astype(jnp.int32)
    gate = jnp.where(lane == 0.0, w1, jnp.where(lane == 1.0, w2, 0.0))
    return idx, gate


def _outproj_kernel(mixc_ref, og_ref, or_ref, gr_ref, rg_ref, gng_ref, rng_ref, wout_ref, x_ref, mod_ref, g2_ref,
                    wr_ref, br_ref, xo_ref, h_ref, idx_ref, gate_ref):
    gla = _head_norm(og_ref[0] + og_ref[1], gng_ref[...], False) * _silu(gr_ref[...].astype(F32))
    ret = _head_norm(or_ref[0] + or_ref[1], rng_ref[...], True) * _silu(rg_ref[...].astype(F32))
    y = (_dot(mixc_ref[...], wout_ref[0, 0:2 * GROUP_W, :])
         + _dot(gla.astype(BF16), wout_ref[0, 2 * GROUP_W:3 * GROUP_W, :])
         + _dot(ret.astype(BF16), wout_ref[0, 3 * GROUP_W:4 * GROUP_W, :]))
    x = x_ref[...] + mod_ref[0, 2:3, :] * y
    xo_ref[...] = x
    hn = x * lax.rsqrt(jnp.mean(x * x, axis=-1, keepdims=True) + EPS) * g2_ref[...]
    h = hn * (1.0 + mod_ref[0, 4:5, :]) + mod_ref[0, 3:4, :]
    h_ref[...] = h
    hb = h.astype(BF16)
    w = wr_ref[...]
    w_hi = w.astype(BF16)
    w_lo = (w - w_hi.astype(F32)).astype(BF16)
    h_lo = (h - hb.astype(F32)).astype(BF16)
    logits = _dot(hb, w_hi) + _dot(hb, w_lo) + _dot(h_lo, w_hi) + br_ref[...]
    idx, gate = _route(logits)
    idx_ref[...] = idx
    gate_ref[...] = gate


def _outproj_call(mixc, og, orr, zg, zr, gng, rng, wout, l, x, mods, g2, wr, br, tpb):
    n, d = x.shape
    nt = n // TM
    return pl.pallas_call(
        _outproj_kernel,
        grid=(nt,),
        in_specs=[
            pl.BlockSpec((TM, 2 * GROUP_W), lambda i: (i, 0)),
            pl.BlockSpec((2, TM, GROUP_W), lambda i: (0, i, 0)),
            pl.BlockSpec((2, TM, GROUP_W), lambda i: (0, i, 0)),
            pl.BlockSpec((TM, GROUP_W), lambda i: (i, 2)),
            pl.BlockSpec((TM, GROUP_W), lambda i: (i, 2)),
            pl.BlockSpec((1, GROUP_W), lambda i: (0, 0)),
            pl.BlockSpec((1, GROUP_W), lambda i: (0, 0)),
            pl.BlockSpec((1, d, d), lambda i: (l, 0, 0), pipeline_mode=pl.Buffered(1)),
            pl.BlockSpec((TM, d), lambda i: (i, 0)),
            pl.BlockSpec((1, N_MOD, d), lambda i: (_mod_row(i, tpb), 0, 0)),
            pl.BlockSpec((1, d), lambda i: (0, 0)),
            pl.BlockSpec((d, LANES), lambda i: (0, 0)),
            pl.BlockSpec((1, LANES), lambda i: (0, 0)),
        ],
        out_specs=[
            pl.BlockSpec((TM, d), lambda i: (i, 0)),
            pl.BlockSpec((TM, d), lambda i: (i, 0)),
            pl.BlockSpec((TM, LANES), lambda i: (i, 0)),
            pl.BlockSpec((TM, LANES), lambda i: (i, 0)),
        ],
        out_shape=[
            jax.ShapeDtypeStruct((n, d), F32),
            jax.ShapeDtypeStruct((n, d), F32),
            jax.ShapeDtypeStruct((n, LANES), jnp.int32),
            jax.ShapeDtypeStruct((n, LANES), F32),
        ],
        compiler_params=_cparams(("arbitrary",)),
        name="outproj",
    )(mixc, og, orr, zg, zr, gng, rng, wout, x, mods, g2, wr, br)


def _sc_gather(table, idx):
    n_idx = idx.shape[0]
    width = table.shape[1]
    assert n_idx % (SC_WINDOW * SC_WORKERS) == 0
    per = n_idx // SC_WORKERS
    per_pad = (per + LANES - 1) // LANES * LANES
    idx_w = jnp.pad(idx.reshape(SC_WORKERS, per), ((0, 0), (0, per_pad - per)))
    mesh = plsc.VectorSubcoreMesh(core_axis_name="core", subcore_axis_name="subcore")
    n_cores = SC_WORKERS // 16

    @functools.partial(
        pl.kernel,
        out_type=jax.ShapeDtypeStruct((n_idx, width), table.dtype),
        mesh=mesh,
        scratch_types=[
            pltpu.VMEM((per_pad,), jnp.int32),
            pltpu.VMEM((SC_WINDOW, width), table.dtype),
            pltpu.SemaphoreType.DMA,
        ],
    )
    def gather_kernel(table_hbm, idx_hbm, out_hbm, idx_v, rows_v, sem):
        wid = lax.axis_index("subcore") * n_cores + lax.axis_index("core")
        base = wid * per
        pltpu.sync_copy(idx_hbm.at[wid], idx_v)

        @pl.loop(0, per // SC_WINDOW)
        def _(j):
            pltpu.async_copy(table_hbm.at[idx_v.at[pl.ds(j * SC_WINDOW, SC_WINDOW)]], rows_v, sem).wait()
            pltpu.sync_copy(rows_v, out_hbm.at[pl.ds(base + j * SC_WINDOW, SC_WINDOW)])

    return gather_kernel(table, idx_w)


def _expert_up_kernel(blk_e_ref, nvalid_ref, x_ref, w1_ref, w3_ref, h_ref):
    @pl.when(nvalid_ref[pl.program_id(0)] > 0)
    def _compute():
        x = x_ref[...].astype(BF16)
        h1 = _dot(x, w1_ref[0].astype(BF16))
        h3 = _dot(x, w3_ref[0].astype(BF16))
        h_ref[...] = (_silu(h1) * h3).astype(BF16)


def _expert_down_kernel(blk_e_ref, nvalid_ref, h_ref, w2_ref, y_ref):
    @pl.when(nvalid_ref[pl.program_id(0)] > 0)
    def _compute():
        y_ref[...] = _dot(h_ref[...], w2_ref[0].astype(BF16))


def _experts_call(blk_e, nvalid, xs, w1, w3, w2):
    n_slots, d = xs.shape
    nb = n_slots // MOE_TB
    hidden = w1.shape[-1]
    up = pl.pallas_call(
        _expert_up_kernel,
        grid_spec=pltpu.PrefetchScalarGridSpec(
            num_scalar_prefetch=2,
            grid=(nb,),
            in_specs=[
                pl.BlockSpec((MOE_TB, d), lambda i, be, nv: (i, 0)),
                pl.BlockSpec((1, d, hidden), lambda i, be, nv: (be[i], 0, 0)),
                pl.BlockSpec((1, d, hidden), lambda i, be, nv: (be[i], 0, 0)),
            ],
            out_specs=pl.BlockSpec((MOE_TB, hidden), lambda i, be, nv: (i, 0)),
        ),
        out_shape=jax.ShapeDtypeStruct((n_slots, hidden), BF16),
        compiler_params=_cparams(("arbitrary",)),
        name="expert_up",
    )(blk_e, nvalid, xs, w1, w3)
    return pl.pallas_call(
        _expert_down_kernel,
        grid_spec=pltpu.PrefetchScalarGridSpec(
            num_scalar_prefetch=2,
            grid=(nb,),
            in_specs=[
                pl.BlockSpec((MOE_TB, hidden), lambda i, be, nv: (i, 0)),
                pl.BlockSpec((1, hidden, d), lambda i, be, nv: (be[i], 0, 0)),
            ],
            out_specs=pl.BlockSpec((MOE_TB, d), lambda i, be, nv: (i, 0)),
        ),
        out_shape=jax.ShapeDtypeStruct((n_slots, d), F32),
        compiler_params=_cparams(("arbitrary",)),
        name="expert_down",
    )(blk_e, nvalid, up, w2)


def _slot_plan(idx):
    n = idx.shape[0]
    n_asg = n * TOP_K
    flat_e = idx[:, :TOP_K].reshape(n_asg)
    order = jnp.argsort(flat_e).astype(jnp.int32)
    experts = jnp.arange(N_EXPERTS, dtype=jnp.int32)
    counts = jnp.sum(flat_e[:, None] == experts[None, :], axis=0, dtype=jnp.int32)
    padded = (counts + MOE_TB - 1) // MOE_TB * MOE_TB
    pad_end = jnp.cumsum(padded)
    pad_start = pad_end - padded
    start = jnp.cumsum(counts) - counts
    n_slots = (n_asg + MOE_TB - 1) // MOE_TB * MOE_TB + N_EXPERTS * MOE_TB
    nb = n_slots // MOE_TB
    blk0 = jnp.arange(nb, dtype=jnp.int32) * MOE_TB
    blk_e = jnp.minimum(jnp.sum(blk0[:, None] >= pad_end[None, :], axis=1, dtype=jnp.int32), N_EXPERTS - 1)
    sel = (blk_e[:, None] == experts[None, :]).astype(jnp.int32)
    blk_cnt = jnp.sum(sel * counts[None, :], axis=1)
    blk_pad0 = jnp.sum(sel * pad_start[None, :], axis=1)
    blk_start = jnp.sum(sel * start[None, :], axis=1)
    nvalid = jnp.clip(blk_cnt - (blk0 - blk_pad0), 0, MOE_TB).astype(jnp.int32)
    blk_w = jnp.where(nvalid > 0, blk_e, jnp.max(jnp.where(nvalid > 0, blk_e, 0)))
    within = jnp.arange(MOE_TB, dtype=jnp.int32)[None, :]
    valid = within < nvalid[:, None]
    pos = jnp.clip((blk_start + blk0 - blk_pad0)[:, None] + within, 0, n_asg - 1)
    asg = order[pos.reshape(n_slots)]
    slot_tok = jnp.where(valid.reshape(n_slots), lax.shift_right_logical(asg, 1), 0).astype(jnp.int32)
    rank = jnp.argsort(order).astype(jnp.int32)
    sel_a = (flat_e[:, None] == experts[None, :]).astype(jnp.int32)
    dest = rank + jnp.sum(sel_a * (pad_start - start)[None, :], axis=1)
    dest_ct = dest.reshape(n, TOP_K).T.reshape(n_asg)
    return blk_w, nvalid, slot_tok, dest_ct


def _combined(x_ref, y_ref, gate_ref, mod_ref):
    f = gate_ref[:, 0:1] * y_ref[0] + gate_ref[:, 1:2] * y_ref[1]
    return x_ref[...] + mod_ref[0, 5:6, :] * f


def _combine_kernel(x_ref, y_ref, gate_ref, mod_ref, o_ref):
    o_ref[...] = _combined(x_ref, y_ref, gate_ref, mod_ref)


def _combine_call(x, y, gate, mods, tpb):
    n, d = x.shape
    return pl.pallas_call(
        _combine_kernel,
        grid=(n // TM,),
        in_specs=[
            pl.BlockSpec((TM, d), lambda i: (i, 0)),
            pl.BlockSpec((TOP_K, TM, d), lambda i: (0, i, 0)),
            pl.BlockSpec((TM, LANES), lambda i: (i, 0)),
            pl.BlockSpec((1, N_MOD, d), lambda i: (_mod_row(i, tpb), 0, 0)),
        ],
        out_specs=pl.BlockSpec((TM, d), lambda i: (i, 0)),
        out_shape=jax.ShapeDtypeStruct((n, d), F32),
        compiler_params=_cparams(("arbitrary",)),
        name="combine",
    )(x, y, gate, mods)


def _final_kernel(x_ref, y_ref, gate_ref, mod_ref, g_ref, o_ref):
    x = _combined(x_ref, y_ref, gate_ref, mod_ref)
    o_ref[0] = x * lax.rsqrt(jnp.mean(x * x, axis=-1, keepdims=True) + EPS) * g_ref[...]


def _final_call(x, y, gate, mods, gf, bsz, tpb):
    n, d = x.shape
    lat = tpb - 1
    return pl.pallas_call(
        _final_kernel,
        grid=(bsz, lat),
        in_specs=[
            pl.BlockSpec((TM, d), lambda b, j: (b * tpb + j + 1, 0)),
            pl.BlockSpec((TOP_K, TM, d), lambda b, j: (0, b * tpb + j + 1, 0)),
            pl.BlockSpec((TM, LANES), lambda b, j: (b * tpb + j + 1, 0)),
            pl.BlockSpec((1, N_MOD, d), lambda b, j: (b, 0, 0)),
            pl.BlockSpec((1, d), lambda b, j: (0, 0)),
        ],
        out_specs=pl.BlockSpec((1, TM, d), lambda b, j: (b, j, 0)),
        out_shape=jax.ShapeDtypeStruct((bsz, lat * TM, d), F32),
        compiler_params=_cparams(("arbitrary", "arbitrary")),
        name="final_norm",
    )(x, y, gate, mods, gf)


def _rope_tables(seq):
    n_freq = HEAD_DK // 4
    t = jnp.arange(seq)
    inv = ROPE_BASE ** (-jnp.arange(n_freq, dtype=F32) / n_freq)
    ang = jnp.concatenate([(t // GRID_W).astype(F32)[:, None] * inv, (t % GRID_W).astype(F32)[:, None] * inv], axis=-1)
    cos = jnp.concatenate([jnp.ones((TM, HEAD_DK // 2), F32), jnp.cos(ang)], axis=0)
    sin = jnp.concatenate([jnp.zeros((TM, HEAD_DK // 2), F32), jnp.sin(ang)], axis=0)
    cos_t = jnp.tile(jnp.concatenate([cos, cos], axis=-1), (1, N_HEADS))
    sin_t = jnp.tile(jnp.concatenate([-sin, sin], axis=-1), (1, N_HEADS))
    return cos_t, sin_t


def kernel(x, c, ctx, c_ctx, norm1_g, norm2_g, ada_w, ada_b, w_in, cf_dw, cf_b, cf_ln_g, cf_ln_b, sc_dw, gla_w2,
           gla_b2, gla_ng, ret_logit, ret_ng, w_out, w_grp, b_grp, w_rt, b_rt, e_w1, e_w3, e_w2, final_g):
    bsz, seq, d = x.shape
    depth = w_in.shape[0]
    assert d == D_MODEL and ctx.shape[1] == TM and seq % TM == 0 and bsz == 2 and TOP_K == 2
    assert w_in.shape[-1] == IN_W
    tpb = 1 + seq // TM
    n = bsz * tpb * TM
    nk = N_HEADS * HEAD_DK

    s8 = jnp.concatenate([c, c_ctx[None, :], jnp.zeros((8 - bsz - 1, d), F32)], axis=0)
    mods_all = _ada_call(s8, ada_w, ada_b)[:, :bsz + 1, :].reshape(depth, bsz + 1, N_MOD, d)

    w_in_p = _pack_w_in(w_in)
    w_out_b = _cast_w_out(w_out)
    cos_t, sin_t = _rope_tables(seq)
    w2pad = jnp.zeros((depth, 2, LANES, nk), F32)
    w2pad = w2pad.at[:, 0, 0:GLA_RANK, :].set(gla_w2[:, 0]).at[:, 1, GLA_RANK:2 * GLA_RANK, :].set(gla_w2[:, 1])
    wr_all = jnp.concatenate([w_rt, w_grp, jnp.zeros((depth, d, LANES - N_EXPERTS - N_GROUPS), F32)], axis=-1)
    br_all = jnp.concatenate([b_rt, b_grp, jnp.zeros((depth, LANES - N_EXPERTS - N_GROUPS), F32)], axis=-1)

    xa = jnp.concatenate([ctx, x], axis=1).reshape(n, d)
    out = None
    for l in range(depth):
        mods = mods_all[l]
        zc, zg, zr = _inproj_call(xa, mods, norm1_g[l][None, :], w_in_p, l, tpb)
        mixc = _conv_call(zc, cf_dw[l], cf_b[l][None, :], cf_ln_g[l][None, :], cf_ln_b[l][None, :], sc_dw[l], tpb)
        og = _gla_call(zg, w2pad[l], gla_b2[l][:, None, :], bsz, tpb)
        lg = jnp.broadcast_to(ret_logit[l][:, :, None], (2, N_HEADS, LANES))
        orr = _ret_call(zr, cos_t, sin_t, lg, bsz, tpb)
        xa, h2, idx, gate = _outproj_call(mixc, og, orr, zg, zr, gla_ng[l][None, :], ret_ng[l][None, :], w_out_b, l,
                                          xa, mods, norm2_g[l][None, :], wr_all[l], br_all[l][None, :], tpb)
        blk_e, nvalid, slot_tok, dest_ct = _slot_plan(idx)
        xs = _sc_gather(h2, slot_tok)
        ys = _experts_call(blk_e, nvalid, xs, e_w1[l], e_w3[l], e_w2[l])
        y = _sc_gather(ys, dest_ct).reshape(TOP_K, n, d)
        if l == depth - 1:
            out = _final_call(xa, y, gate, mods, final_g[None, :], bsz, tpb)
        else:
            xa = _combine_call(xa, y, gate, mods, tpb)
    return out
```

```python
import functools

import jax
import jax.numpy as jnp
from jax import lax
from jax.experimental import pallas as pl
from jax.experimental.pallas import tpu as pltpu
from jax.experimental.pallas import tpu_sc as plsc

F32 = jnp.float32
BF16 = jnp.bfloat16

D_MODEL = 2048
GRID_W = 64
GROUP_W = D_MODEL // 4
CF_KERNEL = 31
SC_KERNEL = 3
N_HEADS = 4
HEAD_DK = 64
HEAD_DV = 128
GLA_RANK = 16
GLA_TAU = 16.0
GLA_CHUNK = 128
ROPE_BASE = 10000.0
N_GROUPS = 4
EXPERTS_PER_GROUP = 4
N_EXPERTS = N_GROUPS * EXPERTS_PER_GROUP
TOP_K = 2
EXPERT_HIDDEN = D_MODEL // 2
N_MOD = 6
EPS = 1e-6

TM = 256
LANES = 128
ADA_TN = 1024
MOE_TB = 512
VMEM_LIMIT = 56 * 1024 * 1024
SC_WORKERS = 32
SC_WINDOW = 16

ZC_W = 5 * GROUP_W
ZG_W = 3 * GROUP_W + LANES
ZR_W = 3 * GROUP_W
Z_W = ZC_W + ZG_W + ZR_W
IN_W = Z_W - (LANES - 2 * GLA_RANK)
GLR_END = ZC_W + 3 * GROUP_W + 2 * GLA_RANK
PACK_MOVES = (
    (3 * GROUP_W, 0, 2 * GROUP_W),
    (2 * GROUP_W, 2 * GROUP_W, GROUP_W),
    (0, 3 * GROUP_W, 2 * GROUP_W),
    (ZC_W, ZC_W, GLR_END - ZC_W),
    (GLR_END, ZC_W + ZG_W, ZR_W),
)


def _cparams(sem):
    return pltpu.CompilerParams(dimension_semantics=sem, vmem_limit_bytes=VMEM_LIMIT)


def _sigmoid(x):
    return 1.0 / (1.0 + jnp.exp(-x))


def _silu(x):
    return x * _sigmoid(x)


def _log_sigmoid(x):
    return jnp.minimum(x, 0.0) - jnp.log1p(jnp.exp(-jnp.abs(x)))


def _dot(a, b):
    return jnp.dot(a, b, preferred_element_type=F32)


def _dot_nt(a, b):
    return lax.dot_general(a, b, (((1,), (1,)), ((), ())), preferred_element_type=F32)


def _dot_tn(a, b):
    return lax.dot_general(a, b, (((0,), (0,)), ((), ())), preferred_element_type=F32)


def _mod_row(i, tpb):
    return jnp.where(i % tpb == 0, 2, i // tpb)


def _pack_kernel(w_ref, o_ref):
    for src, dst, width in PACK_MOVES:
        o_ref[0, :, dst:dst + width] = w_ref[0, :, src:src + width].astype(BF16)
    o_ref[0, :, GLR_END:ZC_W + ZG_W] = jnp.zeros((o_ref.shape[1], ZC_W + ZG_W - GLR_END), BF16)


def _pack_w_in(w_in):
    depth, d, _ = w_in.shape
    return pl.pallas_call(
        _pack_kernel,
        grid=(depth, d // TM),
        in_specs=[pl.BlockSpec((1, TM, IN_W), lambda l, i: (l, i, 0))],
        out_specs=pl.BlockSpec((1, TM, Z_W), lambda l, i: (l, i, 0)),
        out_shape=jax.ShapeDtypeStruct((depth, d, Z_W), BF16),
        compiler_params=_cparams(("arbitrary", "arbitrary")),
        name="pack_w_in",
    )(w_in)


def _cast_kernel(w_ref, o_ref):
    o_ref[...] = w_ref[...].astype(BF16)


def _cast_w_out(w_out):
    depth, k, d = w_out.shape
    return pl.pallas_call(
        _cast_kernel,
        grid=(depth, k // TM),
        in_specs=[pl.BlockSpec((1, TM, d), lambda l, i: (l, i, 0))],
        out_specs=pl.BlockSpec((1, TM, d), lambda l, i: (l, i, 0)),
        out_shape=jax.ShapeDtypeStruct((depth, k, d), BF16),
        compiler_params=_cparams(("arbitrary", "arbitrary")),
        name="cast_w_out",
    )(w_out)


def _ada_kernel(s_ref, w_ref, b_ref, o_ref):
    a = _silu(s_ref[...]).astype(BF16)
    o_ref[0] = _dot(a, w_ref[0].astype(BF16)) + b_ref[0]


def _ada_call(s8, ada_w, ada_b):
    depth, d, nm = ada_w.shape
    return pl.pallas_call(
        _ada_kernel,
        grid=(depth, nm // ADA_TN),
        in_specs=[
            pl.BlockSpec((8, d), lambda l, j: (0, 0)),
            pl.BlockSpec((1, d, ADA_TN), lambda l, j: (l, 0, j)),
            pl.BlockSpec((1, 1, ADA_TN), lambda l, j: (l, 0, j)),
        ],
        out_specs=pl.BlockSpec((1, 8, ADA_TN), lambda l, j: (l, 0, j)),
        out_shape=jax.ShapeDtypeStruct((depth, 8, nm), F32),
        compiler_params=_cparams(("arbitrary", "arbitrary")),
        name="adaln",
    )(s8, ada_w, ada_b.reshape(depth, 1, nm))


def _inproj_kernel(x_ref, mod_ref, g_ref, w_ref, zc_ref, zg_ref, zr_ref):
    x = x_ref[...]
    y = x * lax.rsqrt(jnp.mean(x * x, axis=-1, keepdims=True) + EPS) * g_ref[...]
    h = (y * (1.0 + mod_ref[0, 1:2, :]) + mod_ref[0, 0:1, :]).astype(BF16)
    off = 0
    for ref, width in ((zc_ref, ZC_W), (zg_ref, ZG_W), (zr_ref, ZR_W)):
        c = 0
        while c < width:
            step = min(512, width - c)
            ref[:, c:c + step] = _dot(h, w_ref[0, :, off + c:off + c + step]).astype(BF16)
            c += step
        off += width


def _inproj_call(x, mods, g1, w_packed, l, tpb):
    n, d = x.shape
    nt = n // TM
    return pl.pallas_call(
        _inproj_kernel,
        grid=(nt,),
        in_specs=[
            pl.BlockSpec((TM, d), lambda i: (i, 0)),
            pl.BlockSpec((1, N_MOD, d), lambda i: (_mod_row(i, tpb), 0, 0)),
            pl.BlockSpec((1, d), lambda i: (0, 0)),
            pl.BlockSpec((1, d, Z_W), lambda i: (l, 0, 0), pipeline_mode=pl.Buffered(1)),
        ],
        out_specs=[
            pl.BlockSpec((TM, ZC_W), lambda i: (i, 0)),
            pl.BlockSpec((TM, ZG_W), lambda i: (i, 0)),
            pl.BlockSpec((TM, ZR_W), lambda i: (i, 0)),
        ],
        out_shape=[
            jax.ShapeDtypeStruct((n, ZC_W), BF16),
            jax.ShapeDtypeStruct((n, ZG_W), BF16),
            jax.ShapeDtypeStruct((n, ZR_W), BF16),
        ],
        compiler_params=_cparams(("arbitrary",)),
        name="inproj",
    )(x, mods, g1, w_packed)


PAD_LEAD = 16
SEG = GRID_W
LAT_STRIDE = SEG + PAD_LEAD
PAD_ROWS = (TM // SEG) * LAT_STRIDE + PAD_LEAD


def _conv_kernel(zc_ref, prev_ref, next_ref, cfw_ref, cfb_ref, lng_ref, lnb_ref, scw_ref, o_ref, pad_ref,
                 *, tpb):
    j = pl.program_id(0) % tpb
    nseg = TM // SEG
    half = CF_KERNEL // 2
    zeros_lead = jnp.zeros((PAD_LEAD, GROUP_W), F32)

    def glu():
        cfa = zc_ref[:, 3 * GROUP_W:4 * GROUP_W].astype(F32)
        cfg = zc_ref[:, 4 * GROUP_W:5 * GROUP_W].astype(F32)
        return cfa * _sigmoid(cfg)

    def finish_cf(acc, s):
        y = acc + cfb_ref[...]
        yc = y - jnp.mean(y, axis=-1, keepdims=True)
        yn = yc * lax.rsqrt(jnp.mean(yc * yc, axis=-1, keepdims=True) + EPS)
        o_ref[s * SEG:(s + 1) * SEG, 0:GROUP_W] = _silu(yn * lng_ref[...] + lnb_ref[...]).astype(BF16)

    def conformer(stride):
        for s in range(nseg):
            base = s * stride + PAD_LEAD - half
            acc = jnp.zeros((SEG, GROUP_W), F32)
            for k in range(CF_KERNEL):
                acc = acc + cfw_ref[k:k + 1, :] * pad_ref[base + k:base + k + SEG, :]
            finish_cf(acc, s)

    def sc_products():
        scc = zc_ref[:, 0:GROUP_W].astype(F32)
        scv = zc_ref[:, GROUP_W:2 * GROUP_W].astype(F32)
        scb = zc_ref[:, 2 * GROUP_W:3 * GROUP_W].astype(F32)
        return scc * scv, scb

    @pl.when(j != 0)
    def _latent():
        u = glu()
        for s in range(nseg):
            pad_ref[s * LAT_STRIDE:s * LAT_STRIDE + PAD_LEAD, :] = zeros_lead
            pad_ref[s * LAT_STRIDE + PAD_LEAD:(s + 1) * LAT_STRIDE, :] = u[s * SEG:(s + 1) * SEG]
        pad_ref[nseg * LAT_STRIDE:nseg * LAT_STRIDE + PAD_LEAD, :] = zeros_lead
        conformer(LAT_STRIDE)
        usc, scb = sc_products()
        up = prev_ref[:, 0:GROUP_W].astype(F32) * prev_ref[:, GROUP_W:2 * GROUP_W].astype(F32)
        un = next_ref[:, 0:GROUP_W].astype(F32) * next_ref[:, GROUP_W:2 * GROUP_W].astype(F32)
        up = jnp.where(j == 1, 0.0, up)
        un = jnp.where(j == tpb - 1, 0.0, un)
        above = jnp.concatenate([up, usc[:TM - GRID_W]], axis=0)
        below = jnp.concatenate([usc[GRID_W:], un], axis=0)
        y = scb * (scw_ref[0:1, :] * above + scw_ref[1:2, :] * usc + scw_ref[2:3, :] * below)
        o_ref[:, GROUP_W:2 * GROUP_W] = y.astype(BF16)

    @pl.when(j == 0)
    def _context():
        u = glu()
        pad_ref[0:PAD_LEAD, :] = zeros_lead
        pad_ref[PAD_LEAD:PAD_LEAD + TM, :] = u
        pad_ref[PAD_LEAD + TM:2 * PAD_LEAD + TM, :] = zeros_lead
        conformer(SEG)
        usc, scb = sc_products()
        pad_ref[PAD_LEAD:PAD_LEAD + TM, :] = usc
        before = pad_ref[PAD_LEAD - 1:PAD_LEAD - 1 + TM, :]
        after = pad_ref[PAD_LEAD + 1:PAD_LEAD + 1 + TM, :]
        y = scb * (scw_ref[0:1, :] * before + scw_ref[1:2, :] * usc + scw_ref[2:3, :] * after)
        o_ref[:, GROUP_W:2 * GROUP_W] = y.astype(BF16)


def _conv_call(zc, cfw, cfb, lng, lnb, scw, tpb):
    n = zc.shape[0]
    nt = n // TM
    r = TM // GRID_W
    nhalo = n // GRID_W
    return pl.pallas_call(
        functools.partial(_conv_kernel, tpb=tpb),
        grid=(nt,),
        in_specs=[
            pl.BlockSpec((TM, ZC_W), lambda i: (i, 0)),
            pl.BlockSpec((GRID_W, 2 * GROUP_W), lambda i: (jnp.maximum(i * r - 1, 0), 0)),
            pl.BlockSpec((GRID_W, 2 * GROUP_W), lambda i: (jnp.minimum(i * r + r, nhalo - 1), 0)),
            pl.BlockSpec((CF_KERNEL, GROUP_W), lambda i: (0, 0)),
            pl.BlockSpec((1, GROUP_W), lambda i: (0, 0)),
            pl.BlockSpec((1, GROUP_W), lambda i: (0, 0)),
            pl.BlockSpec((1, GROUP_W), lambda i: (0, 0)),
            pl.BlockSpec((SC_KERNEL, GROUP_W), lambda i: (0, 0)),
        ],
        out_specs=pl.BlockSpec((TM, 2 * GROUP_W), lambda i: (i, 0)),
        out_shape=jax.ShapeDtypeStruct((n, 2 * GROUP_W), BF16),
        scratch_shapes=[pltpu.VMEM((PAD_ROWS, GROUP_W), F32)],
        compiler_params=_cparams(("arbitrary",)),
        name="convmix",
    )(zc, zc, zc, cfw, cfb, lng, lnb, scw)


def _scan_block(d, j, tpb):
    return jnp.where(d == 0, j, jnp.where(j == 0, 0, tpb - j))


def _tri(rev, size):
    ri = lax.broadcasted_iota(jnp.int32, (size, size), 0)
    ci = lax.broadcasted_iota(jnp.int32, (size, size), 1)
    return (ri <= ci) if rev else (ri >= ci)


def _gla_kernel(zg_ref, w2_ref, b2_ref, o_ref, st_ref):
    d = pl.program_id(1)

    @pl.when(pl.program_id(2) == 0)
    def _init():
        st_ref[...] = jnp.zeros_like(st_ref)

    def run(rev):
        nk = N_HEADS * HEAD_DK
        zz = _dot(zg_ref[:, 3 * GROUP_W:3 * GROUP_W + LANES], w2_ref[0].astype(BF16)) + b2_ref[0]
        la = _log_sigmoid(zz) * (1.0 / GLA_TAU)
        tri = _tri(rev, GLA_CHUNK)
        trib = tri.astype(BF16)
        order = range(TM // GLA_CHUNK - 1, -1, -1) if rev else range(TM // GLA_CHUNK)
        states = [st_ref[h] for h in range(N_HEADS)]
        for c in order:
            rows = slice(c * GLA_CHUNK, (c + 1) * GLA_CHUNK)
            la_c = la[rows]
            hi = la_c.astype(BF16)
            lo = (la_c - hi.astype(F32)).astype(BF16)
            bc = _dot(trib, hi) + _dot(trib, lo)
            b_last = bc[0:1] if rev else bc[GLA_CHUNK - 1:GLA_CHUNK]
            b_mid = bc[GLA_CHUNK // 2:GLA_CHUNK // 2 + 1]
            q = zg_ref[rows, 0:nk].astype(F32) * (HEAD_DK ** -0.5)
            k = zg_ref[rows, nk:2 * nk].astype(F32)
            q_in = (q * jnp.exp(bc)).astype(BF16)
            q_mid = (q * jnp.exp(bc - b_mid)).astype(BF16)
            k_mid = (k * jnp.exp(b_mid - bc)).astype(BF16)
            k_st = (k * jnp.exp(b_last - bc)).astype(BF16)
            a_row = jnp.exp(b_last)
            for h in range(N_HEADS):
                hs = slice(h * HEAD_DK, (h + 1) * HEAD_DK)
                vs = slice(h * HEAD_DV, (h + 1) * HEAD_DV)
                vh = zg_ref[rows, GROUP_W + h * HEAD_DV:GROUP_W + (h + 1) * HEAD_DV]
                sc = jnp.where(tri, _dot_nt(q_mid[:, hs], k_mid[:, hs]), 0.0).astype(BF16)
                st = states[h]
                o_ref[0, rows, vs] = _dot(sc, vh) + _dot_nt(q_in[:, hs], st.astype(BF16))
                states[h] = st * a_row[:, hs] + _dot_tn(vh, k_st[:, hs])
        for h in range(N_HEADS):
            st_ref[h] = states[h]

    @pl.when(d == 0)
    def _fwd():
        run(False)

    @pl.when(d == 1)
    def _bwd():
        run(True)


def _gla_call(zg, w2pad, b2, bsz, tpb):
    n = zg.shape[0]
    nk = N_HEADS * HEAD_DK

    def rowblk(b, d, j):
        return b * tpb + _scan_block(d, j, tpb)

    return pl.pallas_call(
        _gla_kernel,
        grid=(bsz, 2, tpb),
        in_specs=[
            pl.BlockSpec((TM, ZG_W), lambda b, d, j: (rowblk(b, d, j), 0)),
            pl.BlockSpec((1, LANES, nk), lambda b, d, j: (d, 0, 0)),
            pl.BlockSpec((1, 1, nk), lambda b, d, j: (d, 0, 0)),
        ],
        out_specs=pl.BlockSpec((1, TM, GROUP_W), lambda b, d, j: (d, rowblk(b, d, j), 0)),
        out_shape=jax.ShapeDtypeStruct((2, n, GROUP_W), F32),
        scratch_shapes=[pltpu.VMEM((N_HEADS, HEAD_DV, HEAD_DK), F32)],
        compiler_params=_cparams(("arbitrary", "arbitrary", "arbitrary")),
        name="gla",
    )(zg, w2pad, b2)


def _ret_kernel(zr_ref, cos_ref, sin_ref, lg_ref, o_ref, st_ref):
    d = pl.program_id(1)

    @pl.when(pl.program_id(2) == 0)
    def _init():
        st_ref[...] = jnp.zeros_like(st_ref)

    def run(rev):
        nk = N_HEADS * HEAD_DK
        lane = lax.broadcasted_iota(jnp.int32, (TM, nk), 1)
        first_half = (lane % HEAD_DK) < (HEAD_DK // 2)
        cosf = cos_ref[...]
        sinf = sin_ref[...]

        def rope(x):
            swapped = jnp.where(first_half, pltpu.roll(x, nk - HEAD_DK // 2, 1), pltpu.roll(x, HEAD_DK // 2, 1))
            return x * cosf + swapped * sinf

        q = rope(zr_ref[:, 0:nk].astype(F32)).astype(BF16)
        k = rope(zr_ref[:, nk:2 * nk].astype(F32)) * (HEAD_DK ** -0.5)
        lg = _log_sigmoid(lg_ref[0])
        tri = _tri(rev, TM)
        ri = lax.broadcasted_iota(jnp.int32, (TM, TM), 0)
        ci = lax.broadcasted_iota(jnp.int32, (TM, TM), 1)
        rel = ((ci - ri) if rev else (ri - ci)).astype(F32)
        pos = lax.broadcasted_iota(jnp.int32, (TM, LANES), 0).astype(F32)
        q_steps = (TM - pos) if rev else (pos + 1.0)
        k_steps = pos if rev else (TM - 1.0 - pos)
        lg_wide = jnp.concatenate([lg, lg], axis=1)
        for h in range(N_HEADS):
            hs = slice(h * HEAD_DK, (h + 1) * HEAD_DK)
            vs = slice(h * HEAD_DV, (h + 1) * HEAD_DV)
            lgh = lg[h:h + 1, :]
            decay = jnp.where(tri, jnp.exp(rel * lg_wide[h:h + 1, :]), 0.0)
            q_decay = jnp.exp(q_steps * lgh)
            k_decay = jnp.exp(k_steps[:, 0:HEAD_DK] * lgh[:, 0:HEAD_DK])
            chunk_decay = jnp.exp(float(TM) * lgh[:, 0:HEAD_DK])
            st = st_ref[h]
            qh = q[:, hs]
            kh = k[:, hs]
            vh = zr_ref[:, GROUP_W + h * HEAD_DV:GROUP_W + (h + 1) * HEAD_DV]
            sc = (_dot_nt(qh, kh.astype(BF16)) * decay).astype(BF16)
            o_ref[0, :, vs] = _dot(sc, vh) + _dot_nt(qh, st.astype(BF16)) * q_decay
            st_ref[h] = st * chunk_decay + _dot_tn(vh, (kh * k_decay).astype(BF16))

    @pl.when(d == 0)
    def _fwd():
        run(False)

    @pl.when(d == 1)
    def _bwd():
        run(True)


def _ret_call(zr, cos_t, sin_t, lg, bsz, tpb):
    n = zr.shape[0]
    nk = N_HEADS * HEAD_DK

    def rowblk(b, d, j):
        return b * tpb + _scan_block(d, j, tpb)

    return pl.pallas_call(
        _ret_kernel,
        grid=(bsz, 2, tpb),
        in_specs=[
            pl.BlockSpec((TM, ZR_W), lambda b, d, j: (rowblk(b, d, j), 0)),
            pl.BlockSpec((TM, nk), lambda b, d, j: (_scan_block(d, j, tpb), 0)),
            pl.BlockSpec((TM, nk), lambda b, d, j: (_scan_block(d, j, tpb), 0)),
            pl.BlockSpec((1, N_HEADS, LANES), lambda b, d, j: (d, 0, 0)),
        ],
        out_specs=pl.BlockSpec((1, TM, GROUP_W), lambda b, d, j: (d, rowblk(b, d, j), 0)),
        out_shape=jax.ShapeDtypeStruct((2, n, GROUP_W), F32),
        scratch_shapes=[pltpu.VMEM((N_HEADS, HEAD_DV, HEAD_DK), F32)],
        compiler_params=_cparams(("arbitrary", "arbitrary", "arbitrary")),
        name="retention",
    )(zr, cos_t, sin_t, lg)


def _head_norm(o, g, center):
    outs = []
    for h in range(N_HEADS):
        oh = o[:, h * HEAD_DV:(h + 1) * HEAD_DV]
        if center:
            oh = oh - jnp.mean(oh, axis=-1, keepdims=True)
        outs.append(oh * lax.rsqrt(jnp.mean(oh * oh, axis=-1, keepdims=True) + EPS))
    return jnp.concatenate(outs, axis=1) * g


def _route(logits):
    lane = lax.broadcasted_iota(jnp.int32, logits.shape, 1).astype(F32)
    neg = jnp.float32(-jnp.inf)
    big = jnp.float32(LANES)
    gmask = (lane >= N_EXPERTS) & (lane < N_EXPERTS + N_GROUPS)
    gl = jnp.where(gmask, logits, neg)
    gmax = jnp.max(gl, axis=-1, keepdims=True)
    gidx = jnp.min(jnp.where(gl == gmax, lane - N_EXPERTS, big), axis=-1, keepdims=True)
    g_w = 1.0 / jnp.sum(jnp.where(gmask, jnp.exp(logits - gmax), 0.0), axis=-1, keepdims=True)
    emask = (lane >= gidx * EXPERTS_PER_GROUP) & (lane < (gidx + 1.0) * EXPERTS_PER_GROUP)
    el = jnp.where(emask, logits, neg)
    e1 = jnp.max(el, axis=-1, keepdims=True)
    i1 = jnp.min(jnp.where(el == e1, lane, big), axis=-1, keepdims=True)
    el2 = jnp.where(lane == i1, neg, el)
    e2 = jnp.max(el2, axis=-1, keepdims=True)
    i2 = jnp.min(jnp.where(el2 == e2, lane, big), axis=-1, keepdims=True)
    r = jnp.exp(e2 - e1)
    w1 = g_w / (1.0 + r)
    w2 = g_w * r / (1.0 + r)
    idx = jnp.where(lane == 0.0, i1, jnp.where(lane == 1.0, i2, 0.0)).astype(jnp.int32)
    gate = jnp.where(lane == 0.0, w1, jnp.where(lane == 1.0, w2, 0.0))
    return idx, gate


def _outproj_kernel(mixc_ref, og_ref, or_ref, gr_ref, rg_ref, gng_ref, rng_ref, wout_ref, x_ref, mod_ref, g2_ref,
                    wr_ref, br_ref, xo_ref, h_ref, idx_ref, gate_ref):
    gla = _head_norm(og_ref[0] + og_ref[1], gng_ref[...], False) * _silu(gr_ref[...].astype(F32))
    ret = _head_norm(or_ref[0] + or_ref[1], rng_ref[...], True) * _silu(rg_ref[...].astype(F32))
    y = (_dot(mixc_ref[...], wout_ref[0, 0:2 * GROUP_W, :])
         + _dot(gla.astype(BF16), wout_ref[0, 2 * GROUP_W:3 * GROUP_W, :])
         + _dot(ret.astype(BF16), wout_ref[0, 3 * GROUP_W:4 * GROUP_W, :]))
    x = x_ref[...] + mod_ref[0, 2:3, :] * y
    xo_ref[...] = x
    hn = x * lax.rsqrt(jnp.mean(x * x, axis=-1, keepdims=True) + EPS) * g2_ref[...]
    h = hn * (1.0 + mod_ref[0, 4:5, :]) + mod_ref[0, 3:4, :]
    h_ref[...] = h
    hb = h.astype(BF16)
    w = wr_ref[...]
    w_hi = w.astype(BF16)
    w_lo = (w - w_hi.astype(F32)).astype(BF16)
    h_lo = (h - hb.astype(F32)).astype(BF16)
    logits = _dot(hb, w_hi) + _dot(hb, w_lo) + _dot(h_lo, w_hi) + br_ref[...]
    idx, gate = _route(logits)
    idx_ref[...] = idx
    gate_ref[...] = gate


def _outproj_call(mixc, og, orr, zg, zr, gng, rng, wout, l, x, mods, g2, wr, br, tpb):
    n, d = x.shape
    nt = n // TM
    return pl.pallas_call(
        _outproj_kernel,
        grid=(nt,),
        in_specs=[
            pl.BlockSpec((TM, 2 * GROUP_W), lambda i: (i, 0)),
            pl.BlockSpec((2, TM, GROUP_W), lambda i: (0, i, 0)),
            pl.BlockSpec((2, TM, GROUP_W), lambda i: (0, i, 0)),
            pl.BlockSpec((TM, GROUP_W), lambda i: (i, 2)),
            pl.BlockSpec((TM, GROUP_W), lambda i: (i, 2)),
            pl.BlockSpec((1, GROUP_W), lambda i: (0, 0)),
            pl.BlockSpec((1, GROUP_W), lambda i: (0, 0)),
            pl.BlockSpec((1, d, d), lambda i: (l, 0, 0), pipeline_mode=pl.Buffered(1)),
            pl.BlockSpec((TM, d), lambda i: (i, 0)),
            pl.BlockSpec((1, N_MOD, d), lambda i: (_mod_row(i, tpb), 0, 0)),
            pl.BlockSpec((1, d), lambda i: (0, 0)),
            pl.BlockSpec((d, LANES), lambda i: (0, 0)),
            pl.BlockSpec((1, LANES), lambda i: (0, 0)),
        ],
        out_specs=[
            pl.BlockSpec((TM, d), lambda i: (i, 0)),
            pl.BlockSpec((TM, d), lambda i: (i, 0)),
            pl.BlockSpec((TM, LANES), lambda i: (i, 0)),
            pl.BlockSpec((TM, LANES), lambda i: (i, 0)),
        ],
        out_shape=[
            jax.ShapeDtypeStruct((n, d), F32),
            jax.ShapeDtypeStruct((n, d), F32),
            jax.ShapeDtypeStruct((n, LANES), jnp.int32),
            jax.ShapeDtypeStruct((n, LANES), F32),
        ],
        compiler_params=_cparams(("arbitrary",)),
        name="outproj",
    )(mixc, og, orr, zg, zr, gng, rng, wout, x, mods, g2, wr, br)


def _sc_gather(table, idx):
    n_idx = idx.shape[0]
    width = table.shape[1]
    assert n_idx % (SC_WINDOW * SC_WORKERS) == 0
    per = n_idx // SC_WORKERS
    per_pad = (per + LANES - 1) // LANES * LANES
    idx_w = jnp.pad(idx.reshape(SC_WORKERS, per), ((0, 0), (0, per_pad - per)))
    mesh = plsc.VectorSubcoreMesh(core_axis_name="core", subcore_axis_name="subcore")
    n_cores = SC_WORKERS // 16

    @functools.partial(
        pl.kernel,
        out_type=jax.ShapeDtypeStruct((n_idx, width), table.dtype),
        mesh=mesh,
        scratch_types=[
            pltpu.VMEM((per_pad,), jnp.int32),
            pltpu.VMEM((SC_WINDOW, width), table.dtype),
            pltpu.SemaphoreType.DMA,
        ],
    )
    def gather_kernel(table_hbm, idx_hbm, out_hbm, idx_v, rows_v, sem):
        wid = lax.axis_index("subcore") * n_cores + lax.axis_index("core")
        base = wid * per
        pltpu.sync_copy(idx_hbm.at[wid], idx_v)

        @pl.loop(0, per // SC_WINDOW)
        def _(j):
            pltpu.async_copy(table_hbm.at[idx_v.at[pl.ds(j * SC_WINDOW, SC_WINDOW)]], rows_v, sem).wait()
            pltpu.sync_copy(rows_v, out_hbm.at[pl.ds(base + j * SC_WINDOW, SC_WINDOW)])

    return gather_kernel(table, idx_w)


def _expert_up_kernel(blk_e_ref, nvalid_ref, x_ref, w1_ref, w3_ref, h_ref):
    @pl.when(nvalid_ref[pl.program_id(0)] > 0)
    def _compute():
        x = x_ref[...].astype(BF16)
        h1 = _dot(x, w1_ref[0, 0].astype(BF16))
        h3 = _dot(x, w3_ref[0, 0].astype(BF16))
        h_ref[...] = (_silu(h1) * h3).astype(BF16)


def _expert_down_kernel(blk_e_ref, nvalid_ref, h_ref, w2_ref, y_ref):
    @pl.when(nvalid_ref[pl.program_id(0)] > 0)
    def _compute():
        y_ref[...] = _dot(h_ref[...], w2_ref[0, 0].astype(BF16))


def _experts_call(blk_e, nvalid, xs, w1, w3, w2, l):
    n_slots, d = xs.shape
    nb = n_slots // MOE_TB
    hidden = w1.shape[-1]
    up = pl.pallas_call(
        _expert_up_kernel,
        grid_spec=pltpu.PrefetchScalarGridSpec(
            num_scalar_prefetch=2,
            grid=(nb,),
            in_specs=[
                pl.BlockSpec((MOE_TB, d), lambda i, be, nv: (i, 0)),
                pl.BlockSpec((1, 1, d, hidden), lambda i, be, nv: (l, be[i], 0, 0)),
                pl.BlockSpec((1, 1, d, hidden), lambda i, be, nv: (l, be[i], 0, 0)),
            ],
            out_specs=pl.BlockSpec((MOE_TB, hidden), lambda i, be, nv: (i, 0)),
        ),
        out_shape=jax.ShapeDtypeStruct((n_slots, hidden), BF16),
        compiler_params=_cparams(("arbitrary",)),
        name="expert_up",
    )(blk_e, nvalid, xs, w1, w3)
    return pl.pallas_call(
        _expert_down_kernel,
        grid_spec=pltpu.PrefetchScalarGridSpec(
            num_scalar_prefetch=2,
            grid=(nb,),
            in_specs=[
                pl.BlockSpec((MOE_TB, hidden), lambda i, be, nv: (i, 0)),
                pl.BlockSpec((1, 1, hidden, d), lambda i, be, nv: (l, be[i], 0, 0)),
            ],
            out_specs=pl.BlockSpec((MOE_TB, d), lambda i, be, nv: (i, 0)),
        ),
        out_shape=jax.ShapeDtypeStruct((n_slots, d), F32),
        compiler_params=_cparams(("arbitrary",)),
        name="expert_down",
    )(blk_e, nvalid, up, w2)


def _slot_plan(idx):
    n = idx.shape[0]
    n_asg = n * TOP_K
    flat_e = idx[:, :TOP_K].reshape(n_asg)
    order = jnp.argsort(flat_e).astype(jnp.int32)
    experts = jnp.arange(N_EXPERTS, dtype=jnp.int32)
    counts = jnp.sum(flat_e[:, None] == experts[None, :], axis=0, dtype=jnp.int32)
    padded = (counts + MOE_TB - 1) // MOE_TB * MOE_TB
    pad_end = jnp.cumsum(padded)
    pad_start = pad_end - padded
    start = jnp.cumsum(counts) - counts
    n_slots = (n_asg + MOE_TB - 1) // MOE_TB * MOE_TB + N_EXPERTS * MOE_TB
    nb = n_slots // MOE_TB
    blk0 = jnp.arange(nb, dtype=jnp.int32) * MOE_TB
    blk_e = jnp.minimum(jnp.sum(blk0[:, None] >= pad_end[None, :], axis=1, dtype=jnp.int32), N_EXPERTS - 1)
    sel = (blk_e[:, None] == experts[None, :]).astype(jnp.int32)
    blk_cnt = jnp.sum(sel * counts[None, :], axis=1)
    blk_pad0 = jnp.sum(sel * pad_start[None, :], axis=1)
    blk_start = jnp.sum(sel * start[None, :], axis=1)
    nvalid = jnp.clip(blk_cnt - (blk0 - blk_pad0), 0, MOE_TB).astype(jnp.int32)
    blk_w = jnp.where(nvalid > 0, blk_e, jnp.max(jnp.where(nvalid > 0, blk_e, 0)))
    within = jnp.arange(MOE_TB, dtype=jnp.int32)[None, :]
    valid = within < nvalid[:, None]
    pos = jnp.clip((blk_start + blk0 - blk_pad0)[:, None] + within, 0, n_asg - 1)
    asg = order[pos.reshape(n_slots)]
    filler = jnp.arange(n_slots, dtype=jnp.int32) % n
    slot_tok = jnp.where(valid.reshape(n_slots), lax.shift_right_logical(asg, 1), filler).astype(jnp.int32)
    rank = jnp.argsort(order).astype(jnp.int32)
    sel_a = (flat_e[:, None] == experts[None, :]).astype(jnp.int32)
    dest = rank + jnp.sum(sel_a * (pad_start - start)[None, :], axis=1)
    dest_ct = dest.reshape(n, TOP_K).T.reshape(n_asg)
    return blk_w, nvalid, slot_tok, dest_ct


def _combined(x_ref, y_ref, gate_ref, mod_ref):
    f = gate_ref[:, 0:1] * y_ref[0] + gate_ref[:, 1:2] * y_ref[1]
    return x_ref[...] + mod_ref[0, 5:6, :] * f


def _combine_kernel(x_ref, y_ref, gate_ref, mod_ref, o_ref):
    o_ref[...] = _combined(x_ref, y_ref, gate_ref, mod_ref)


def _combine_call(x, y, gate, mods, tpb):
    n, d = x.shape
    return pl.pallas_call(
        _combine_kernel,
        grid=(n // TM,),
        in_specs=[
            pl.BlockSpec((TM, d), lambda i: (i, 0)),
            pl.BlockSpec((TOP_K, TM, d), lambda i: (0, i, 0)),
            pl.BlockSpec((TM, LANES), lambda i: (i, 0)),
            pl.BlockSpec((1, N_MOD, d), lambda i: (_mod_row(i, tpb), 0, 0)),
        ],
        out_specs=pl.BlockSpec((TM, d), lambda i: (i, 0)),
        out_shape=jax.ShapeDtypeStruct((n, d), F32),
        compiler_params=_cparams(("arbitrary",)),
        name="combine",
    )(x, y, gate, mods)


def _final_kernel(x_ref, y_ref, gate_ref, mod_ref, g_ref, o_ref):
    x = _combined(x_ref, y_ref, gate_ref, mod_ref)
    o_ref[0] = x * lax.rsqrt(jnp.mean(x * x, axis=-1, keepdims=True) + EPS) * g_ref[...]


def _final_call(x, y, gate, mods, gf, bsz, tpb):
    n, d = x.shape
    lat = tpb - 1
    return pl.pallas_call(
        _final_kernel,
        grid=(bsz, lat),
        in_specs=[
            pl.BlockSpec((TM, d), lambda b, j: (b * tpb + j + 1, 0)),
            pl.BlockSpec((TOP_K, TM, d), lambda b, j: (0, b * tpb + j + 1, 0)),
            pl.BlockSpec((TM, LANES), lambda b, j: (b * tpb + j + 1, 0)),
            pl.BlockSpec((1, N_MOD, d), lambda b, j: (b, 0, 0)),
            pl.BlockSpec((1, d), lambda b, j: (0, 0)),
        ],
        out_specs=pl.BlockSpec((1, TM, d), lambda b, j: (b, j, 0)),
        out_shape=jax.ShapeDtypeStruct((bsz, lat * TM, d), F32),
        compiler_params=_cparams(("arbitrary", "arbitrary")),
        name="final_norm",
    )(x, y, gate, mods, gf)


def _rope_tables(seq):
    n_freq = HEAD_DK // 4
    t = jnp.arange(seq)
    inv = ROPE_BASE ** (-jnp.arange(n_freq, dtype=F32) / n_freq)
    ang = jnp.concatenate([(t // GRID_W).astype(F32)[:, None] * inv, (t % GRID_W).astype(F32)[:, None] * inv], axis=-1)
    cos = jnp.concatenate([jnp.ones((TM, HEAD_DK // 2), F32), jnp.cos(ang)], axis=0)
    sin = jnp.concatenate([jnp.zeros((TM, HEAD_DK // 2), F32), jnp.sin(ang)], axis=0)
    cos_t = jnp.tile(jnp.concatenate([cos, cos], axis=-1), (1, N_HEADS))
    sin_t = jnp.tile(jnp.concatenate([-sin, sin], axis=-1), (1, N_HEADS))
    return cos_t, sin_t


def kernel(x, c, ctx, c_ctx, norm1_g, norm2_g, ada_w, ada_b, w_in, cf_dw, cf_b, cf_ln_g, cf_ln_b, sc_dw, gla_w2,
           gla_b2, gla_ng, ret_logit, ret_ng, w_out, w_grp, b_grp, w_rt, b_rt, e_w1, e_w3, e_w2, final_g):
    bsz, seq, d = x.shape
    depth = w_in.shape[0]
    assert d == D_MODEL and ctx.shape[1] == TM and seq % TM == 0 and bsz == 2 and TOP_K == 2
    assert w_in.shape[-1] == IN_W
    tpb = 1 + seq // TM
    n = bsz * tpb * TM
    nk = N_HEADS * HEAD_DK

    s8 = jnp.concatenate([c, c_ctx[None, :], jnp.zeros((8 - bsz - 1, d), F32)], axis=0)
    mods_all = _ada_call(s8, ada_w, ada_b)[:, :bsz + 1, :].reshape(depth, bsz + 1, N_MOD, d)

    w_in_p = _pack_w_in(w_in)
    w_out_b = _cast_w_out(w_out)
    cos_t, sin_t = _rope_tables(seq)
    w2pad = jnp.zeros((depth, 2, LANES, nk), F32)
    w2pad = w2pad.at[:, 0, 0:GLA_RANK, :].set(gla_w2[:, 0]).at[:, 1, GLA_RANK:2 * GLA_RANK, :].set(gla_w2[:, 1])
    wr_all = jnp.concatenate([w_rt, w_grp, jnp.zeros((depth, d, LANES - N_EXPERTS - N_GROUPS), F32)], axis=-1)
    br_all = jnp.concatenate([b_rt, b_grp, jnp.zeros((depth, LANES - N_EXPERTS - N_GROUPS), F32)], axis=-1)

    xa = jnp.concatenate([ctx, x], axis=1).reshape(n, d)
    out = None
    for l in range(depth):
        mods = mods_all[l]
        zc, zg, zr = _inproj_call(xa, mods, norm1_g[l][None, :], w_in_p, l, tpb)
        mixc = _conv_call(zc, cf_dw[l], cf_b[l][None, :], cf_ln_g[l][None, :], cf_ln_b[l][None, :], sc_dw[l], tpb)
        og = _gla_call(zg, w2pad[l], gla_b2[l][:, None, :], bsz, tpb)
        lg = jnp.broadcast_to(ret_logit[l][:, :, None], (2, N_HEADS, LANES))
        orr = _ret_call(zr, cos_t, sin_t, lg, bsz, tpb)
        xa, h2, idx, gate = _outproj_call(mixc, og, orr, zg, zr, gla_ng[l][None, :], ret_ng[l][None, :], w_out_b, l,
                                          xa, mods, norm2_g[l][None, :], wr_all[l], br_all[l][None, :], tpb)
        blk_e, nvalid, slot_tok, dest_ct = _slot_plan(idx)
        xs = _sc_gather(h2, slot_tok)
        ys = _experts_call(blk_e, nvalid, xs, e_w1, e_w3, e_w2, l)
        y = _sc_gather(ys, dest_ct).reshape(TOP_K, n, d)
        if l == depth - 1:
            out = _final_call(xa, y, gate, mods, final_g[None, :], bsz, tpb)
        else:
            xa = _combine_call(xa, y, gate, mods, tpb)
    return out
```

```python
import functools

import jax
import jax.numpy as jnp
from jax import lax
from jax.experimental import pallas as pl
from jax.experimental.pallas import tpu as pltpu
from jax.experimental.pallas import tpu_sc as plsc

F32 = jnp.float32
BF16 = jnp.bfloat16

D_MODEL = 2048
GRID_W = 64
GROUP_W = D_MODEL // 4
CF_KERNEL = 31
SC_KERNEL = 3
N_HEADS = 4
HEAD_DK = 64
HEAD_DV = 128
GLA_RANK = 16
GLA_TAU = 16.0
GLA_CHUNK = 128
ROPE_BASE = 10000.0
N_GROUPS = 4
EXPERTS_PER_GROUP = 4
N_EXPERTS = N_GROUPS * EXPERTS_PER_GROUP
TOP_K = 2
EXPERT_HIDDEN = D_MODEL // 2
N_MOD = 6
EPS = 1e-6

TM = 256
LANES = 128
ADA_TN = 1024
MOE_TB = 512
VMEM_LIMIT = 56 * 1024 * 1024
SC_WORKERS = 32
SC_WINDOW = 16

ZC_W = 5 * GROUP_W
ZG_W = 3 * GROUP_W + LANES
ZR_W = 3 * GROUP_W
Z_W = ZC_W + ZG_W + ZR_W
IN_W = Z_W - (LANES - 2 * GLA_RANK)
GLR_END = ZC_W + 3 * GROUP_W + 2 * GLA_RANK
PACK_MOVES = (
    (3 * GROUP_W, 0, 2 * GROUP_W),
    (2 * GROUP_W, 2 * GROUP_W, GROUP_W),
    (0, 3 * GROUP_W, 2 * GROUP_W),
    (ZC_W, ZC_W, GLR_END - ZC_W),
    (GLR_END, ZC_W + ZG_W, ZR_W),
)


def _cparams(sem):
    return pltpu.CompilerParams(dimension_semantics=sem, vmem_limit_bytes=VMEM_LIMIT)


def _sigmoid(x):
    return 1.0 / (1.0 + jnp.exp(-x))


def _silu(x):
    return x * _sigmoid(x)


def _log_sigmoid(x):
    return jnp.minimum(x, 0.0) - jnp.log1p(jnp.exp(-jnp.abs(x)))


def _dot(a, b):
    return jnp.dot(a, b, preferred_element_type=F32)


def _dot_nt(a, b):
    return lax.dot_general(a, b, (((1,), (1,)), ((), ())), preferred_element_type=F32)


def _dot_tn(a, b):
    return lax.dot_general(a, b, (((0,), (0,)), ((), ())), preferred_element_type=F32)


def _mod_row(i, tpb):
    return jnp.where(i % tpb == 0, 2, i // tpb)


def _pack_kernel(w_ref, o_ref):
    for src, dst, width in PACK_MOVES:
        o_ref[0, :, dst:dst + width] = w_ref[0, :, src:src + width].astype(BF16)
    o_ref[0, :, GLR_END:ZC_W + ZG_W] = jnp.zeros((o_ref.shape[1], ZC_W + ZG_W - GLR_END), BF16)


def _pack_w_in(w_in):
    depth, d, _ = w_in.shape
    return pl.pallas_call(
        _pack_kernel,
        grid=(depth, d // TM),
        in_specs=[pl.BlockSpec((1, TM, IN_W), lambda l, i: (l, i, 0))],
        out_specs=pl.BlockSpec((1, TM, Z_W), lambda l, i: (l, i, 0)),
        out_shape=jax.ShapeDtypeStruct((depth, d, Z_W), BF16),
        compiler_params=_cparams(("arbitrary", "arbitrary")),
        name="pack_w_in",
    )(w_in)


def _cast_kernel(w_ref, o_ref):
    o_ref[...] = w_ref[...].astype(BF16)


def _cast_w_out(w_out):
    depth, k, d = w_out.shape
    return pl.pallas_call(
        _cast_kernel,
        grid=(depth, k // TM),
        in_specs=[pl.BlockSpec((1, TM, d), lambda l, i: (l, i, 0))],
        out_specs=pl.BlockSpec((1, TM, d), lambda l, i: (l, i, 0)),
        out_shape=jax.ShapeDtypeStruct((depth, k, d), BF16),
        compiler_params=_cparams(("arbitrary", "arbitrary")),
        name="cast_w_out",
    )(w_out)


def _ada_kernel(s_ref, w_ref, b_ref, o_ref):
    a = _silu(s_ref[...]).astype(BF16)
    o_ref[0] = _dot(a, w_ref[0].astype(BF16)) + b_ref[0]


def _ada_call(s8, ada_w, ada_b):
    depth, d, nm = ada_w.shape
    return pl.pallas_call(
        _ada_kernel,
        grid=(depth, nm // ADA_TN),
        in_specs=[
            pl.BlockSpec((8, d), lambda l, j: (0, 0)),
            pl.BlockSpec((1, d, ADA_TN), lambda l, j: (l, 0, j)),
            pl.BlockSpec((1, 1, ADA_TN), lambda l, j: (l, 0, j)),
        ],
        out_specs=pl.BlockSpec((1, 8, ADA_TN), lambda l, j: (l, 0, j)),
        out_shape=jax.ShapeDtypeStruct((depth, 8, nm), F32),
        compiler_params=_cparams(("arbitrary", "arbitrary")),
        name="adaln",
    )(s8, ada_w, ada_b.reshape(depth, 1, nm))


def _combined(x_ref, y_ref, gate_ref, mod_ref):
    f = gate_ref[:, 0:1] * y_ref[0] + gate_ref[:, 1:2] * y_ref[1]
    return x_ref[...] + mod_ref[0, 5:6, :] * f


def _inproj_kernel(*refs, fuse_prev):
    if fuse_prev:
        x_ref, y_ref, gate_ref, pmod_ref, mod_ref, g_ref, w_ref, zc_ref, zg_ref, zr_ref, xo_ref = refs
        x = _combined(x_ref, y_ref, gate_ref, pmod_ref)
        xo_ref[...] = x
    else:
        x_ref, mod_ref, g_ref, w_ref, zc_ref, zg_ref, zr_ref = refs
        x = x_ref[...]
    y = x * lax.rsqrt(jnp.mean(x * x, axis=-1, keepdims=True) + EPS) * g_ref[...]
    h = (y * (1.0 + mod_ref[0, 1:2, :]) + mod_ref[0, 0:1, :]).astype(BF16)
    off = 0
    for ref, width in ((zc_ref, ZC_W), (zg_ref, ZG_W), (zr_ref, ZR_W)):
        c = 0
        while c < width:
            step = min(512, width - c)
            ref[:, c:c + step] = _dot(h, w_ref[0, :, off + c:off + c + step]).astype(BF16)
            c += step
        off += width


def _inproj_call(x, prev, mods, g1, w_packed, l, tpb):
    n, d = x.shape
    nt = n // TM
    mod_spec = pl.BlockSpec((1, N_MOD, d), lambda i: (_mod_row(i, tpb), 0, 0))
    in_specs = [pl.BlockSpec((TM, d), lambda i: (i, 0))]
    out_specs = [
        pl.BlockSpec((TM, ZC_W), lambda i: (i, 0)),
        pl.BlockSpec((TM, ZG_W), lambda i: (i, 0)),
        pl.BlockSpec((TM, ZR_W), lambda i: (i, 0)),
    ]
    out_shape = [
        jax.ShapeDtypeStruct((n, ZC_W), BF16),
        jax.ShapeDtypeStruct((n, ZG_W), BF16),
        jax.ShapeDtypeStruct((n, ZR_W), BF16),
    ]
    args = [x]
    if prev is not None:
        in_specs += [
            pl.BlockSpec((TOP_K, TM, d), lambda i: (0, i, 0)),
            pl.BlockSpec((TM, LANES), lambda i: (i, 0)),
            mod_spec,
        ]
        out_specs.append(pl.BlockSpec((TM, d), lambda i: (i, 0)))
        out_shape.append(jax.ShapeDtypeStruct((n, d), F32))
        args += list(prev)
    in_specs += [
        mod_spec,
        pl.BlockSpec((1, d), lambda i: (0, 0)),
        pl.BlockSpec((1, d, Z_W), lambda i: (l, 0, 0), pipeline_mode=pl.Buffered(1)),
    ]
    return pl.pallas_call(
        functools.partial(_inproj_kernel, fuse_prev=prev is not None),
        grid=(nt,),
        in_specs=in_specs,
        out_specs=out_specs,
        out_shape=out_shape,
        compiler_params=_cparams(("arbitrary",)),
        name="inproj",
    )(*args, mods, g1, w_packed)


PAD_LEAD = 16
SEG = GRID_W
LAT_STRIDE = SEG + PAD_LEAD
PAD_ROWS = (TM // SEG) * LAT_STRIDE + PAD_LEAD


def _conv_kernel(zc_ref, prev_ref, next_ref, cfw_ref, cfb_ref, lng_ref, lnb_ref, scw_ref, o_ref, pad_ref,
                 *, tpb):
    j = pl.program_id(0) % tpb
    nseg = TM // SEG
    half = CF_KERNEL // 2
    zeros_lead = jnp.zeros((PAD_LEAD, GROUP_W), F32)

    def glu():
        cfa = zc_ref[:, 3 * GROUP_W:4 * GROUP_W].astype(F32)
        cfg = zc_ref[:, 4 * GROUP_W:5 * GROUP_W].astype(F32)
        return cfa * _sigmoid(cfg)

    def finish_cf(acc, s):
        y = acc + cfb_ref[...]
        yc = y - jnp.mean(y, axis=-1, keepdims=True)
        yn = yc * lax.rsqrt(jnp.mean(yc * yc, axis=-1, keepdims=True) + EPS)
        o_ref[s * SEG:(s + 1) * SEG, 0:GROUP_W] = _silu(yn * lng_ref[...] + lnb_ref[...]).astype(BF16)

    def conformer(stride):
        for s in range(nseg):
            base = s * stride + PAD_LEAD - half
            acc = jnp.zeros((SEG, GROUP_W), F32)
            for k in range(CF_KERNEL):
                acc = acc + cfw_ref[k:k + 1, :] * pad_ref[base + k:base + k + SEG, :]
            finish_cf(acc, s)

    def sc_products():
        scc = zc_ref[:, 0:GROUP_W].astype(F32)
        scv = zc_ref[:, GROUP_W:2 * GROUP_W].astype(F32)
        scb = zc_ref[:, 2 * GROUP_W:3 * GROUP_W].astype(F32)
        return scc * scv, scb

    @pl.when(j != 0)
    def _latent():
        u = glu()
        for s in range(nseg):
            pad_ref[s * LAT_STRIDE:s * LAT_STRIDE + PAD_LEAD, :] = zeros_lead
            pad_ref[s * LAT_STRIDE + PAD_LEAD:(s + 1) * LAT_STRIDE, :] = u[s * SEG:(s + 1) * SEG]
        pad_ref[nseg * LAT_STRIDE:nseg * LAT_STRIDE + PAD_LEAD, :] = zeros_lead
        conformer(LAT_STRIDE)
        usc, scb = sc_products()
        up = prev_ref[:, 0:GROUP_W].astype(F32) * prev_ref[:, GROUP_W:2 * GROUP_W].astype(F32)
        un = next_ref[:, 0:GROUP_W].astype(F32) * next_ref[:, GROUP_W:2 * GROUP_W].astype(F32)
        up = jnp.where(j == 1, 0.0, up)
        un = jnp.where(j == tpb - 1, 0.0, un)
        above = jnp.concatenate([up, usc[:TM - GRID_W]], axis=0)
        below = jnp.concatenate([usc[GRID_W:], un], axis=0)
        y = scb * (scw_ref[0:1, :] * above + scw_ref[1:2, :] * usc + scw_ref[2:3, :] * below)
        o_ref[:, GROUP_W:2 * GROUP_W] = y.astype(BF16)

    @pl.when(j == 0)
    def _context():
        u = glu()
        pad_ref[0:PAD_LEAD, :] = zeros_lead
        pad_ref[PAD_LEAD:PAD_LEAD + TM, :] = u
        pad_ref[PAD_LEAD + TM:2 * PAD_LEAD + TM, :] = zeros_lead
        conformer(SEG)
        usc, scb = sc_products()
        pad_ref[PAD_LEAD:PAD_LEAD + TM, :] = usc
        before = pad_ref[PAD_LEAD - 1:PAD_LEAD - 1 + TM, :]
        after = pad_ref[PAD_LEAD + 1:PAD_LEAD + 1 + TM, :]
        y = scb * (scw_ref[0:1, :] * before + scw_ref[1:2, :] * usc + scw_ref[2:3, :] * after)
        o_ref[:, GROUP_W:2 * GROUP_W] = y.astype(BF16)


def _conv_call(zc, cfw, cfb, lng, lnb, scw, tpb):
    n = zc.shape[0]
    nt = n // TM
    r = TM // GRID_W
    nhalo = n // GRID_W
    return pl.pallas_call(
        functools.partial(_conv_kernel, tpb=tpb),
        grid=(nt,),
        in_specs=[
            pl.BlockSpec((TM, ZC_W), lambda i: (i, 0)),
            pl.BlockSpec((GRID_W, 2 * GROUP_W), lambda i: (jnp.maximum(i * r - 1, 0), 0)),
            pl.BlockSpec((GRID_W, 2 * GROUP_W), lambda i: (jnp.minimum(i * r + r, nhalo - 1), 0)),
            pl.BlockSpec((CF_KERNEL, GROUP_W), lambda i: (0, 0)),
            pl.BlockSpec((1, GROUP_W), lambda i: (0, 0)),
            pl.BlockSpec((1, GROUP_W), lambda i: (0, 0)),
            pl.BlockSpec((1, GROUP_W), lambda i: (0, 0)),
            pl.BlockSpec((SC_KERNEL, GROUP_W), lambda i: (0, 0)),
        ],
        out_specs=pl.BlockSpec((TM, 2 * GROUP_W), lambda i: (i, 0)),
        out_shape=jax.ShapeDtypeStruct((n, 2 * GROUP_W), BF16),
        scratch_shapes=[pltpu.VMEM((PAD_ROWS, GROUP_W), F32)],
        compiler_params=_cparams(("arbitrary",)),
        name="convmix",
    )(zc, zc, zc, cfw, cfb, lng, lnb, scw)


def _scan_block(d, j, tpb):
    return jnp.where(d == 0, j, jnp.where(j == 0, 0, tpb - j))


def _tri(rev, size):
    ri = lax.broadcasted_iota(jnp.int32, (size, size), 0)
    ci = lax.broadcasted_iota(jnp.int32, (size, size), 1)
    return (ri <= ci) if rev else (ri >= ci)


def _gla_kernel(zg_ref, w2_ref, b2_ref, o_ref, st_ref):
    d = pl.program_id(1)

    @pl.when(pl.program_id(2) == 0)
    def _init():
        st_ref[...] = jnp.zeros_like(st_ref)

    def run(rev):
        nk = N_HEADS * HEAD_DK
        zz = _dot(zg_ref[:, 3 * GROUP_W:3 * GROUP_W + LANES], w2_ref[0].astype(BF16)) + b2_ref[0]
        la = _log_sigmoid(zz) * (1.0 / GLA_TAU)
        tri = _tri(rev, GLA_CHUNK)
        trib = tri.astype(BF16)
        order = range(TM // GLA_CHUNK - 1, -1, -1) if rev else range(TM // GLA_CHUNK)
        states = [st_ref[h] for h in range(N_HEADS)]
        for c in order:
            rows = slice(c * GLA_CHUNK, (c + 1) * GLA_CHUNK)
            la_c = la[rows]
            hi = la_c.astype(BF16)
            lo = (la_c - hi.astype(F32)).astype(BF16)
            bc = _dot(trib, hi) + _dot(trib, lo)
            b_last = bc[0:1] if rev else bc[GLA_CHUNK - 1:GLA_CHUNK]
            b_mid = bc[GLA_CHUNK // 2:GLA_CHUNK // 2 + 1]
            q = zg_ref[rows, 0:nk].astype(F32) * (HEAD_DK ** -0.5)
            k = zg_ref[rows, nk:2 * nk].astype(F32)
            q_in = (q * jnp.exp(bc)).astype(BF16)
            q_mid = (q * jnp.exp(bc - b_mid)).astype(BF16)
            k_mid = (k * jnp.exp(b_mid - bc)).astype(BF16)
            k_st = (k * jnp.exp(b_last - bc)).astype(BF16)
            a_row = jnp.exp(b_last)
            for h in range(N_HEADS):
                hs = slice(h * HEAD_DK, (h + 1) * HEAD_DK)
                vs = slice(h * HEAD_DV, (h + 1) * HEAD_DV)
                vh = zg_ref[rows, GROUP_W + h * HEAD_DV:GROUP_W + (h + 1) * HEAD_DV]
                sc = jnp.where(tri, _dot_nt(q_mid[:, hs], k_mid[:, hs]), 0.0).astype(BF16)
                st = states[h]
                o_ref[0, rows, vs] = _dot(sc, vh) + _dot_nt(q_in[:, hs], st.astype(BF16))
                states[h] = st * a_row[:, hs] + _dot_tn(vh, k_st[:, hs])
        for h in range(N_HEADS):
            st_ref[h] = states[h]

    @pl.when(d == 0)
    def _fwd():
        run(False)

    @pl.when(d == 1)
    def _bwd():
        run(True)


def _gla_call(zg, w2pad, b2, bsz, tpb):
    n = zg.shape[0]
    nk = N_HEADS * HEAD_DK

    def rowblk(b, d, j):
        return b * tpb + _scan_block(d, j, tpb)

    return pl.pallas_call(
        _gla_kernel,
        grid=(bsz, 2, tpb),
        in_specs=[
            pl.BlockSpec((TM, ZG_W), lambda b, d, j: (rowblk(b, d, j), 0)),
            pl.BlockSpec((1, LANES, nk), lambda b, d, j: (d, 0, 0)),
            pl.BlockSpec((1, 1, nk), lambda b, d, j: (d, 0, 0)),
        ],
        out_specs=pl.BlockSpec((1, TM, GROUP_W), lambda b, d, j: (d, rowblk(b, d, j), 0)),
        out_shape=jax.ShapeDtypeStruct((2, n, GROUP_W), F32),
        scratch_shapes=[pltpu.VMEM((N_HEADS, HEAD_DV, HEAD_DK), F32)],
        compiler_params=_cparams(("arbitrary", "arbitrary", "arbitrary")),
        name="gla",
    )(zg, w2pad, b2)


def _ret_kernel(zr_ref, cos_ref, sin_ref, lg_ref, o_ref, st_ref):
    d = pl.program_id(1)

    @pl.when(pl.program_id(2) == 0)
    def _init():
        st_ref[...] = jnp.zeros_like(st_ref)

    def run(rev):
        nk = N_HEADS * HEAD_DK
        lane = lax.broadcasted_iota(jnp.int32, (TM, nk), 1)
        first_half = (lane % HEAD_DK) < (HEAD_DK // 2)
        cosf = cos_ref[...]
        sinf = sin_ref[...]

        def rope(x):
            swapped = jnp.where(first_half, pltpu.roll(x, nk - HEAD_DK // 2, 1), pltpu.roll(x, HEAD_DK // 2, 1))
            return x * cosf + swapped * sinf

        q = rope(zr_ref[:, 0:nk].astype(F32)).astype(BF16)
        k = rope(zr_ref[:, nk:2 * nk].astype(F32)) * (HEAD_DK ** -0.5)
        lg = _log_sigmoid(lg_ref[0])
        tri = _tri(rev, TM)
        ri = lax.broadcasted_iota(jnp.int32, (TM, TM), 0)
        ci = lax.broadcasted_iota(jnp.int32, (TM, TM), 1)
        rel = ((ci - ri) if rev else (ri - ci)).astype(F32)
        pos = lax.broadcasted_iota(jnp.int32, (TM, LANES), 0).astype(F32)
        q_steps = (TM - pos) if rev else (pos + 1.0)
        k_steps = pos if rev else (TM - 1.0 - pos)
        lg_wide = jnp.concatenate([lg, lg], axis=1)
        for h in range(N_HEADS):
            hs = slice(h * HEAD_DK, (h + 1) * HEAD_DK)
            vs = slice(h * HEAD_DV, (h + 1) * HEAD_DV)
            lgh = lg[h:h + 1, :]
            decay = jnp.where(tri, jnp.exp(rel * lg_wide[h:h + 1, :]), 0.0)
            q_decay = jnp.exp(q_steps * lgh)
            k_decay = jnp.exp(k_steps[:, 0:HEAD_DK] * lgh[:, 0:HEAD_DK])
            chunk_decay = jnp.exp(float(TM) * lgh[:, 0:HEAD_DK])
            st = st_ref[h]
            qh = q[:, hs]
            kh = k[:, hs]
            vh = zr_ref[:, GROUP_W + h * HEAD_DV:GROUP_W + (h + 1) * HEAD_DV]
            sc = (_dot_nt(qh, kh.astype(BF16)) * decay).astype(BF16)
            o_ref[0, :, vs] = _dot(sc, vh) + _dot_nt(qh, st.astype(BF16)) * q_decay
            st_ref[h] = st * chunk_decay + _dot_tn(vh, (kh * k_decay).astype(BF16))

    @pl.when(d == 0)
    def _fwd():
        run(False)

    @pl.when(d == 1)
    def _bwd():
        run(True)


def _ret_call(zr, cos_t, sin_t, lg, bsz, tpb):
    n = zr.shape[0]
    nk = N_HEADS * HEAD_DK

    def rowblk(b, d, j):
        return b * tpb + _scan_block(d, j, tpb)

    return pl.pallas_call(
        _ret_kernel,
        grid=(bsz, 2, tpb),
        in_specs=[
            pl.BlockSpec((TM, ZR_W), lambda b, d, j: (rowblk(b, d, j), 0)),
            pl.BlockSpec((TM, nk), lambda b, d, j: (_scan_block(d, j, tpb), 0)),
            pl.BlockSpec((TM, nk), lambda b, d, j: (_scan_block(d, j, tpb), 0)),
            pl.BlockSpec((1, N_HEADS, LANES), lambda b, d, j: (d, 0, 0)),
        ],
        out_specs=pl.BlockSpec((1, TM, GROUP_W), lambda b, d, j: (d, rowblk(b, d, j), 0)),
        out_shape=jax.ShapeDtypeStruct((2, n, GROUP_W), F32),
        scratch_shapes=[pltpu.VMEM((N_HEADS, HEAD_DV, HEAD_DK), F32)],
        compiler_params=_cparams(("arbitrary", "arbitrary", "arbitrary")),
        name="retention",
    )(zr, cos_t, sin_t, lg)


def _head_norm(o, g, center):
    outs = []
    for h in range(N_HEADS):
        oh = o[:, h * HEAD_DV:(h + 1) * HEAD_DV]
        if center:
            oh = oh - jnp.mean(oh, axis=-1, keepdims=True)
        outs.append(oh * lax.rsqrt(jnp.mean(oh * oh, axis=-1, keepdims=True) + EPS))
    return jnp.concatenate(outs, axis=1) * g


def _route(logits):
    lane = lax.broadcasted_iota(jnp.int32, logits.shape, 1).astype(F32)
    neg = jnp.float32(-jnp.inf)
    big = jnp.float32(LANES)
    gmask = (lane >= N_EXPERTS) & (lane < N_EXPERTS + N_GROUPS)
    gl = jnp.where(gmask, logits, neg)
    gmax = jnp.max(gl, axis=-1, keepdims=True)
    gidx = jnp.min(jnp.where(gl == gmax, lane - N_EXPERTS, big), axis=-1, keepdims=True)
    g_w = 1.0 / jnp.sum(jnp.where(gmask, jnp.exp(logits - gmax), 0.0), axis=-1, keepdims=True)
    emask = (lane >= gidx * EXPERTS_PER_GROUP) & (lane < (gidx + 1.0) * EXPERTS_PER_GROUP)
    el = jnp.where(emask, logits, neg)
    e1 = jnp.max(el, axis=-1, keepdims=True)
    i1 = jnp.min(jnp.where(el == e1, lane, big), axis=-1, keepdims=True)
    el2 = jnp.where(lane == i1, neg, el)
    e2 = jnp.max(el2, axis=-1, keepdims=True)
    i2 = jnp.min(jnp.where(el2 == e2, lane, big), axis=-1, keepdims=True)
    r = jnp.exp(e2 - e1)
    w1 = g_w / (1.0 + r)
    w2 = g_w * r / (1.0 + r)
    idx = jnp.where(lane == 0.0, i1, jnp.where(lane == 1.0, i2, 0.0)).astype(jnp.int32)
    gate = jnp.where(lane == 0.0, w1, jnp.where(lane == 1.0, w2, 0.0))
    return idx, gate


def _outproj_kernel(mixc_ref, og_ref, or_ref, gr_ref, rg_ref, gng_ref, rng_ref, wout_ref, x_ref, mod_ref, g2_ref,
                    wr_ref, br_ref, xo_ref, h_ref, idx_ref, gate_ref):
    gla = _head_norm(og_ref[0] + og_ref[1], gng_ref[...], False) * _silu(gr_ref[...].astype(F32))
    ret = _head_norm(or_ref[0] + or_ref[1], rng_ref[...], True) * _silu(rg_ref[...].astype(F32))
    y = (_dot(mixc_ref[...], wout_ref[0, 0:2 * GROUP_W, :])
         + _dot(gla.astype(BF16), wout_ref[0, 2 * GROUP_W:3 * GROUP_W, :])
         + _dot(ret.astype(BF16), wout_ref[0, 3 * GROUP_W:4 * GROUP_W, :]))
    x = x_ref[...] + mod_ref[0, 2:3, :] * y
    xo_ref[...] = x
    hn = x * lax.rsqrt(jnp.mean(x * x, axis=-1, keepdims=True) + EPS) * g2_ref[...]
    h = hn * (1.0 + mod_ref[0, 4:5, :]) + mod_ref[0, 3:4, :]
    h_ref[...] = h
    hb = h.astype(BF16)
    w = wr_ref[...]
    w_hi = w.astype(BF16)
    w_lo = (w - w_hi.astype(F32)).astype(BF16)
    h_lo = (h - hb.astype(F32)).astype(BF16)
    logits = _dot(hb, w_hi) + _dot(hb, w_lo) + _dot(h_lo, w_hi) + br_ref[...]
    idx, gate = _route(logits)
    idx_ref[...] = idx
    gate_ref[...] = gate


def _outproj_call(mixc, og, orr, zg, zr, gng, rng, wout, l, x, mods, g2, wr, br, tpb):
    n, d = x.shape
    nt = n // TM
    return pl.pallas_call(
        _outproj_kernel,
        grid=(nt,),
        in_specs=[
            pl.BlockSpec((TM, 2 * GROUP_W), lambda i: (i, 0)),
            pl.BlockSpec((2, TM, GROUP_W), lambda i: (0, i, 0)),
            pl.BlockSpec((2, TM, GROUP_W), lambda i: (0, i, 0)),
            pl.BlockSpec((TM, GROUP_W), lambda i: (i, 2)),
            pl.BlockSpec((TM, GROUP_W), lambda i: (i, 2)),
            pl.BlockSpec((1, GROUP_W), lambda i: (0, 0)),
            pl.BlockSpec((1, GROUP_W), lambda i: (0, 0)),
            pl.BlockSpec((1, d, d), lambda i: (l, 0, 0), pipeline_mode=pl.Buffered(1)),
            pl.BlockSpec((TM, d), lambda i: (i, 0)),
            pl.BlockSpec((1, N_MOD, d), lambda i: (_mod_row(i, tpb), 0, 0)),
            pl.BlockSpec((1, d), lambda i: (0, 0)),
            pl.BlockSpec((d, LANES), lambda i: (0, 0)),
            pl.BlockSpec((1, LANES), lambda i: (0, 0)),
        ],
        out_specs=[
            pl.BlockSpec((TM, d), lambda i: (i, 0)),
            pl.BlockSpec((TM, d), lambda i: (i, 0)),
            pl.BlockSpec((TM, LANES), lambda i: (i, 0)),
            pl.BlockSpec((TM, LANES), lambda i: (i, 0)),
        ],
        out_shape=[
            jax.ShapeDtypeStruct((n, d), F32),
            jax.ShapeDtypeStruct((n, d), F32),
            jax.ShapeDtypeStruct((n, LANES), jnp.int32),
            jax.ShapeDtypeStruct((n, LANES), F32),
        ],
        compiler_params=_cparams(("arbitrary",)),
        name="outproj",
    )(mixc, og, orr, zg, zr, gng, rng, wout, x, mods, g2, wr, br)


def _sc_gather(table, idx):
    n_idx = idx.shape[0]
    width = table.shape[1]
    assert n_idx % (SC_WINDOW * SC_WORKERS) == 0
    per = n_idx // SC_WORKERS
    n_steps = per // SC_WINDOW
    assert n_steps % 2 == 0
    per_pad = (per + LANES - 1) // LANES * LANES
    idx_w = jnp.pad(idx.reshape(SC_WORKERS, per), ((0, 0), (0, per_pad - per)))
    mesh = plsc.VectorSubcoreMesh(core_axis_name="core", subcore_axis_name="subcore")
    n_cores = SC_WORKERS // 16

    @functools.partial(
        pl.kernel,
        out_type=jax.ShapeDtypeStruct((n_idx, width), table.dtype),
        mesh=mesh,
        scratch_types=[
            pltpu.VMEM((per_pad,), jnp.int32),
            pltpu.VMEM((2, SC_WINDOW, width), table.dtype),
            pltpu.SemaphoreType.DMA((2,)),
            pltpu.SemaphoreType.DMA((2,)),
        ],
    )
    def gather_kernel(table_hbm, idx_hbm, out_hbm, idx_v, rows_v, sem_g, sem_w):
        wid = lax.axis_index("subcore") * n_cores + lax.axis_index("core")
        base = wid * per
        pltpu.sync_copy(idx_hbm.at[wid], idx_v)

        def gather(step, buf):
            return pltpu.make_async_copy(table_hbm.at[idx_v.at[pl.ds(step * SC_WINDOW, SC_WINDOW)]],
                                         rows_v.at[buf], sem_g.at[buf])

        def write(step, buf):
            return pltpu.make_async_copy(rows_v.at[buf], out_hbm.at[pl.ds(base + step * SC_WINDOW, SC_WINDOW)],
                                         sem_w.at[buf])

        gather(0, 0).start()

        @pl.loop(0, n_steps, step=2)
        def _(s):
            for buf in range(2):
                step = s + buf
                other = 1 - buf
                gather(step, buf).wait()
                write(step, buf).start()

                @pl.when(step >= 1)
                def _():
                    write(step - 1, other).wait()

                @pl.when(step + 1 < n_steps)
                def _():
                    gather(step + 1, other).start()

        write(n_steps - 1, 1).wait()

    return gather_kernel(table, idx_w)


def _new_expert(blk_e_ref):
    i = pl.program_id(0)
    return (i == 0) | (blk_e_ref[i] != blk_e_ref[jnp.maximum(i - 1, 0)])


def _expert_up_kernel(blk_e_ref, nvalid_ref, x_ref, w1_ref, w3_ref, h_ref, w1b_ref, w3b_ref):
    @pl.when(nvalid_ref[pl.program_id(0)] > 0)
    def _compute():
        @pl.when(_new_expert(blk_e_ref))
        def _cast():
            w1b_ref[...] = w1_ref[0, 0].astype(BF16)
            w3b_ref[...] = w3_ref[0, 0].astype(BF16)

        x = x_ref[...].astype(BF16)
        h1 = _dot(x, w1b_ref[...])
        h3 = _dot(x, w3b_ref[...])
        h_ref[...] = (_silu(h1) * h3).astype(BF16)


def _expert_down_kernel(blk_e_ref, nvalid_ref, h_ref, w2_ref, y_ref, w2b_ref):
    @pl.when(nvalid_ref[pl.program_id(0)] > 0)
    def _compute():
        @pl.when(_new_expert(blk_e_ref))
        def _cast():
            w2b_ref[...] = w2_ref[0, 0].astype(BF16)

        y_ref[...] = _dot(h_ref[...], w2b_ref[...])


def _experts_call(blk_e, nvalid, xs, w1, w3, w2, l):
    n_slots, d = xs.shape
    nb = n_slots // MOE_TB
    hidden = w1.shape[-1]
    up = pl.pallas_call(
        _expert_up_kernel,
        grid_spec=pltpu.PrefetchScalarGridSpec(
            num_scalar_prefetch=2,
            grid=(nb,),
            in_specs=[
                pl.BlockSpec((MOE_TB, d), lambda i, be, nv: (i, 0)),
                pl.BlockSpec((1, 1, d, hidden), lambda i, be, nv: (l, be[i], 0, 0)),
                pl.BlockSpec((1, 1, d, hidden), lambda i, be, nv: (l, be[i], 0, 0)),
            ],
            out_specs=pl.BlockSpec((MOE_TB, hidden), lambda i, be, nv: (i, 0)),
            scratch_shapes=[pltpu.VMEM((d, hidden), BF16), pltpu.VMEM((d, hidden), BF16)],
        ),
        out_shape=jax.ShapeDtypeStruct((n_slots, hidden), BF16),
        compiler_params=_cparams(("arbitrary",)),
        name="expert_up",
    )(blk_e, nvalid, xs, w1, w3)
    return pl.pallas_call(
        _expert_down_kernel,
        grid_spec=pltpu.PrefetchScalarGridSpec(
            num_scalar_prefetch=2,
            grid=(nb,),
            in_specs=[
                pl.BlockSpec((MOE_TB, hidden), lambda i, be, nv: (i, 0)),
                pl.BlockSpec((1, 1, hidden, d), lambda i, be, nv: (l, be[i], 0, 0)),
            ],
            out_specs=pl.BlockSpec((MOE_TB, d), lambda i, be, nv: (i, 0)),
            scratch_shapes=[pltpu.VMEM((hidden, d), BF16)],
        ),
        out_shape=jax.ShapeDtypeStruct((n_slots, d), F32),
        compiler_params=_cparams(("arbitrary",)),
        name="expert_down",
    )(blk_e, nvalid, up, w2)


def _slot_plan(idx):
    n = idx.shape[0]
    n_asg = n * TOP_K
    flat_e = idx[:, :TOP_K].reshape(n_asg)
    order = jnp.argsort(flat_e).astype(jnp.int32)
    experts = jnp.arange(N_EXPERTS, dtype=jnp.int32)
    counts = jnp.sum(flat_e[:, None] == experts[None, :], axis=0, dtype=jnp.int32)
    padded = (counts + MOE_TB - 1) // MOE_TB * MOE_TB
    pad_end = jnp.cumsum(padded)
    pad_start = pad_end - padded
    start = jnp.cumsum(counts) - counts
    n_slots = (n_asg + MOE_TB - 1) // MOE_TB * MOE_TB + N_EXPERTS * MOE_TB
    nb = n_slots // MOE_TB
    blk0 = jnp.arange(nb, dtype=jnp.int32) * MOE_TB
    blk_e = jnp.minimum(jnp.sum(blk0[:, None] >= pad_end[None, :], axis=1, dtype=jnp.int32), N_EXPERTS - 1)
    sel = (blk_e[:, None] == experts[None, :]).astype(jnp.int32)
    blk_cnt = jnp.sum(sel * counts[None, :], axis=1)
    blk_pad0 = jnp.sum(sel * pad_start[None, :], axis=1)
    blk_start = jnp.sum(sel * start[None, :], axis=1)
    nvalid = jnp.clip(blk_cnt - (blk0 - blk_pad0), 0, MOE_TB).astype(jnp.int32)
    blk_w = jnp.where(nvalid > 0, blk_e, jnp.max(jnp.where(nvalid > 0, blk_e, 0)))
    within = jnp.arange(MOE_TB, dtype=jnp.int32)[None, :]
    valid = within < nvalid[:, None]
    pos = jnp.clip((blk_start + blk0 - blk_pad0)[:, None] + within, 0, n_asg - 1)
    asg = order[pos.reshape(n_slots)]
    filler = jnp.arange(n_slots, dtype=jnp.int32) % n
    slot_tok = jnp.where(valid.reshape(n_slots), lax.shift_right_logical(asg, 1), filler).astype(jnp.int32)
    rank = jnp.argsort(order).astype(jnp.int32)
    sel_a = (flat_e[:, None] == experts[None, :]).astype(jnp.int32)
    dest = rank + jnp.sum(sel_a * (pad_start - start)[None, :], axis=1)
    dest_ct = dest.reshape(n, TOP_K).T.reshape(n_asg)
    return blk_w, nvalid, slot_tok, dest_ct


def _final_kernel(x_ref, y_ref, gate_ref, mod_ref, g_ref, o_ref):
    x = _combined(x_ref, y_ref, gate_ref, mod_ref)
    o_ref[0] = x * lax.rsqrt(jnp.mean(x * x, axis=-1, keepdims=True) + EPS) * g_ref[...]


def _final_call(x, y, gate, mods, gf, bsz, tpb):
    n, d = x.shape
    lat = tpb - 1
    return pl.pallas_call(
        _final_kernel,
        grid=(bsz, lat),
        in_specs=[
            pl.BlockSpec((TM, d), lambda b, j: (b * tpb + j + 1, 0)),
            pl.BlockSpec((TOP_K, TM, d), lambda b, j: (0, b * tpb + j + 1, 0)),
            pl.BlockSpec((TM, LANES), lambda b, j: (b * tpb + j + 1, 0)),
            pl.BlockSpec((1, N_MOD, d), lambda b, j: (b, 0, 0)),
            pl.BlockSpec((1, d), lambda b, j: (0, 0)),
        ],
        out_specs=pl.BlockSpec((1, TM, d), lambda b, j: (b, j, 0)),
        out_shape=jax.ShapeDtypeStruct((bsz, lat * TM, d), F32),
        compiler_params=_cparams(("arbitrary", "arbitrary")),
        name="final_norm",
    )(x, y, gate, mods, gf)


def _rope_tables(seq):
    n_freq = HEAD_DK // 4
    t = jnp.arange(seq)
    inv = ROPE_BASE ** (-jnp.arange(n_freq, dtype=F32) / n_freq)
    ang = jnp.concatenate([(t // GRID_W).astype(F32)[:, None] * inv, (t % GRID_W).astype(F32)[:, None] * inv], axis=-1)
    cos = jnp.concatenate([jnp.ones((TM, HEAD_DK // 2), F32), jnp.cos(ang)], axis=0)
    sin = jnp.concatenate([jnp.zeros((TM, HEAD_DK // 2), F32), jnp.sin(ang)], axis=0)
    cos_t = jnp.tile(jnp.concatenate([cos, cos], axis=-1), (1, N_HEADS))
    sin_t = jnp.tile(jnp.concatenate([-sin, sin], axis=-1), (1, N_HEADS))
    return cos_t, sin_t


def kernel(x, c, ctx, c_ctx, norm1_g, norm2_g, ada_w, ada_b, w_in, cf_dw, cf_b, cf_ln_g, cf_ln_b, sc_dw, gla_w2,
           gla_b2, gla_ng, ret_logit, ret_ng, w_out, w_grp, b_grp, w_rt, b_rt, e_w1, e_w3, e_w2, final_g):
    bsz, seq, d = x.shape
    depth = w_in.shape[0]
    assert d == D_MODEL and ctx.shape[1] == TM and seq % TM == 0 and bsz == 2 and TOP_K == 2
    assert w_in.shape[-1] == IN_W
    tpb = 1 + seq // TM
    n = bsz * tpb * TM
    nk = N_HEADS * HEAD_DK

    s8 = jnp.concatenate([c, c_ctx[None, :], jnp.zeros((8 - bsz - 1, d), F32)], axis=0)
    mods_all = _ada_call(s8, ada_w, ada_b)[:, :bsz + 1, :].reshape(depth, bsz + 1, N_MOD, d)

    w_in_p = _pack_w_in(w_in)
    w_out_b = _cast_w_out(w_out)
    cos_t, sin_t = _rope_tables(seq)
    w2pad = jnp.zeros((depth, 2, LANES, nk), F32)
    w2pad = w2pad.at[:, 0, 0:GLA_RANK, :].set(gla_w2[:, 0]).at[:, 1, GLA_RANK:2 * GLA_RANK, :].set(gla_w2[:, 1])
    wr_all = jnp.concatenate([w_rt, w_grp, jnp.zeros((depth, d, LANES - N_EXPERTS - N_GROUPS), F32)], axis=-1)
    br_all = jnp.concatenate([b_rt, b_grp, jnp.zeros((depth, LANES - N_EXPERTS - N_GROUPS), F32)], axis=-1)

    xa = jnp.concatenate([ctx, x], axis=1).reshape(n, d)
    out = None
    prev = None
    for l in range(depth):
        mods = mods_all[l]
        if prev is None:
            zc, zg, zr = _inproj_call(xa, None, mods, norm1_g[l][None, :], w_in_p, l, tpb)
        else:
            zc, zg, zr, xa = _inproj_call(xa, prev, mods, norm1_g[l][None, :], w_in_p, l, tpb)
        mixc = _conv_call(zc, cf_dw[l], cf_b[l][None, :], cf_ln_g[l][None, :], cf_ln_b[l][None, :], sc_dw[l], tpb)
        og = _gla_call(zg, w2pad[l], gla_b2[l][:, None, :], bsz, tpb)
        lg = jnp.broadcast_to(ret_logit[l][:, :, None], (2, N_HEADS, LANES))
        orr = _ret_call(zr, cos_t, sin_t, lg, bsz, tpb)
        xa, h2, idx, gate = _outproj_call(mixc, og, orr, zg, zr, gla_ng[l][None, :], ret_ng[l][None, :], w_out_b, l,
                                          xa, mods, norm2_g[l][None, :], wr_all[l], br_all[l][None, :], tpb)
        blk_e, nvalid, slot_tok, dest_ct = _slot_plan(idx)
        xs = _sc_gather(h2, slot_tok)
        ys = _experts_call(blk_e, nvalid, xs, e_w1, e_w3, e_w2, l)
        y = _sc_gather(ys, dest_ct).reshape(TOP_K, n, d)
        if l == depth - 1:
            out = _final_call(xa, y, gate, mods, final_g[None, :], bsz, tpb)
        else:
            prev = (y, gate, mods)
    return out
```

```python
import functools

import jax
import jax.numpy as jnp
from jax import lax
from jax.experimental import pallas as pl
from jax.experimental.pallas import tpu as pltpu
from jax.experimental.pallas import tpu_sc as plsc

F32 = jnp.float32
BF16 = jnp.bfloat16

D_MODEL = 2048
GRID_W = 64
GROUP_W = D_MODEL // 4
CF_KERNEL = 31
SC_KERNEL = 3
N_HEADS = 4
HEAD_DK = 64
HEAD_DV = 128
GLA_RANK = 16
GLA_TAU = 16.0
GLA_CHUNK = 128
ROPE_BASE = 10000.0
N_GROUPS = 4
EXPERTS_PER_GROUP = 4
N_EXPERTS = N_GROUPS * EXPERTS_PER_GROUP
TOP_K = 2
EXPERT_HIDDEN = D_MODEL // 2
N_MOD = 6
EPS = 1e-6

TM = 256
LANES = 128
ADA_TN = 1024
MOE_TB = 512
VMEM_LIMIT = 56 * 1024 * 1024
SC_WORKERS = 32
SC_WINDOW = 16

ZC_W = 5 * GROUP_W
ZG_W = 3 * GROUP_W + LANES
ZR_W = 3 * GROUP_W
Z_W = ZC_W + ZG_W + ZR_W
IN_W = Z_W - (LANES - 2 * GLA_RANK)
GLR_END = ZC_W + 3 * GROUP_W + 2 * GLA_RANK
PACK_MOVES = (
    (3 * GROUP_W, 0, 2 * GROUP_W),
    (2 * GROUP_W, 2 * GROUP_W, GROUP_W),
    (0, 3 * GROUP_W, 2 * GROUP_W),
    (ZC_W, ZC_W, GLR_END - ZC_W),
    (GLR_END, ZC_W + ZG_W, ZR_W),
)


def _cparams(sem):
    return pltpu.CompilerParams(dimension_semantics=sem, vmem_limit_bytes=VMEM_LIMIT)


def _sigmoid(x):
    return 1.0 / (1.0 + jnp.exp(-x))


def _silu(x):
    return x * _sigmoid(x)


def _log_sigmoid(x):
    return jnp.minimum(x, 0.0) - jnp.log1p(jnp.exp(-jnp.abs(x)))


def _dot(a, b):
    return jnp.dot(a, b, preferred_element_type=F32)


def _dot_nt(a, b):
    return lax.dot_general(a, b, (((1,), (1,)), ((), ())), preferred_element_type=F32)


def _dot_tn(a, b):
    return lax.dot_general(a, b, (((0,), (0,)), ((), ())), preferred_element_type=F32)


def _pack_pairs(x):
    w = x.shape[1] // 2
    xb = x.astype(BF16).astype(F32)
    hi = pltpu.bitcast(xb[:, :w], jnp.uint32)
    lo = pltpu.bitcast(xb[:, w:], jnp.uint32)
    return hi | lax.shift_right_logical(lo, jnp.uint32(16))


def _unpack_pairs(p):
    hi = pltpu.bitcast(p & jnp.uint32(0xFFFF0000), F32)
    lo = pltpu.bitcast(lax.shift_left(p, jnp.uint32(16)), F32)
    return jnp.concatenate([hi, lo], axis=1)


def _mod_row(i, tpb):
    return jnp.where(i % tpb == 0, 2, i // tpb)


def _pack_kernel(w_ref, o_ref):
    for src, dst, width in PACK_MOVES:
        o_ref[0, :, dst:dst + width] = w_ref[0, :, src:src + width].astype(BF16)
    o_ref[0, :, GLR_END:ZC_W + ZG_W] = jnp.zeros((o_ref.shape[1], ZC_W + ZG_W - GLR_END), BF16)


def _pack_w_in(w_in):
    depth, d, _ = w_in.shape
    return pl.pallas_call(
        _pack_kernel,
        grid=(depth, d // TM),
        in_specs=[pl.BlockSpec((1, TM, IN_W), lambda l, i: (l, i, 0))],
        out_specs=pl.BlockSpec((1, TM, Z_W), lambda l, i: (l, i, 0)),
        out_shape=jax.ShapeDtypeStruct((depth, d, Z_W), BF16),
        compiler_params=_cparams(("arbitrary", "arbitrary")),
        name="pack_w_in",
    )(w_in)


def _cast_kernel(w_ref, o_ref):
    o_ref[...] = w_ref[...].astype(BF16)


def _cast_w_out(w_out):
    depth, k, d = w_out.shape
    return pl.pallas_call(
        _cast_kernel,
        grid=(depth, k // TM),
        in_specs=[pl.BlockSpec((1, TM, d), lambda l, i: (l, i, 0))],
        out_specs=pl.BlockSpec((1, TM, d), lambda l, i: (l, i, 0)),
        out_shape=jax.ShapeDtypeStruct((depth, k, d), BF16),
        compiler_params=_cparams(("arbitrary", "arbitrary")),
        name="cast_w_out",
    )(w_out)


def _ada_kernel(s_ref, w_ref, b_ref, o_ref):
    a = _silu(s_ref[...]).astype(BF16)
    o_ref[0] = _dot(a, w_ref[0].astype(BF16)) + b_ref[0]


def _ada_call(s8, ada_w, ada_b):
    depth, d, nm = ada_w.shape
    return pl.pallas_call(
        _ada_kernel,
        grid=(depth, nm // ADA_TN),
        in_specs=[
            pl.BlockSpec((8, d), lambda l, j: (0, 0)),
            pl.BlockSpec((1, d, ADA_TN), lambda l, j: (l, 0, j)),
            pl.BlockSpec((1, 1, ADA_TN), lambda l, j: (l, 0, j)),
        ],
        out_specs=pl.BlockSpec((1, 8, ADA_TN), lambda l, j: (l, 0, j)),
        out_shape=jax.ShapeDtypeStruct((depth, 8, nm), F32),
        compiler_params=_cparams(("arbitrary", "arbitrary")),
        name="adaln",
    )(s8, ada_w, ada_b.reshape(depth, 1, nm))


def _combined(x_ref, y_ref, gate_ref, mod_ref):
    f = gate_ref[:, 0:1] * _unpack_pairs(y_ref[0]) + gate_ref[:, 1:2] * _unpack_pairs(y_ref[1])
    return x_ref[...] + mod_ref[0, 5:6, :] * f


def _inproj_kernel(*refs, first, tpb):
    if first:
        ctx_ref, lat_ref, mod_ref, g_ref, w_ref, zc_ref, zg_ref, zr_ref, xo_ref = refs
        x = jnp.where(pl.program_id(0) % tpb == 0, ctx_ref[...], lat_ref[...])
    else:
        x_ref, y_ref, gate_ref, pmod_ref, mod_ref, g_ref, w_ref, zc_ref, zg_ref, zr_ref, xo_ref = refs
        x = _combined(x_ref, y_ref, gate_ref, pmod_ref)
    xo_ref[...] = x
    y = x * lax.rsqrt(jnp.mean(x * x, axis=-1, keepdims=True) + EPS) * g_ref[...]
    h = (y * (1.0 + mod_ref[0, 1:2, :]) + mod_ref[0, 0:1, :]).astype(BF16)
    off = 0
    for ref, width in ((zc_ref, ZC_W), (zg_ref, ZG_W), (zr_ref, ZR_W)):
        c = 0
        while c < width:
            step = min(512, width - c)
            ref[:, c:c + step] = _dot(h, w_ref[0, :, off + c:off + c + step]).astype(BF16)
            c += step
        off += width


def _inproj_call(stream, mods, g1, w_packed, l, tpb, n):
    d = D_MODEL
    nt = n // TM
    nlat = tpb - 1
    mod_spec = pl.BlockSpec((1, N_MOD, d), lambda i: (_mod_row(i, tpb), 0, 0))
    first = len(stream) == 2
    if first:
        in_specs = [
            pl.BlockSpec((TM, d), lambda i: (i // tpb, 0)),
            pl.BlockSpec((TM, d), lambda i: ((i // tpb) * nlat + jnp.maximum(i % tpb - 1, 0), 0)),
        ]
    else:
        in_specs = [
            pl.BlockSpec((TM, d), lambda i: (i, 0)),
            pl.BlockSpec((TOP_K, TM, d // 2), lambda i: (0, i, 0)),
            pl.BlockSpec((TM, LANES), lambda i: (i, 0)),
            mod_spec,
        ]
    in_specs += [
        mod_spec,
        pl.BlockSpec((1, d), lambda i: (0, 0)),
        pl.BlockSpec((1, d, Z_W), lambda i: (l, 0, 0), pipeline_mode=pl.Buffered(1)),
    ]
    return pl.pallas_call(
        functools.partial(_inproj_kernel, first=first, tpb=tpb),
        grid=(nt,),
        in_specs=in_specs,
        out_specs=[
            pl.BlockSpec((TM, ZC_W), lambda i: (i, 0)),
            pl.BlockSpec((TM, ZG_W), lambda i: (i, 0)),
            pl.BlockSpec((TM, ZR_W), lambda i: (i, 0)),
            pl.BlockSpec((TM, d), lambda i: (i, 0)),
        ],
        out_shape=[
            jax.ShapeDtypeStruct((n, ZC_W), BF16),
            jax.ShapeDtypeStruct((n, ZG_W), BF16),
            jax.ShapeDtypeStruct((n, ZR_W), BF16),
            jax.ShapeDtypeStruct((n, d), F32),
        ],
        compiler_params=_cparams(("arbitrary",)),
        name="inproj",
    )(*stream, mods, g1, w_packed)


PAD_LEAD = 16
SEG = GRID_W
LAT_STRIDE = SEG + PAD_LEAD
PAD_ROWS = (TM // SEG) * LAT_STRIDE + PAD_LEAD


def _conv_kernel(zc_ref, prev_ref, next_ref, cfw_ref, cfb_ref, lng_ref, lnb_ref, scw_ref, o_ref, pad_ref,
                 *, tpb):
    j = pl.program_id(0) % tpb
    nseg = TM // SEG
    half = CF_KERNEL // 2
    zeros_lead = jnp.zeros((PAD_LEAD, GROUP_W), F32)

    def glu():
        cfa = zc_ref[:, 3 * GROUP_W:4 * GROUP_W].astype(F32)
        cfg = zc_ref[:, 4 * GROUP_W:5 * GROUP_W].astype(F32)
        return cfa * _sigmoid(cfg)

    def finish_cf(acc, s):
        y = acc + cfb_ref[...]
        yc = y - jnp.mean(y, axis=-1, keepdims=True)
        yn = yc * lax.rsqrt(jnp.mean(yc * yc, axis=-1, keepdims=True) + EPS)
        o_ref[s * SEG:(s + 1) * SEG, 0:GROUP_W] = _silu(yn * lng_ref[...] + lnb_ref[...]).astype(BF16)

    def conformer(stride):
        for s in range(nseg):
            base = s * stride + PAD_LEAD - half
            acc = jnp.zeros((SEG, GROUP_W), F32)
            for k in range(CF_KERNEL):
                acc = acc + cfw_ref[k:k + 1, :] * pad_ref[base + k:base + k + SEG, :]
            finish_cf(acc, s)

    def sc_products():
        scc = zc_ref[:, 0:GROUP_W].astype(F32)
        scv = zc_ref[:, GROUP_W:2 * GROUP_W].astype(F32)
        scb = zc_ref[:, 2 * GROUP_W:3 * GROUP_W].astype(F32)
        return scc * scv, scb

    @pl.when(j != 0)
    def _latent():
        u = glu()
        for s in range(nseg):
            pad_ref[s * LAT_STRIDE:s * LAT_STRIDE + PAD_LEAD, :] = zeros_lead
            pad_ref[s * LAT_STRIDE + PAD_LEAD:(s + 1) * LAT_STRIDE, :] = u[s * SEG:(s + 1) * SEG]
        pad_ref[nseg * LAT_STRIDE:nseg * LAT_STRIDE + PAD_LEAD, :] = zeros_lead
        conformer(LAT_STRIDE)
        usc, scb = sc_products()
        up = prev_ref[:, 0:GROUP_W].astype(F32) * prev_ref[:, GROUP_W:2 * GROUP_W].astype(F32)
        un = next_ref[:, 0:GROUP_W].astype(F32) * next_ref[:, GROUP_W:2 * GROUP_W].astype(F32)
        up = jnp.where(j == 1, 0.0, up)
        un = jnp.where(j == tpb - 1, 0.0, un)
        above = jnp.concatenate([up, usc[:TM - GRID_W]], axis=0)
        below = jnp.concatenate([usc[GRID_W:], un], axis=0)
        y = scb * (scw_ref[0:1, :] * above + scw_ref[1:2, :] * usc + scw_ref[2:3, :] * below)
        o_ref[:, GROUP_W:2 * GROUP_W] = y.astype(BF16)

    @pl.when(j == 0)
    def _context():
        u = glu()
        pad_ref[0:PAD_LEAD, :] = zeros_lead
        pad_ref[PAD_LEAD:PAD_LEAD + TM, :] = u
        pad_ref[PAD_LEAD + TM:2 * PAD_LEAD + TM, :] = zeros_lead
        conformer(SEG)
        usc, scb = sc_products()
        pad_ref[PAD_LEAD:PAD_LEAD + TM, :] = usc
        before = pad_ref[PAD_LEAD - 1:PAD_LEAD - 1 + TM, :]
        after = pad_ref[PAD_LEAD + 1:PAD_LEAD + 1 + TM, :]
        y = scb * (scw_ref[0:1, :] * before + scw_ref[1:2, :] * usc + scw_ref[2:3, :] * after)
        o_ref[:, GROUP_W:2 * GROUP_W] = y.astype(BF16)


def _conv_call(zc, cfw, cfb, lng, lnb, scw, tpb):
    n = zc.shape[0]
    nt = n // TM
    r = TM // GRID_W
    nhalo = n // GRID_W
    return pl.pallas_call(
        functools.partial(_conv_kernel, tpb=tpb),
        grid=(nt,),
        in_specs=[
            pl.BlockSpec((TM, ZC_W), lambda i: (i, 0)),
            pl.BlockSpec((GRID_W, 2 * GROUP_W), lambda i: (jnp.maximum(i * r - 1, 0), 0)),
            pl.BlockSpec((GRID_W, 2 * GROUP_W), lambda i: (jnp.minimum(i * r + r, nhalo - 1), 0)),
            pl.BlockSpec((CF_KERNEL, GROUP_W), lambda i: (0, 0)),
            pl.BlockSpec((1, GROUP_W), lambda i: (0, 0)),
            pl.BlockSpec((1, GROUP_W), lambda i: (0, 0)),
            pl.BlockSpec((1, GROUP_W), lambda i: (0, 0)),
            pl.BlockSpec((SC_KERNEL, GROUP_W), lambda i: (0, 0)),
        ],
        out_specs=pl.BlockSpec((TM, 2 * GROUP_W), lambda i: (i, 0)),
        out_shape=jax.ShapeDtypeStruct((n, 2 * GROUP_W), BF16),
        scratch_shapes=[pltpu.VMEM((PAD_ROWS, GROUP_W), F32)],
        compiler_params=_cparams(("arbitrary",)),
        name="convmix",
    )(zc, zc, zc, cfw, cfb, lng, lnb, scw)


def _scan_block(d, j, tpb):
    return jnp.where(d == 0, j, jnp.where(j == 0, 0, tpb - j))


def _tri(rev, size):
    ri = lax.broadcasted_iota(jnp.int32, (size, size), 0)
    ci = lax.broadcasted_iota(jnp.int32, (size, size), 1)
    return (ri <= ci) if rev else (ri >= ci)


def _gla_kernel(zg_ref, w2_ref, b2_ref, o_ref, st_ref):
    d = pl.program_id(1)

    @pl.when(pl.program_id(2) == 0)
    def _init():
        st_ref[...] = jnp.zeros_like(st_ref)

    def run(rev):
        nk = N_HEADS * HEAD_DK
        zz = _dot(zg_ref[:, 3 * GROUP_W:3 * GROUP_W + LANES], w2_ref[0].astype(BF16)) + b2_ref[0]
        la = _log_sigmoid(zz) * (1.0 / GLA_TAU)
        tri = _tri(rev, GLA_CHUNK)
        trib = tri.astype(BF16)
        order = range(TM // GLA_CHUNK - 1, -1, -1) if rev else range(TM // GLA_CHUNK)
        states = [st_ref[h] for h in range(N_HEADS)]
        for c in order:
            rows = slice(c * GLA_CHUNK, (c + 1) * GLA_CHUNK)
            la_c = la[rows]
            hi = la_c.astype(BF16)
            lo = (la_c - hi.astype(F32)).astype(BF16)
            bc = _dot(trib, hi) + _dot(trib, lo)
            b_last = bc[0:1] if rev else bc[GLA_CHUNK - 1:GLA_CHUNK]
            b_mid = bc[GLA_CHUNK // 2:GLA_CHUNK // 2 + 1]
            q = zg_ref[rows, 0:nk].astype(F32) * (HEAD_DK ** -0.5)
            k = zg_ref[rows, nk:2 * nk].astype(F32)
            q_in = (q * jnp.exp(bc)).astype(BF16)
            q_mid = (q * jnp.exp(bc - b_mid)).astype(BF16)
            k_mid = (k * jnp.exp(b_mid - bc)).astype(BF16)
            k_st = (k * jnp.exp(b_last - bc)).astype(BF16)
            a_row = jnp.exp(b_last)
            for h in range(N_HEADS):
                hs = slice(h * HEAD_DK, (h + 1) * HEAD_DK)
                vs = slice(h * HEAD_DV, (h + 1) * HEAD_DV)
                vh = zg_ref[rows, GROUP_W + h * HEAD_DV:GROUP_W + (h + 1) * HEAD_DV]
                sc = jnp.where(tri, _dot_nt(q_mid[:, hs], k_mid[:, hs]), 0.0).astype(BF16)
                st = states[h]
                o_ref[0, rows, vs] = _dot(sc, vh) + _dot_nt(q_in[:, hs], st.astype(BF16))
                states[h] = st * a_row[:, hs] + _dot_tn(vh, k_st[:, hs])
        for h in range(N_HEADS):
            st_ref[h] = states[h]

    @pl.when(d == 0)
    def _fwd():
        run(False)

    @pl.when(d == 1)
    def _bwd():
        run(True)


def _gla_call(zg, w2pad, b2, bsz, tpb):
    n = zg.shape[0]
    nk = N_HEADS * HEAD_DK

    def rowblk(b, d, j):
        return b * tpb + _scan_block(d, j, tpb)

    return pl.pallas_call(
        _gla_kernel,
        grid=(bsz, 2, tpb),
        in_specs=[
            pl.BlockSpec((TM, ZG_W), lambda b, d, j: (rowblk(b, d, j), 0)),
            pl.BlockSpec((1, LANES, nk), lambda b, d, j: (d, 0, 0)),
            pl.BlockSpec((1, 1, nk), lambda b, d, j: (d, 0, 0)),
        ],
        out_specs=pl.BlockSpec((1, TM, GROUP_W), lambda b, d, j: (d, rowblk(b, d, j), 0)),
        out_shape=jax.ShapeDtypeStruct((2, n, GROUP_W), F32),
        scratch_shapes=[pltpu.VMEM((N_HEADS, HEAD_DV, HEAD_DK), F32)],
        compiler_params=_cparams(("arbitrary", "arbitrary", "arbitrary")),
        name="gla",
    )(zg, w2pad, b2)


def _ret_kernel(zr_ref, cos_ref, sin_ref, lg_ref, o_ref, st_ref):
    d = pl.program_id(1)

    @pl.when(pl.program_id(2) == 0)
    def _init():
        st_ref[...] = jnp.zeros_like(st_ref)

    def run(rev):
        nk = N_HEADS * HEAD_DK
        lane = lax.broadcasted_iota(jnp.int32, (TM, nk), 1)
        first_half = (lane % HEAD_DK) < (HEAD_DK // 2)
        cosf = cos_ref[...]
        sinf = sin_ref[...]

        def rope(x):
            swapped = jnp.where(first_half, pltpu.roll(x, nk - HEAD_DK // 2, 1), pltpu.roll(x, HEAD_DK // 2, 1))
            return x * cosf + swapped * sinf

        q = rope(zr_ref[:, 0:nk].astype(F32)).astype(BF16)
        k = rope(zr_ref[:, nk:2 * nk].astype(F32)) * (HEAD_DK ** -0.5)
        lg = _log_sigmoid(lg_ref[0])
        tri = _tri(rev, TM)
        ri = lax.broadcasted_iota(jnp.int32, (TM, TM), 0)
        ci = lax.broadcasted_iota(jnp.int32, (TM, TM), 1)
        rel = ((ci - ri) if rev else (ri - ci)).astype(F32)
        pos = lax.broadcasted_iota(jnp.int32, (TM, LANES), 0).astype(F32)
        q_steps = (TM - pos) if rev else (pos + 1.0)
        k_steps = pos if rev else (TM - 1.0 - pos)
        lg_wide = jnp.concatenate([lg, lg], axis=1)
        for h in range(N_HEADS):
            hs = slice(h * HEAD_DK, (h + 1) * HEAD_DK)
            vs = slice(h * HEAD_DV, (h + 1) * HEAD_DV)
            lgh = lg[h:h + 1, :]
            decay = jnp.where(tri, jnp.exp(rel * lg_wide[h:h + 1, :]), 0.0)
            q_decay = jnp.exp(q_steps * lgh)
            k_decay = jnp.exp(k_steps[:, 0:HEAD_DK] * lgh[:, 0:HEAD_DK])
            chunk_decay = jnp.exp(float(TM) * lgh[:, 0:HEAD_DK])
            st = st_ref[h]
            qh = q[:, hs]
            kh = k[:, hs]
            vh = zr_ref[:, GROUP_W + h * HEAD_DV:GROUP_W + (h + 1) * HEAD_DV]
            sc = (_dot_nt(qh, kh.astype(BF16)) * decay).astype(BF16)
            o_ref[0, :, vs] = _dot(sc, vh) + _dot_nt(qh, st.astype(BF16)) * q_decay
            st_ref[h] = st * chunk_decay + _dot_tn(vh, (kh * k_decay).astype(BF16))

    @pl.when(d == 0)
    def _fwd():
        run(False)

    @pl.when(d == 1)
    def _bwd():
        run(True)


def _ret_call(zr, cos_t, sin_t, lg, bsz, tpb):
    n = zr.shape[0]
    nk = N_HEADS * HEAD_DK

    def rowblk(b, d, j):
        return b * tpb + _scan_block(d, j, tpb)

    return pl.pallas_call(
        _ret_kernel,
        grid=(bsz, 2, tpb),
        in_specs=[
            pl.BlockSpec((TM, ZR_W), lambda b, d, j: (rowblk(b, d, j), 0)),
            pl.BlockSpec((TM, nk), lambda b, d, j: (_scan_block(d, j, tpb), 0)),
            pl.BlockSpec((TM, nk), lambda b, d, j: (_scan_block(d, j, tpb), 0)),
            pl.BlockSpec((1, N_HEADS, LANES), lambda b, d, j: (d, 0, 0)),
        ],
        out_specs=pl.BlockSpec((1, TM, GROUP_W), lambda b, d, j: (d, rowblk(b, d, j), 0)),
        out_shape=jax.ShapeDtypeStruct((2, n, GROUP_W), F32),
        scratch_shapes=[pltpu.VMEM((N_HEADS, HEAD_DV, HEAD_DK), F32)],
        compiler_params=_cparams(("arbitrary", "arbitrary", "arbitrary")),
        name="retention",
    )(zr, cos_t, sin_t, lg)


def _head_norm(o, g, center):
    outs = []
    for h in range(N_HEADS):
        oh = o[:, h * HEAD_DV:(h + 1) * HEAD_DV]
        if center:
            oh = oh - jnp.mean(oh, axis=-1, keepdims=True)
        outs.append(oh * lax.rsqrt(jnp.mean(oh * oh, axis=-1, keepdims=True) + EPS))
    return jnp.concatenate(outs, axis=1) * g


def _route(logits):
    lane = lax.broadcasted_iota(jnp.int32, logits.shape, 1).astype(F32)
    neg = jnp.float32(-jnp.inf)
    big = jnp.float32(LANES)
    gmask = (lane >= N_EXPERTS) & (lane < N_EXPERTS + N_GROUPS)
    gl = jnp.where(gmask, logits, neg)
    gmax = jnp.max(gl, axis=-1, keepdims=True)
    gidx = jnp.min(jnp.where(gl == gmax, lane - N_EXPERTS, big), axis=-1, keepdims=True)
    g_w = 1.0 / jnp.sum(jnp.where(gmask, jnp.exp(logits - gmax), 0.0), axis=-1, keepdims=True)
    emask = (lane >= gidx * EXPERTS_PER_GROUP) & (lane < (gidx + 1.0) * EXPERTS_PER_GROUP)
    el = jnp.where(emask, logits, neg)
    e1 = jnp.max(el, axis=-1, keepdims=True)
    i1 = jnp.min(jnp.where(el == e1, lane, big), axis=-1, keepdims=True)
    el2 = jnp.where(lane == i1, neg, el)
    e2 = jnp.max(el2, axis=-1, keepdims=True)
    i2 = jnp.min(jnp.where(el2 == e2, lane, big), axis=-1, keepdims=True)
    r = jnp.exp(e2 - e1)
    w1 = g_w / (1.0 + r)
    w2 = g_w * r / (1.0 + r)
    idx = jnp.where(lane == 0.0, i1, jnp.where(lane == 1.0, i2, 0.0)).astype(jnp.int32)
    gate = jnp.where(lane == 0.0, w1, jnp.where(lane == 1.0, w2, 0.0))
    return idx, gate


def _outproj_kernel(mixc_ref, og_ref, or_ref, gr_ref, rg_ref, gng_ref, rng_ref, wout_ref, x_ref, mod_ref, g2_ref,
                    wr_ref, br_ref, xo_ref, h_ref, idx_ref, gate_ref):
    gla = _head_norm(og_ref[0] + og_ref[1], gng_ref[...], False) * _silu(gr_ref[...].astype(F32))
    ret = _head_norm(or_ref[0] + or_ref[1], rng_ref[...], True) * _silu(rg_ref[...].astype(F32))
    y = (_dot(mixc_ref[...], wout_ref[0, 0:2 * GROUP_W, :])
         + _dot(gla.astype(BF16), wout_ref[0, 2 * GROUP_W:3 * GROUP_W, :])
         + _dot(ret.astype(BF16), wout_ref[0, 3 * GROUP_W:4 * GROUP_W, :]))
    x = x_ref[...] + mod_ref[0, 2:3, :] * y
    xo_ref[...] = x
    hn = x * lax.rsqrt(jnp.mean(x * x, axis=-1, keepdims=True) + EPS) * g2_ref[...]
    h = hn * (1.0 + mod_ref[0, 4:5, :]) + mod_ref[0, 3:4, :]
    h_ref[...] = _pack_pairs(h)
    hb = h.astype(BF16)
    w = wr_ref[...]
    w_hi = w.astype(BF16)
    w_lo = (w - w_hi.astype(F32)).astype(BF16)
    h_lo = (h - hb.astype(F32)).astype(BF16)
    logits = _dot(hb, w_hi) + _dot(hb, w_lo) + _dot(h_lo, w_hi) + br_ref[...]
    idx, gate = _route(logits)
    idx_ref[...] = idx
    gate_ref[...] = gate


def _outproj_call(mixc, og, orr, zg, zr, gng, rng, wout, l, x, mods, g2, wr, br, tpb):
    n, d = x.shape
    nt = n // TM
    return pl.pallas_call(
        _outproj_kernel,
        grid=(nt,),
        in_specs=[
            pl.BlockSpec((TM, 2 * GROUP_W), lambda i: (i, 0)),
            pl.BlockSpec((2, TM, GROUP_W), lambda i: (0, i, 0)),
            pl.BlockSpec((2, TM, GROUP_W), lambda i: (0, i, 0)),
            pl.BlockSpec((TM, GROUP_W), lambda i: (i, 2)),
            pl.BlockSpec((TM, GROUP_W), lambda i: (i, 2)),
            pl.BlockSpec((1, GROUP_W), lambda i: (0, 0)),
            pl.BlockSpec((1, GROUP_W), lambda i: (0, 0)),
            pl.BlockSpec((1, d, d), lambda i: (l, 0, 0), pipeline_mode=pl.Buffered(1)),
            pl.BlockSpec((TM, d), lambda i: (i, 0)),
            pl.BlockSpec((1, N_MOD, d), lambda i: (_mod_row(i, tpb), 0, 0)),
            pl.BlockSpec((1, d), lambda i: (0, 0)),
            pl.BlockSpec((d, LANES), lambda i: (0, 0)),
            pl.BlockSpec((1, LANES), lambda i: (0, 0)),
        ],
        out_specs=[
            pl.BlockSpec((TM, d), lambda i: (i, 0)),
            pl.BlockSpec((TM, d // 2), lambda i: (i, 0)),
            pl.BlockSpec((TM, LANES), lambda i: (i, 0)),
            pl.BlockSpec((TM, LANES), lambda i: (i, 0)),
        ],
        out_shape=[
            jax.ShapeDtypeStruct((n, d), F32),
            jax.ShapeDtypeStruct((n, d // 2), jnp.uint32),
            jax.ShapeDtypeStruct((n, LANES), jnp.int32),
            jax.ShapeDtypeStruct((n, LANES), F32),
        ],
        compiler_params=_cparams(("arbitrary",)),
        name="outproj",
    )(mixc, og, orr, zg, zr, gng, rng, wout, x, mods, g2, wr, br)


def _sc_gather(table, idx):
    n_idx = idx.shape[0]
    width = table.shape[1]
    assert n_idx % (SC_WINDOW * SC_WORKERS) == 0
    per = n_idx // SC_WORKERS
    n_steps = per // SC_WINDOW
    assert n_steps % 2 == 0
    per_pad = (per + LANES - 1) // LANES * LANES
    idx_w = jnp.pad(idx.reshape(SC_WORKERS, per), ((0, 0), (0, per_pad - per)))
    mesh = plsc.VectorSubcoreMesh(core_axis_name="core", subcore_axis_name="subcore")
    n_cores = SC_WORKERS // 16

    @functools.partial(
        pl.kernel,
        out_type=jax.ShapeDtypeStruct((n_idx, width), table.dtype),
        mesh=mesh,
        scratch_types=[
            pltpu.VMEM((per_pad,), jnp.int32),
            pltpu.VMEM((2, SC_WINDOW, width), table.dtype),
            pltpu.SemaphoreType.DMA((2,)),
            pltpu.SemaphoreType.DMA((2,)),
        ],
    )
    def gather_kernel(table_hbm, idx_hbm, out_hbm, idx_v, rows_v, sem_g, sem_w):
        wid = lax.axis_index("subcore") * n_cores + lax.axis_index("core")
        base = wid * per
        pltpu.sync_copy(idx_hbm.at[wid], idx_v)

        def gather(step, buf):
            return pltpu.make_async_copy(table_hbm.at[idx_v.at[pl.ds(step * SC_WINDOW, SC_WINDOW)]],
                                         rows_v.at[buf], sem_g.at[buf])

        def write(step, buf):
            return pltpu.make_async_copy(rows_v.at[buf], out_hbm.at[pl.ds(base + step * SC_WINDOW, SC_WINDOW)],
                                         sem_w.at[buf])

        gather(0, 0).start()

        @pl.loop(0, n_steps, step=2)
        def _(s):
            for buf in range(2):
                step = s + buf
                other = 1 - buf
                gather(step, buf).wait()
                write(step, buf).start()

                @pl.when(step >= 1)
                def _():
                    write(step - 1, other).wait()

                @pl.when(step + 1 < n_steps)
                def _():
                    gather(step + 1, other).start()

        write(n_steps - 1, 1).wait()

    return gather_kernel(table, idx_w)


def _expert_up_kernel(blk_e_ref, nvalid_ref, x_ref, w1_ref, w3_ref, h_ref):
    @pl.when(nvalid_ref[pl.program_id(0)] > 0)
    def _compute():
        x = _unpack_pairs(x_ref[...]).astype(BF16)
        h1 = _dot(x, w1_ref[0, 0].astype(BF16))
        h3 = _dot(x, w3_ref[0, 0].astype(BF16))
        h_ref[...] = (_silu(h1) * h3).astype(BF16)


def _expert_down_kernel(blk_e_ref, nvalid_ref, h_ref, w2_ref, y_ref):
    @pl.when(nvalid_ref[pl.program_id(0)] > 0)
    def _compute():
        y_ref[...] = _pack_pairs(_dot(h_ref[...], w2_ref[0, 0].astype(BF16)))


def _experts_call(blk_e, nvalid, xs, w1, w3, w2, l):
    n_slots = xs.shape[0]
    d = w1.shape[-2]
    nb = n_slots // MOE_TB
    hidden = w1.shape[-1]
    up = pl.pallas_call(
        _expert_up_kernel,
        grid_spec=pltpu.PrefetchScalarGridSpec(
            num_scalar_prefetch=2,
            grid=(nb,),
            in_specs=[
                pl.BlockSpec((MOE_TB, d // 2), lambda i, be, nv: (i, 0)),
                pl.BlockSpec((1, 1, d, hidden), lambda i, be, nv: (l, be[i], 0, 0)),
                pl.BlockSpec((1, 1, d, hidden), lambda i, be, nv: (l, be[i], 0, 0)),
            ],
            out_specs=pl.BlockSpec((MOE_TB, hidden), lambda i, be, nv: (i, 0)),
        ),
        out_shape=jax.ShapeDtypeStruct((n_slots, hidden), BF16),
        compiler_params=_cparams(("arbitrary",)),
        name="expert_up",
    )(blk_e, nvalid, xs, w1, w3)
    return pl.pallas_call(
        _expert_down_kernel,
        grid_spec=pltpu.PrefetchScalarGridSpec(
            num_scalar_prefetch=2,
            grid=(nb,),
            in_specs=[
                pl.BlockSpec((MOE_TB, hidden), lambda i, be, nv: (i, 0)),
                pl.BlockSpec((1, 1, hidden, d), lambda i, be, nv: (l, be[i], 0, 0)),
            ],
            out_specs=pl.BlockSpec((MOE_TB, d // 2), lambda i, be, nv: (i, 0)),
        ),
        out_shape=jax.ShapeDtypeStruct((n_slots, d // 2), jnp.uint32),
        compiler_params=_cparams(("arbitrary",)),
        name="expert_down",
    )(blk_e, nvalid, up, w2)


def _slot_plan(idx):
    n = idx.shape[0]
    n_asg = n * TOP_K
    flat_e = idx[:, :TOP_K].reshape(n_asg)
    order = jnp.argsort(flat_e).astype(jnp.int32)
    experts = jnp.arange(N_EXPERTS, dtype=jnp.int32)
    counts = jnp.sum(flat_e[:, None] == experts[None, :], axis=0, dtype=jnp.int32)
    padded = (counts + MOE_TB - 1) // MOE_TB * MOE_TB
    pad_end = jnp.cumsum(padded)
    pad_start = pad_end - padded
    start = jnp.cumsum(counts) - counts
    n_slots = (n_asg + MOE_TB - 1) // MOE_TB * MOE_TB + N_EXPERTS * MOE_TB
    nb = n_slots // MOE_TB
    blk0 = jnp.arange(nb, dtype=jnp.int32) * MOE_TB
    blk_e = jnp.minimum(jnp.sum(blk0[:, None] >= pad_end[None, :], axis=1, dtype=jnp.int32), N_EXPERTS - 1)
    sel = (blk_e[:, None] == experts[None, :]).astype(jnp.int32)
    blk_cnt = jnp.sum(sel * counts[None, :], axis=1)
    blk_pad0 = jnp.sum(sel * pad_start[None, :], axis=1)
    blk_start = jnp.sum(sel * start[None, :], axis=1)
    nvalid = jnp.clip(blk_cnt - (blk0 - blk_pad0), 0, MOE_TB).astype(jnp.int32)
    blk_w = jnp.where(nvalid > 0, blk_e, jnp.max(jnp.where(nvalid > 0, blk_e, 0)))
    within = jnp.arange(MOE_TB, dtype=jnp.int32)[None, :]
    valid = within < nvalid[:, None]
    pos = jnp.clip((blk_start + blk0 - blk_pad0)[:, None] + within, 0, n_asg - 1)
    asg = order[pos.reshape(n_slots)]
    filler = jnp.arange(n_slots, dtype=jnp.int32) % n
    slot_tok = jnp.where(valid.reshape(n_slots), lax.shift_right_logical(asg, 1), filler).astype(jnp.int32)
    rank = jnp.argsort(order).astype(jnp.int32)
    sel_a = (flat_e[:, None] == experts[None, :]).astype(jnp.int32)
    dest = rank + jnp.sum(sel_a * (pad_start - start)[None, :], axis=1)
    dest_ct = dest.reshape(n, TOP_K).T.reshape(n_asg)
    return blk_w, nvalid, slot_tok, dest_ct


def _final_kernel(x_ref, y_ref, gate_ref, mod_ref, g_ref, o_ref):
    x = _combined(x_ref, y_ref, gate_ref, mod_ref)
    o_ref[0] = x * lax.rsqrt(jnp.mean(x * x, axis=-1, keepdims=True) + EPS) * g_ref[...]


def _final_call(x, y, gate, mods, gf, bsz, tpb):
    n, d = x.shape
    lat = tpb - 1
    return pl.pallas_call(
        _final_kernel,
        grid=(bsz, lat),
        in_specs=[
            pl.BlockSpec((TM, d), lambda b, j: (b * tpb + j + 1, 0)),
            pl.BlockSpec((TOP_K, TM, d // 2), lambda b, j: (0, b * tpb + j + 1, 0)),
            pl.BlockSpec((TM, LANES), lambda b, j: (b * tpb + j + 1, 0)),
            pl.BlockSpec((1, N_MOD, d), lambda b, j: (b, 0, 0)),
            pl.BlockSpec((1, d), lambda b, j: (0, 0)),
        ],
        out_specs=pl.BlockSpec((1, TM, d), lambda b, j: (b, j, 0)),
        out_shape=jax.ShapeDtypeStruct((bsz, lat * TM, d), F32),
        compiler_params=_cparams(("arbitrary", "arbitrary")),
        name="final_norm",
    )(x, y, gate, mods, gf)


def _rope_tables(seq):
    n_freq = HEAD_DK // 4
    t = jnp.arange(seq)
    inv = ROPE_BASE ** (-jnp.arange(n_freq, dtype=F32) / n_freq)
    ang = jnp.concatenate([(t // GRID_W).astype(F32)[:, None] * inv, (t % GRID_W).astype(F32)[:, None] * inv], axis=-1)
    cos = jnp.concatenate([jnp.ones((TM, HEAD_DK // 2), F32), jnp.cos(ang)], axis=0)
    sin = jnp.concatenate([jnp.zeros((TM, HEAD_DK // 2), F32), jnp.sin(ang)], axis=0)
    cos_t = jnp.tile(jnp.concatenate([cos, cos], axis=-1), (1, N_HEADS))
    sin_t = jnp.tile(jnp.concatenate([-sin, sin], axis=-1), (1, N_HEADS))
    return cos_t, sin_t


def kernel(x, c, ctx, c_ctx, norm1_g, norm2_g, ada_w, ada_b, w_in, cf_dw, cf_b, cf_ln_g, cf_ln_b, sc_dw, gla_w2,
           gla_b2, gla_ng, ret_logit, ret_ng, w_out, w_grp, b_grp, w_rt, b_rt, e_w1, e_w3, e_w2, final_g):
    bsz, seq, d = x.shape
    depth = w_in.shape[0]
    assert d == D_MODEL and ctx.shape[1] == TM and seq % TM == 0 and bsz == 2 and TOP_K == 2
    assert w_in.shape[-1] == IN_W
    tpb = 1 + seq // TM
    n = bsz * tpb * TM
    nk = N_HEADS * HEAD_DK

    s8 = jnp.concatenate([c, c_ctx[None, :], jnp.zeros((8 - bsz - 1, d), F32)], axis=0)
    mods_all = _ada_call(s8, ada_w, ada_b)[:, :bsz + 1, :].reshape(depth, bsz + 1, N_MOD, d)

    w_in_p = _pack_w_in(w_in)
    w_out_b = _cast_w_out(w_out)
    cos_t, sin_t = _rope_tables(seq)
    w2pad = jnp.zeros((depth, 2, LANES, nk), F32)
    w2pad = w2pad.at[:, 0, 0:GLA_RANK, :].set(gla_w2[:, 0]).at[:, 1, GLA_RANK:2 * GLA_RANK, :].set(gla_w2[:, 1])
    wr_all = jnp.concatenate([w_rt, w_grp, jnp.zeros((depth, d, LANES - N_EXPERTS - N_GROUPS), F32)], axis=-1)
    br_all = jnp.concatenate([b_rt, b_grp, jnp.zeros((depth, LANES - N_EXPERTS - N_GROUPS), F32)], axis=-1)

    out = None
    stream = (ctx.reshape(bsz * TM, d), x.reshape(bsz * seq, d))
    for l in range(depth):
        mods = mods_all[l]
        zc, zg, zr, xa = _inproj_call(stream, mods, norm1_g[l][None, :], w_in_p, l, tpb, n)
        mixc = _conv_call(zc, cf_dw[l], cf_b[l][None, :], cf_ln_g[l][None, :], cf_ln_b[l][None, :], sc_dw[l], tpb)
        og = _gla_call(zg, w2pad[l], gla_b2[l][:, None, :], bsz, tpb)
        lg = jnp.broadcast_to(ret_logit[l][:, :, None], (2, N_HEADS, LANES))
        orr = _ret_call(zr, cos_t, sin_t, lg, bsz, tpb)
        xa, h2, idx, gate = _outproj_call(mixc, og, orr, zg, zr, gla_ng[l][None, :], ret_ng[l][None, :], w_out_b, l,
                                          xa, mods, norm2_g[l][None, :], wr_all[l], br_all[l][None, :], tpb)
        blk_e, nvalid, slot_tok, dest_ct = _slot_plan(idx)
        xs = _sc_gather(h2, slot_tok)
        ys = _experts_call(blk_e, nvalid, xs, e_w1, e_w3, e_w2, l)
        y = _sc_gather(ys, dest_ct).reshape(TOP_K, n, d // 2)
        if l == depth - 1:
            out = _final_call(xa, y, gate, mods, final_g[None, :], bsz, tpb)
        else:
            stream = (xa, y, gate, mods)
    return out
```

```python
import functools

import jax
import jax.numpy as jnp
from jax import lax
from jax.experimental import pallas as pl
from jax.experimental.pallas import tpu as pltpu
from jax.experimental.pallas import tpu_sc as plsc

F32 = jnp.float32
BF16 = jnp.bfloat16

D_MODEL = 2048
GRID_W = 64
GROUP_W = D_MODEL // 4
CF_KERNEL = 31
SC_KERNEL = 3
N_HEADS = 4
HEAD_DK = 64
HEAD_DV = 128
GLA_RANK = 16
GLA_TAU = 16.0
GLA_CHUNK = 128
ROPE_BASE = 10000.0
N_GROUPS = 4
EXPERTS_PER_GROUP = 4
N_EXPERTS = N_GROUPS * EXPERTS_PER_GROUP
TOP_K = 2
EXPERT_HIDDEN = D_MODEL // 2
N_MOD = 6
EPS = 1e-6

TM = 256
LANES = 128
SUBLANES = 8
ADA_TN = 1024
MOE_TB = 512
VMEM_LIMIT = 56 * 1024 * 1024
SC_WORKERS = 32
SC_WINDOW = 16

ZC_W = 5 * GROUP_W
ZG_W = 3 * GROUP_W + LANES
ZR_W = 3 * GROUP_W
Z_W = ZC_W + ZG_W + ZR_W
IN_W = Z_W - (LANES - 2 * GLA_RANK)
GLR_END = ZC_W + 3 * GROUP_W + 2 * GLA_RANK
PACK_MOVES = (
    (3 * GROUP_W, 0, 2 * GROUP_W),
    (2 * GROUP_W, 2 * GROUP_W, GROUP_W),
    (0, 3 * GROUP_W, 2 * GROUP_W),
    (ZC_W, ZC_W, GLR_END - ZC_W),
    (GLR_END, ZC_W + ZG_W, ZR_W),
)


def _cparams(sem):
    return pltpu.CompilerParams(dimension_semantics=sem, vmem_limit_bytes=VMEM_LIMIT)


def _sigmoid(x):
    return 1.0 / (1.0 + jnp.exp(-x))


def _silu(x):
    return x * _sigmoid(x)


def _log_sigmoid(x):
    return jnp.minimum(x, 0.0) - jnp.log1p(jnp.exp(-jnp.abs(x)))


def _dot(a, b):
    return jnp.dot(a, b, preferred_element_type=F32)


def _dot_nt(a, b):
    return lax.dot_general(a, b, (((1,), (1,)), ((), ())), preferred_element_type=F32)


def _dot_tn(a, b):
    return lax.dot_general(a, b, (((0,), (0,)), ((), ())), preferred_element_type=F32)


def _pack_pairs(x):
    w = x.shape[1] // 2
    xb = x.astype(BF16).astype(F32)
    hi = pltpu.bitcast(xb[:, :w], jnp.uint32)
    lo = pltpu.bitcast(xb[:, w:], jnp.uint32)
    return hi | lax.shift_right_logical(lo, jnp.uint32(16))


def _unpack_pairs(p):
    hi = pltpu.bitcast(p & jnp.uint32(0xFFFF0000), F32)
    lo = pltpu.bitcast(lax.shift_left(p, jnp.uint32(16)), F32)
    return jnp.concatenate([hi, lo], axis=1)


def _mod_row(i, tpb):
    return jnp.where(i % tpb == 0, 2, i // tpb)


def _pack_kernel(w_ref, o_ref):
    for src, dst, width in PACK_MOVES:
        o_ref[:, dst:dst + width] = w_ref[:, src:src + width].astype(BF16)
    o_ref[:, GLR_END:ZC_W + ZG_W] = jnp.zeros((o_ref.shape[0], ZC_W + ZG_W - GLR_END), BF16)


def _pack_w_in(w_in):
    depth, d, _ = w_in.shape
    return pl.pallas_call(
        _pack_kernel,
        grid=(depth * d // TM,),
        in_specs=[pl.BlockSpec((TM, IN_W), lambda i: (i, 0))],
        out_specs=pl.BlockSpec((TM, Z_W), lambda i: (i, 0)),
        out_shape=jax.ShapeDtypeStruct((depth * d, Z_W), BF16),
        compiler_params=_cparams(("arbitrary",)),
        name="pack_w_in",
    )(w_in.reshape(depth * d, IN_W)).reshape(depth, d, Z_W)


def _cast_kernel(w_ref, o_ref):
    o_ref[...] = w_ref[...].astype(BF16)


def _cast_w_out(w_out):
    depth, k, d = w_out.shape
    return pl.pallas_call(
        _cast_kernel,
        grid=(depth, k // TM),
        in_specs=[pl.BlockSpec((1, TM, d), lambda l, i: (l, i, 0))],
        out_specs=pl.BlockSpec((1, TM, d), lambda l, i: (l, i, 0)),
        out_shape=jax.ShapeDtypeStruct((depth, k, d), BF16),
        compiler_params=_cparams(("arbitrary", "arbitrary")),
        name="cast_w_out",
    )(w_out)


def _ada_kernel(s_ref, w_ref, b_ref, o_ref):
    a = _silu(s_ref[...]).astype(BF16)
    o_ref[0] = _dot(a, w_ref[0].astype(BF16)) + b_ref[0]


def _ada_call(s8, ada_w, ada_b):
    depth, d, nm = ada_w.shape
    return pl.pallas_call(
        _ada_kernel,
        grid=(depth, nm // ADA_TN),
        in_specs=[
            pl.BlockSpec((8, d), lambda l, j: (0, 0)),
            pl.BlockSpec((1, d, ADA_TN), lambda l, j: (l, 0, j)),
            pl.BlockSpec((1, 1, ADA_TN), lambda l, j: (l, 0, j)),
        ],
        out_specs=pl.BlockSpec((1, 8, ADA_TN), lambda l, j: (l, 0, j)),
        out_shape=jax.ShapeDtypeStruct((depth, 8, nm), F32),
        compiler_params=_cparams(("arbitrary", "arbitrary")),
        name="adaln",
    )(s8, ada_w, ada_b.reshape(depth, 1, nm))


def _combined(x_ref, y_ref, gate_ref, mod_ref):
    f = gate_ref[:, 0:1] * _unpack_pairs(y_ref[0]) + gate_ref[:, 1:2] * _unpack_pairs(y_ref[1])
    return x_ref[...] + mod_ref[0, 5:6, :] * f


def _inproj_kernel(*refs, first, tpb):
    if first:
        ctx_ref, lat_ref, mod_ref, g_ref, w_ref, zc_ref, zg_ref, zr_ref, xo_ref = refs
        x = jnp.where(pl.program_id(0) % tpb == 0, ctx_ref[...], lat_ref[...])
    else:
        x_ref, y_ref, gate_ref, pmod_ref, mod_ref, g_ref, w_ref, zc_ref, zg_ref, zr_ref, xo_ref = refs
        x = _combined(x_ref, y_ref, gate_ref, pmod_ref)
    xo_ref[...] = x
    y = x * lax.rsqrt(jnp.mean(x * x, axis=-1, keepdims=True) + EPS) * g_ref[...]
    h = (y * (1.0 + mod_ref[0, 1:2, :]) + mod_ref[0, 0:1, :]).astype(BF16)
    off = 0
    for ref, width in ((zc_ref, ZC_W), (zg_ref, ZG_W), (zr_ref, ZR_W)):
        c = 0
        while c < width:
            step = min(512, width - c)
            ref[:, c:c + step] = _dot(h, w_ref[0, :, off + c:off + c + step]).astype(BF16)
            c += step
        off += width


def _inproj_call(stream, mods, g1, w_packed, l, tpb, n):
    d = D_MODEL
    nt = n // TM
    nlat = tpb - 1
    mod_spec = pl.BlockSpec((1, N_MOD, d), lambda i: (_mod_row(i, tpb), 0, 0))
    first = len(stream) == 2
    if first:
        in_specs = [
            pl.BlockSpec((TM, d), lambda i: (i // tpb, 0)),
            pl.BlockSpec((TM, d), lambda i: ((i // tpb) * nlat + jnp.maximum(i % tpb - 1, 0), 0)),
        ]
    else:
        in_specs = [
            pl.BlockSpec((TM, d), lambda i: (i, 0)),
            pl.BlockSpec((TOP_K, TM, d // 2), lambda i: (0, i, 0)),
            pl.BlockSpec((TM, LANES), lambda i: (i, 0)),
            mod_spec,
        ]
    in_specs += [
        mod_spec,
        pl.BlockSpec((1, d), lambda i: (0, 0)),
        pl.BlockSpec((1, d, Z_W), lambda i: (l, 0, 0), pipeline_mode=pl.Buffered(1)),
    ]
    return pl.pallas_call(
        functools.partial(_inproj_kernel, first=first, tpb=tpb),
        grid=(nt,),
        in_specs=in_specs,
        out_specs=[
            pl.BlockSpec((TM, ZC_W), lambda i: (i, 0)),
            pl.BlockSpec((TM, ZG_W), lambda i: (i, 0)),
            pl.BlockSpec((TM, ZR_W), lambda i: (i, 0)),
            pl.BlockSpec((TM, d), lambda i: (i, 0)),
        ],
        out_shape=[
            jax.ShapeDtypeStruct((n, ZC_W), BF16),
            jax.ShapeDtypeStruct((n, ZG_W), BF16),
            jax.ShapeDtypeStruct((n, ZR_W), BF16),
            jax.ShapeDtypeStruct((n, d), F32),
        ],
        compiler_params=_cparams(("arbitrary",)),
        name="inproj",
    )(*stream, mods, g1, w_packed)


PAD_LEAD = 16
SEG = GRID_W
LAT_STRIDE = SEG + PAD_LEAD
PAD_ROWS = (TM // SEG) * LAT_STRIDE + PAD_LEAD


def _conv_kernel(zc_ref, prev_ref, next_ref, cfw_ref, cfb_ref, lng_ref, lnb_ref, scw_ref, o_ref, pad_ref, shift_ref,
                 *, tpb):
    j = pl.program_id(0) % tpb
    nseg = TM // SEG
    half = CF_KERNEL // 2
    zeros_lead = jnp.zeros((PAD_LEAD, GROUP_W), F32)

    def glu():
        cfa = zc_ref[:, 3 * GROUP_W:4 * GROUP_W].astype(F32)
        cfg = zc_ref[:, 4 * GROUP_W:5 * GROUP_W].astype(F32)
        return cfa * _sigmoid(cfg)

    def finish_cf(acc, s):
        y = acc + cfb_ref[...]
        yc = y - jnp.mean(y, axis=-1, keepdims=True)
        yn = yc * lax.rsqrt(jnp.mean(yc * yc, axis=-1, keepdims=True) + EPS)
        o_ref[s * SEG:(s + 1) * SEG, 0:GROUP_W] = _silu(yn * lng_ref[...] + lnb_ref[...]).astype(BF16)

    def conformer(stride):
        span = PAD_ROWS - SUBLANES
        for r in range(1, SUBLANES):
            shift_ref[r, 0:span, :] = pad_ref[r:r + span, :]
        for s in range(nseg):
            base = s * stride + PAD_LEAD - half
            acc = jnp.zeros((SEG, GROUP_W), F32)
            for k in range(CF_KERNEL):
                r = (base + k) % SUBLANES
                a = base + k - r
                win = pad_ref[a:a + SEG, :] if r == 0 else shift_ref[r, a:a + SEG, :]
                acc = acc + cfw_ref[k:k + 1, :] * win
            finish_cf(acc, s)

    def sc_products():
        scc = zc_ref[:, 0:GROUP_W].astype(F32)
        scv = zc_ref[:, GROUP_W:2 * GROUP_W].astype(F32)
        scb = zc_ref[:, 2 * GROUP_W:3 * GROUP_W].astype(F32)
        return scc * scv, scb

    @pl.when(j != 0)
    def _latent():
        u = glu()
        for s in range(nseg):
            pad_ref[s * LAT_STRIDE:s * LAT_STRIDE + PAD_LEAD, :] = zeros_lead
            pad_ref[s * LAT_STRIDE + PAD_LEAD:(s + 1) * LAT_STRIDE, :] = u[s * SEG:(s + 1) * SEG]
        pad_ref[nseg * LAT_STRIDE:nseg * LAT_STRIDE + PAD_LEAD, :] = zeros_lead
        conformer(LAT_STRIDE)
        usc, scb = sc_products()
        up = prev_ref[:, 0:GROUP_W].astype(F32) * prev_ref[:, GROUP_W:2 * GROUP_W].astype(F32)
        un = next_ref[:, 0:GROUP_W].astype(F32) * next_ref[:, GROUP_W:2 * GROUP_W].astype(F32)
        up = jnp.where(j == 1, 0.0, up)
        un = jnp.where(j == tpb - 1, 0.0, un)
        above = jnp.concatenate([up, usc[:TM - GRID_W]], axis=0)
        below = jnp.concatenate([usc[GRID_W:], un], axis=0)
        y = scb * (scw_ref[0:1, :] * above + scw_ref[1:2, :] * usc + scw_ref[2:3, :] * below)
        o_ref[:, GROUP_W:2 * GROUP_W] = y.astype(BF16)

    @pl.when(j == 0)
    def _context():
        u = glu()
        pad_ref[0:PAD_LEAD, :] = zeros_lead
        pad_ref[PAD_LEAD:PAD_LEAD + TM, :] = u
        pad_ref[PAD_LEAD + TM:PAD_ROWS, :] = jnp.zeros((PAD_ROWS - PAD_LEAD - TM, GROUP_W), F32)
        conformer(SEG)
        usc, scb = sc_products()
        pad_ref[PAD_LEAD:PAD_LEAD + TM, :] = usc
        before = pad_ref[PAD_LEAD - 1:PAD_LEAD - 1 + TM, :]
        after = pad_ref[PAD_LEAD + 1:PAD_LEAD + 1 + TM, :]
        y = scb * (scw_ref[0:1, :] * before + scw_ref[1:2, :] * usc + scw_ref[2:3, :] * after)
        o_ref[:, GROUP_W:2 * GROUP_W] = y.astype(BF16)


def _conv_call(zc, cfw, cfb, lng, lnb, scw, tpb):
    n = zc.shape[0]
    nt = n // TM
    r = TM // GRID_W
    nhalo = n // GRID_W
    return pl.pallas_call(
        functools.partial(_conv_kernel, tpb=tpb),
        grid=(nt,),
        in_specs=[
            pl.BlockSpec((TM, ZC_W), lambda i: (i, 0)),
            pl.BlockSpec((GRID_W, 2 * GROUP_W), lambda i: (jnp.maximum(i * r - 1, 0), 0)),
            pl.BlockSpec((GRID_W, 2 * GROUP_W), lambda i: (jnp.minimum(i * r + r, nhalo - 1), 0)),
            pl.BlockSpec((CF_KERNEL, GROUP_W), lambda i: (0, 0)),
            pl.BlockSpec((1, GROUP_W), lambda i: (0, 0)),
            pl.BlockSpec((1, GROUP_W), lambda i: (0, 0)),
            pl.BlockSpec((1, GROUP_W), lambda i: (0, 0)),
            pl.BlockSpec((SC_KERNEL, GROUP_W), lambda i: (0, 0)),
        ],
        out_specs=pl.BlockSpec((TM, 2 * GROUP_W), lambda i: (i, 0)),
        out_shape=jax.ShapeDtypeStruct((n, 2 * GROUP_W), BF16),
        scratch_shapes=[pltpu.VMEM((PAD_ROWS, GROUP_W), F32), pltpu.VMEM((SUBLANES, PAD_ROWS, GROUP_W), F32)],
        compiler_params=_cparams(("arbitrary",)),
        name="convmix",
    )(zc, zc, zc, cfw, cfb, lng, lnb, scw)


def _scan_block(d, j, tpb):
    return jnp.where(d == 0, j, jnp.where(j == 0, 0, tpb - j))


def _tri(rev, size):
    ri = lax.broadcasted_iota(jnp.int32, (size, size), 0)
    ci = lax.broadcasted_iota(jnp.int32, (size, size), 1)
    return (ri <= ci) if rev else (ri >= ci)


def _gla_kernel(zg_ref, w2_ref, b2_ref, o_ref, st_ref):
    d = pl.program_id(0)

    @pl.when(pl.program_id(1) == 0)
    def _init():
        st_ref[...] = jnp.zeros_like(st_ref)

    def run(rev):
        for b in range(zg_ref.shape[0]):
            run_batch(rev, zg_ref.at[b], o_ref.at[0, b], st_ref.at[b])

    def run_batch(rev, zg_ref, o_ref, st_ref):
        nk = N_HEADS * HEAD_DK
        zz = _dot(zg_ref[:, 3 * GROUP_W:3 * GROUP_W + LANES], w2_ref[0].astype(BF16)) + b2_ref[0]
        la = _log_sigmoid(zz) * (1.0 / GLA_TAU)
        tri = _tri(rev, GLA_CHUNK)
        trib = tri.astype(BF16)
        order = range(TM // GLA_CHUNK - 1, -1, -1) if rev else range(TM // GLA_CHUNK)
        states = [st_ref[h] for h in range(N_HEADS)]
        for c in order:
            rows = slice(c * GLA_CHUNK, (c + 1) * GLA_CHUNK)
            la_c = la[rows]
            hi = la_c.astype(BF16)
            lo = (la_c - hi.astype(F32)).astype(BF16)
            bc = _dot(trib, hi) + _dot(trib, lo)
            b_last = bc[0:1] if rev else bc[GLA_CHUNK - 1:GLA_CHUNK]
            b_mid = bc[GLA_CHUNK // 2:GLA_CHUNK // 2 + 1]
            q = zg_ref[rows, 0:nk].astype(F32) * (HEAD_DK ** -0.5)
            k = zg_ref[rows, nk:2 * nk].astype(F32)
            q_in = (q * jnp.exp(bc)).astype(BF16)
            q_mid = (q * jnp.exp(bc - b_mid)).astype(BF16)
            k_mid = (k * jnp.exp(b_mid - bc)).astype(BF16)
            k_st = (k * jnp.exp(b_last - bc)).astype(BF16)
            a_row = jnp.exp(b_last)
            for h in range(N_HEADS):
                hs = slice(h * HEAD_DK, (h + 1) * HEAD_DK)
                vs = slice(h * HEAD_DV, (h + 1) * HEAD_DV)
                vh = zg_ref[rows, GROUP_W + h * HEAD_DV:GROUP_W + (h + 1) * HEAD_DV]
                sc = jnp.where(tri, _dot_nt(q_mid[:, hs], k_mid[:, hs]), 0.0).astype(BF16)
                st = states[h]
                o_ref[rows, vs] = _dot(sc, vh) + _dot_nt(q_in[:, hs], st.astype(BF16))
                states[h] = st * a_row[:, hs] + _dot_tn(vh, k_st[:, hs])
        for h in range(N_HEADS):
            st_ref[h] = states[h]

    @pl.when(d == 0)
    def _fwd():
        run(False)

    @pl.when(d == 1)
    def _bwd():
        run(True)


def _gla_call(zg, w2pad, b2, bsz, tpb):
    n = zg.shape[0]
    nk = N_HEADS * HEAD_DK
    return pl.pallas_call(
        _gla_kernel,
        grid=(2, tpb),
        in_specs=[
            pl.BlockSpec((bsz, TM, ZG_W), lambda d, j: (0, _scan_block(d, j, tpb), 0)),
            pl.BlockSpec((1, LANES, nk), lambda d, j: (d, 0, 0)),
            pl.BlockSpec((1, 1, nk), lambda d, j: (d, 0, 0)),
        ],
        out_specs=pl.BlockSpec((1, bsz, TM, GROUP_W), lambda d, j: (d, 0, _scan_block(d, j, tpb), 0)),
        out_shape=jax.ShapeDtypeStruct((2, bsz, n // bsz, GROUP_W), F32),
        scratch_shapes=[pltpu.VMEM((bsz, N_HEADS, HEAD_DV, HEAD_DK), F32)],
        compiler_params=_cparams(("arbitrary", "arbitrary")),
        name="gla",
    )(zg.reshape(bsz, n // bsz, ZG_W), w2pad, b2).reshape(2, n, GROUP_W)


def _ret_kernel(zr_ref, cos_ref, sin_ref, lg_ref, o_ref, st_ref):
    d = pl.program_id(0)

    @pl.when(pl.program_id(1) == 0)
    def _init():
        st_ref[...] = jnp.zeros_like(st_ref)

    def run(rev):
        for b in range(zr_ref.shape[0]):
            run_batch(rev, zr_ref.at[b], o_ref.at[0, b], st_ref.at[b])

    def run_batch(rev, zr_ref, o_ref, st_ref):
        nk = N_HEADS * HEAD_DK
        lane = lax.broadcasted_iota(jnp.int32, (TM, nk), 1)
        first_half = (lane % HEAD_DK) < (HEAD_DK // 2)
        cosf = cos_ref[...]
        sinf = sin_ref[...]

        def rope(x):
            swapped = jnp.where(first_half, pltpu.roll(x, nk - HEAD_DK // 2, 1), pltpu.roll(x, HEAD_DK // 2, 1))
            return x * cosf + swapped * sinf

        q = rope(zr_ref[:, 0:nk].astype(F32)).astype(BF16)
        k = rope(zr_ref[:, nk:2 * nk].astype(F32)) * (HEAD_DK ** -0.5)
        lg = _log_sigmoid(lg_ref[0])
        tri = _tri(rev, TM)
        ri = lax.broadcasted_iota(jnp.int32, (TM, TM), 0)
        ci = lax.broadcasted_iota(jnp.int32, (TM, TM), 1)
        rel = ((ci - ri) if rev else (ri - ci)).astype(F32)
        pos = lax.broadcasted_iota(jnp.int32, (TM, LANES), 0).astype(F32)
        q_steps = (TM - pos) if rev else (pos + 1.0)
        k_steps = pos if rev else (TM - 1.0 - pos)
        lg_wide = jnp.concatenate([lg, lg], axis=1)
        for h in range(N_HEADS):
            hs = slice(h * HEAD_DK, (h + 1) * HEAD_DK)
            vs = slice(h * HEAD_DV, (h + 1) * HEAD_DV)
            lgh = lg[h:h + 1, :]
            decay = jnp.where(tri, jnp.exp(rel * lg_wide[h:h + 1, :]), 0.0)
            q_decay = jnp.exp(q_steps * lgh)
            k_decay = jnp.exp(k_steps[:, 0:HEAD_DK] * lgh[:, 0:HEAD_DK])
            chunk_decay = jnp.exp(float(TM) * lgh[:, 0:HEAD_DK])
            st = st_ref[h]
            qh = q[:, hs]
            kh = k[:, hs]
            vh = zr_ref[:, GROUP_W + h * HEAD_DV:GROUP_W + (h + 1) * HEAD_DV]
            sc = (_dot_nt(qh, kh.astype(BF16)) * decay).astype(BF16)
            o_ref[:, vs] = _dot(sc, vh) + _dot_nt(qh, st.astype(BF16)) * q_decay
            st_ref[h] = st * chunk_decay + _dot_tn(vh, (kh * k_decay).astype(BF16))

    @pl.when(d == 0)
    def _fwd():
        run(False)

    @pl.when(d == 1)
    def _bwd():
        run(True)


def _ret_call(zr, cos_t, sin_t, lg, bsz, tpb):
    n = zr.shape[0]
    nk = N_HEADS * HEAD_DK
    return pl.pallas_call(
        _ret_kernel,
        grid=(2, tpb),
        in_specs=[
            pl.BlockSpec((bsz, TM, ZR_W), lambda d, j: (0, _scan_block(d, j, tpb), 0)),
            pl.BlockSpec((TM, nk), lambda d, j: (_scan_block(d, j, tpb), 0)),
            pl.BlockSpec((TM, nk), lambda d, j: (_scan_block(d, j, tpb), 0)),
            pl.BlockSpec((1, N_HEADS, LANES), lambda d, j: (d, 0, 0)),
        ],
        out_specs=pl.BlockSpec((1, bsz, TM, GROUP_W), lambda d, j: (d, 0, _scan_block(d, j, tpb), 0)),
        out_shape=jax.ShapeDtypeStruct((2, bsz, n // bsz, GROUP_W), F32),
        scratch_shapes=[pltpu.VMEM((bsz, N_HEADS, HEAD_DV, HEAD_DK), F32)],
        compiler_params=_cparams(("arbitrary", "arbitrary")),
        name="retention",
    )(zr.reshape(bsz, n // bsz, ZR_W), cos_t, sin_t, lg).reshape(2, n, GROUP_W)


def _head_norm(o, g, center):
    outs = []
    for h in range(N_HEADS):
        oh = o[:, h * HEAD_DV:(h + 1) * HEAD_DV]
        if center:
            oh = oh - jnp.mean(oh, axis=-1, keepdims=True)
        outs.append(oh * lax.rsqrt(jnp.mean(oh * oh, axis=-1, keepdims=True) + EPS))
    return jnp.concatenate(outs, axis=1) * g


def _route(logits):
    lane = lax.broadcasted_iota(jnp.int32, logits.shape, 1).astype(F32)
    neg = jnp.float32(-jnp.inf)
    big = jnp.float32(LANES)
    gmask = (lane >= N_EXPERTS) & (lane < N_EXPERTS + N_GROUPS)
    gl = jnp.where(gmask, logits, neg)
    gmax = jnp.max(gl, axis=-1, keepdims=True)
    gidx = jnp.min(jnp.where(gl == gmax, lane - N_EXPERTS, big), axis=-1, keepdims=True)
    g_w = 1.0 / jnp.sum(jnp.where(gmask, jnp.exp(logits - gmax), 0.0), axis=-1, keepdims=True)
    emask = (lane >= gidx * EXPERTS_PER_GROUP) & (lane < (gidx + 1.0) * EXPERTS_PER_GROUP)
    el = jnp.where(emask, logits, neg)
    e1 = jnp.max(el, axis=-1, keepdims=True)
    i1 = jnp.min(jnp.where(el == e1, lane, big), axis=-1, keepdims=True)
    el2 = jnp.where(lane == i1, neg, el)
    e2 = jnp.max(el2, axis=-1, keepdims=True)
    i2 = jnp.min(jnp.where(el2 == e2, lane, big), axis=-1, keepdims=True)
    r = jnp.exp(e2 - e1)
    w1 = g_w / (1.0 + r)
    w2 = g_w * r / (1.0 + r)
    idx = jnp.where(lane == 0.0, i1, jnp.where(lane == 1.0, i2, 0.0)).astype(jnp.int32)
    gate = jnp.where(lane == 0.0, w1, jnp.where(lane == 1.0, w2, 0.0))
    return idx, gate


def _outproj_kernel(mixc_ref, og_ref, or_ref, gr_ref, rg_ref, gng_ref, rng_ref, wout_ref, x_ref, mod_ref, g2_ref,
                    wr_ref, br_ref, xo_ref, h_ref, idx_ref, gate_ref):
    gla = _head_norm(og_ref[0] + og_ref[1], gng_ref[...], False) * _silu(gr_ref[...].astype(F32))
    ret = _head_norm(or_ref[0] + or_ref[1], rng_ref[...], True) * _silu(rg_ref[...].astype(F32))
    y = (_dot(mixc_ref[...], wout_ref[0, 0:2 * GROUP_W, :])
         + _dot(gla.astype(BF16), wout_ref[0, 2 * GROUP_W:3 * GROUP_W, :])
         + _dot(ret.astype(BF16), wout_ref[0, 3 * GROUP_W:4 * GROUP_W, :]))
    x = x_ref[...] + mod_ref[0, 2:3, :] * y
    xo_ref[...] = x
    hn = x * lax.rsqrt(jnp.mean(x * x, axis=-1, keepdims=True) + EPS) * g2_ref[...]
    h = hn * (1.0 + mod_ref[0, 4:5, :]) + mod_ref[0, 3:4, :]
    h_ref[...] = _pack_pairs(h)
    hb = h.astype(BF16)
    w = wr_ref[...]
    w_hi = w.astype(BF16)
    w_lo = (w - w_hi.astype(F32)).astype(BF16)
    h_lo = (h - hb.astype(F32)).astype(BF16)
    logits = _dot(hb, w_hi) + _dot(hb, w_lo) + _dot(h_lo, w_hi) + br_ref[...]
    idx, gate = _route(logits)
    idx_ref[...] = idx
    gate_ref[...] = gate


def _outproj_call(mixc, og, orr, zg, zr, gng, rng, wout, l, x, mods, g2, wr, br, tpb):
    n, d = x.shape
    nt = n // TM
    return pl.pallas_call(
        _outproj_kernel,
        grid=(nt,),
        in_specs=[
            pl.BlockSpec((TM, 2 * GROUP_W), lambda i: (i, 0)),
            pl.BlockSpec((2, TM, GROUP_W), lambda i: (0, i, 0)),
            pl.BlockSpec((2, TM, GROUP_W), lambda i: (0, i, 0)),
            pl.BlockSpec((TM, GROUP_W), lambda i: (i, 2)),
            pl.BlockSpec((TM, GROUP_W), lambda i: (i, 2)),
            pl.BlockSpec((1, GROUP_W), lambda i: (0, 0)),
            pl.BlockSpec((1, GROUP_W), lambda i: (0, 0)),
            pl.BlockSpec((1, d, d), lambda i: (l, 0, 0), pipeline_mode=pl.Buffered(1)),
            pl.BlockSpec((TM, d), lambda i: (i, 0)),
            pl.BlockSpec((1, N_MOD, d), lambda i: (_mod_row(i, tpb), 0, 0)),
            pl.BlockSpec((1, d), lambda i: (0, 0)),
            pl.BlockSpec((d, LANES), lambda i: (0, 0)),
            pl.BlockSpec((1, LANES), lambda i: (0, 0)),
        ],
        out_specs=[
            pl.BlockSpec((TM, d), lambda i: (i, 0)),
            pl.BlockSpec((TM, d // 2), lambda i: (i, 0)),
            pl.BlockSpec((TM, LANES), lambda i: (i, 0)),
            pl.BlockSpec((TM, LANES), lambda i: (i, 0)),
        ],
        out_shape=[
            jax.ShapeDtypeStruct((n, d), F32),
            jax.ShapeDtypeStruct((n, d // 2), jnp.uint32),
            jax.ShapeDtypeStruct((n, LANES), jnp.int32),
            jax.ShapeDtypeStruct((n, LANES), F32),
        ],
        compiler_params=_cparams(("arbitrary",)),
        name="outproj",
    )(mixc, og, orr, zg, zr, gng, rng, wout, x, mods, g2, wr, br)


def _sc_gather(table, idx):
    n_idx = idx.shape[0]
    width = table.shape[1]
    assert n_idx % (SC_WINDOW * SC_WORKERS) == 0
    per = n_idx // SC_WORKERS
    n_steps = per // SC_WINDOW
    assert n_steps % 2 == 0
    per_pad = (per + LANES - 1) // LANES * LANES
    idx_w = jnp.pad(idx.reshape(SC_WORKERS, per), ((0, 0), (0, per_pad - per)))
    mesh = plsc.VectorSubcoreMesh(core_axis_name="core", subcore_axis_name="subcore")
    n_cores = SC_WORKERS // 16

    @functools.partial(
        pl.kernel,
        out_type=jax.ShapeDtypeStruct((n_idx, width), table.dtype),
        mesh=mesh,
        scratch_types=[
            pltpu.VMEM((per_pad,), jnp.int32),
            pltpu.VMEM((2, SC_WINDOW, width), table.dtype),
            pltpu.SemaphoreType.DMA((2,)),
            pltpu.SemaphoreType.DMA((2,)),
        ],
    )
    def gather_kernel(table_hbm, idx_hbm, out_hbm, idx_v, rows_v, sem_g, sem_w):
        wid = lax.axis_index("subcore") * n_cores + lax.axis_index("core")
        base = wid * per
        pltpu.sync_copy(idx_hbm.at[wid], idx_v)

        def gather(step, buf):
            return pltpu.make_async_copy(table_hbm.at[idx_v.at[pl.ds(step * SC_WINDOW, SC_WINDOW)]],
                                         rows_v.at[buf], sem_g.at[buf])

        def write(step, buf):
            return pltpu.make_async_copy(rows_v.at[buf], out_hbm.at[pl.ds(base + step * SC_WINDOW, SC_WINDOW)],
                                         sem_w.at[buf])

        gather(0, 0).start()

        @pl.loop(0, n_steps, step=2)
        def _(s):
            for buf in range(2):
                step = s + buf
                other = 1 - buf
                gather(step, buf).wait()
                write(step, buf).start()

                @pl.when(step >= 1)
                def _():
                    write(step - 1, other).wait()

                @pl.when(step + 1 < n_steps)
                def _():
                    gather(step + 1, other).start()

        write(n_steps - 1, 1).wait()

    return gather_kernel(table, idx_w)


def _expert_up_kernel(blk_e_ref, nvalid_ref, x_ref, w1_ref, w3_ref, h_ref):
    @pl.when(nvalid_ref[pl.program_id(0)] > 0)
    def _compute():
        x = _unpack_pairs(x_ref[...]).astype(BF16)
        h1 = _dot(x, w1_ref[0, 0].astype(BF16))
        h3 = _dot(x, w3_ref[0, 0].astype(BF16))
        h_ref[...] = (_silu(h1) * h3).astype(BF16)


def _expert_down_kernel(blk_e_ref, nvalid_ref, h_ref, w2_ref, y_ref):
    @pl.when(nvalid_ref[pl.program_id(0)] > 0)
    def _compute():
        y_ref[...] = _pack_pairs(_dot(h_ref[...], w2_ref[0, 0].astype(BF16)))


def _experts_call(blk_e, nvalid, xs, w1, w3, w2, l):
    n_slots = xs.shape[0]
    d = w1.shape[-2]
    nb = n_slots // MOE_TB
    hidden = w1.shape[-1]
    up = pl.pallas_call(
        _expert_up_kernel,
        grid_spec=pltpu.PrefetchScalarGridSpec(
            num_scalar_prefetch=2,
            grid=(nb,),
            in_specs=[
                pl.BlockSpec((MOE_TB, d // 2), lambda i, be, nv: (i, 0)),
                pl.BlockSpec((1, 1, d, hidden), lambda i, be, nv: (l, be[i], 0, 0)),
                pl.BlockSpec((1, 1, d, hidden), lambda i, be, nv: (l, be[i], 0, 0)),
            ],
            out_specs=pl.BlockSpec((MOE_TB, hidden), lambda i, be, nv: (i, 0)),
        ),
        out_shape=jax.ShapeDtypeStruct((n_slots, hidden), BF16),
        compiler_params=_cparams(("arbitrary",)),
        name="expert_up",
    )(blk_e, nvalid, xs, w1, w3)
    return pl.pallas_call(
        _expert_down_kernel,
        grid_spec=pltpu.PrefetchScalarGridSpec(
            num_scalar_prefetch=2,
            grid=(nb,),
            in_specs=[
                pl.BlockSpec((MOE_TB, hidden), lambda i, be, nv: (i, 0)),
                pl.BlockSpec((1, 1, hidden, d), lambda i, be, nv: (l, be[i], 0, 0)),
            ],
            out_specs=pl.BlockSpec((MOE_TB, d // 2), lambda i, be, nv: (i, 0)),
        ),
        out_shape=jax.ShapeDtypeStruct((n_slots, d // 2), jnp.uint32),
        compiler_params=_cparams(("arbitrary",)),
        name="expert_down",
    )(blk_e, nvalid, up, w2)


def _slot_plan(idx):
    n = idx.shape[0]
    n_asg = n * TOP_K
    flat_e = idx[:, :TOP_K].reshape(n_asg)
    order = jnp.argsort(flat_e).astype(jnp.int32)
    experts = jnp.arange(N_EXPERTS, dtype=jnp.int32)
    counts = jnp.sum(flat_e[:, None] == experts[None, :], axis=0, dtype=jnp.int32)
    padded = (counts + MOE_TB - 1) // MOE_TB * MOE_TB
    pad_end = jnp.cumsum(padded)
    pad_start = pad_end - padded
    start = jnp.cumsum(counts) - counts
    n_slots = (n_asg + MOE_TB - 1) // MOE_TB * MOE_TB + N_EXPERTS * MOE_TB
    nb = n_slots // MOE_TB
    blk0 = jnp.arange(nb, dtype=jnp.int32) * MOE_TB
    blk_e = jnp.minimum(jnp.sum(blk0[:, None] >= pad_end[None, :], axis=1, dtype=jnp.int32), N_EXPERTS - 1)
    sel = (blk_e[:, None] == experts[None, :]).astype(jnp.int32)
    blk_cnt = jnp.sum(sel * counts[None, :], axis=1)
    blk_pad0 = jnp.sum(sel * pad_start[None, :], axis=1)
    blk_start = jnp.sum(sel * start[None, :], axis=1)
    nvalid = jnp.clip(blk_cnt - (blk0 - blk_pad0), 0, MOE_TB).astype(jnp.int32)
    blk_w = jnp.where(nvalid > 0, blk_e, jnp.max(jnp.where(nvalid > 0, blk_e, 0)))
    within = jnp.arange(MOE_TB, dtype=jnp.int32)[None, :]
    valid = within < nvalid[:, None]
    pos = jnp.clip((blk_start + blk0 - blk_pad0)[:, None] + within, 0, n_asg - 1)
    asg = order[pos.reshape(n_slots)]
    filler = jnp.arange(n_slots, dtype=jnp.int32) % n
    slot_tok = jnp.where(valid.reshape(n_slots), lax.shift_right_logical(asg, 1), filler).astype(jnp.int32)
    rank = jnp.argsort(order).astype(jnp.int32)
    sel_a = (flat_e[:, None] == experts[None, :]).astype(jnp.int32)
    dest = rank + jnp.sum(sel_a * (pad_start - start)[None, :], axis=1)
    dest_ct = dest.reshape(n, TOP_K).T.reshape(n_asg)
    return blk_w, nvalid, slot_tok, dest_ct


def _final_kernel(x_ref, y_ref, gate_ref, mod_ref, g_ref, o_ref):
    x = _combined(x_ref, y_ref, gate_ref, mod_ref)
    o_ref[0] = x * lax.rsqrt(jnp.mean(x * x, axis=-1, keepdims=True) + EPS) * g_ref[...]


def _final_call(x, y, gate, mods, gf, bsz, tpb):
    n, d = x.shape
    lat = tpb - 1
    return pl.pallas_call(
        _final_kernel,
        grid=(bsz, lat),
        in_specs=[
            pl.BlockSpec((TM, d), lambda b, j: (b * tpb + j + 1, 0)),
            pl.BlockSpec((TOP_K, TM, d // 2), lambda b, j: (0, b * tpb + j + 1, 0)),
            pl.BlockSpec((TM, LANES), lambda b, j: (b * tpb + j + 1, 0)),
            pl.BlockSpec((1, N_MOD, d), lambda b, j: (b, 0, 0)),
            pl.BlockSpec((1, d), lambda b, j: (0, 0)),
        ],
        out_specs=pl.BlockSpec((1, TM, d), lambda b, j: (b, j, 0)),
        out_shape=jax.ShapeDtypeStruct((bsz, lat * TM, d), F32),
        compiler_params=_cparams(("arbitrary", "arbitrary")),
        name="final_norm",
    )(x, y, gate, mods, gf)


def _rope_tables(seq):
    n_freq = HEAD_DK // 4
    t = jnp.arange(seq)
    inv = ROPE_BASE ** (-jnp.arange(n_freq, dtype=F32) / n_freq)
    ang = jnp.concatenate([(t // GRID_W).astype(F32)[:, None] * inv, (t % GRID_W).astype(F32)[:, None] * inv], axis=-1)
    cos = jnp.concatenate([jnp.ones((TM, HEAD_DK // 2), F32), jnp.cos(ang)], axis=0)
    sin = jnp.concatenate([jnp.zeros((TM, HEAD_DK // 2), F32), jnp.sin(ang)], axis=0)
    cos_t = jnp.tile(jnp.concatenate([cos, cos], axis=-1), (1, N_HEADS))
    sin_t = jnp.tile(jnp.concatenate([-sin, sin], axis=-1), (1, N_HEADS))
    return cos_t, sin_t


def kernel(x, c, ctx, c_ctx, norm1_g, norm2_g, ada_w, ada_b, w_in, cf_dw, cf_b, cf_ln_g, cf_ln_b, sc_dw, gla_w2,
           gla_b2, gla_ng, ret_logit, ret_ng, w_out, w_grp, b_grp, w_rt, b_rt, e_w1, e_w3, e_w2, final_g):
    bsz, seq, d = x.shape
    depth = w_in.shape[0]
    assert d == D_MODEL and ctx.shape[1] == TM and seq % TM == 0 and bsz == 2 and TOP_K == 2
    assert w_in.shape[-1] == IN_W
    tpb = 1 + seq // TM
    n = bsz * tpb * TM
    nk = N_HEADS * HEAD_DK

    s8 = jnp.concatenate([c, c_ctx[None, :], jnp.zeros((8 - bsz - 1, d), F32)], axis=0)
    mods_all = _ada_call(s8, ada_w, ada_b)[:, :bsz + 1, :].reshape(depth, bsz + 1, N_MOD, d)

    w_in_p = _pack_w_in(w_in)
    w_out_b = _cast_w_out(w_out)
    cos_t, sin_t = _rope_tables(seq)
    w2pad = jnp.zeros((depth, 2, LANES, nk), F32)
    w2pad = w2pad.at[:, 0, 0:GLA_RANK, :].set(gla_w2[:, 0]).at[:, 1, GLA_RANK:2 * GLA_RANK, :].set(gla_w2[:, 1])
    wr_all = jnp.concatenate([w_rt, w_grp, jnp.zeros((depth, d, LANES - N_EXPERTS - N_GROUPS), F32)], axis=-1)
    br_all = jnp.concatenate([b_rt, b_grp, jnp.zeros((depth, LANES - N_EXPERTS - N_GROUPS), F32)], axis=-1)

    out = None
    stream = (ctx.reshape(bsz * TM, d), x.reshape(bsz * seq, d))
    for l in range(depth):
        mods = mods_all[l]
        zc, zg, zr, xa = _inproj_call(stream, mods, norm1_g[l][None, :], w_in_p, l, tpb, n)
        mixc = _conv_call(zc, cf_dw[l], cf_b[l][None, :], cf_ln_g[l][None, :], cf_ln_b[l][None, :], sc_dw[l], tpb)
        og = _gla_call(zg, w2pad[l], gla_b2[l][:, None, :], bsz, tpb)
        lg = jnp.broadcast_to(ret_logit[l][:, :, None], (2, N_HEADS, LANES))
        orr = _ret_call(zr, cos_t, sin_t, lg, bsz, tpb)
        xa, h2, idx, gate = _outproj_call(mixc, og, orr, zg, zr, gla_ng[l][None, :], ret_ng[l][None, :], w_out_b, l,
                                          xa, mods, norm2_g[l][None, :], wr_all[l], br_all[l][None, :], tpb)
        blk_e, nvalid, slot_tok, dest_ct = _slot_plan(idx)
        xs = _sc_gather(h2, slot_tok)
        ys = _experts_call(blk_e, nvalid, xs, e_w1, e_w3, e_w2, l)
        y = _sc_gather(ys, dest_ct).reshape(TOP_K, n, d // 2)
        if l == depth - 1:
            out = _final_call(xa, y, gate, mods, final_g[None, :], bsz, tpb)
        else:
            stream = (xa, y, gate, mods)
    return out
```

```python
import functools

import jax
import jax.numpy as jnp
from jax import lax
from jax.experimental import pallas as pl
from jax.experimental.pallas import tpu as pltpu
from jax.experimental.pallas import tpu_sc as plsc

F32 = jnp.float32
BF16 = jnp.bfloat16

D_MODEL = 2048
GRID_W = 64
GROUP_W = D_MODEL // 4
CF_KERNEL = 31
SC_KERNEL = 3
N_HEADS = 4
HEAD_DK = 64
HEAD_DV = 128
GLA_RANK = 16
GLA_TAU = 16.0
GLA_CHUNK = 128
ROPE_BASE = 10000.0
N_GROUPS = 4
EXPERTS_PER_GROUP = 4
N_EXPERTS = N_GROUPS * EXPERTS_PER_GROUP
TOP_K = 2
EXPERT_HIDDEN = D_MODEL // 2
N_MOD = 6
EPS = 1e-6

TM = 256
LANES = 128
SUBLANES = 8
ADA_TN = 1024
MOE_TB = 512
VMEM_LIMIT = 56 * 1024 * 1024
SC_WORKERS = 32
SC_WINDOW = 16

ZC_W = 5 * GROUP_W
ZG_W = 3 * GROUP_W + LANES
ZR_W = 3 * GROUP_W
Z_W = ZC_W + ZG_W + ZR_W
IN_W = Z_W - (LANES - 2 * GLA_RANK)
GLR_END = ZC_W + 3 * GROUP_W + 2 * GLA_RANK
PACK_MOVES = (
    (3 * GROUP_W, 0, 2 * GROUP_W),
    (2 * GROUP_W, 2 * GROUP_W, GROUP_W),
    (0, 3 * GROUP_W, 2 * GROUP_W),
    (ZC_W, ZC_W, 3 * GROUP_W),
    (GLR_END, ZC_W + ZG_W, ZR_W),
)


def _cparams(sem):
    return pltpu.CompilerParams(dimension_semantics=sem, vmem_limit_bytes=VMEM_LIMIT)


def _sigmoid(x):
    return 1.0 / (1.0 + jnp.exp(-x))


def _silu(x):
    return x * _sigmoid(x)


def _log_sigmoid(x):
    return jnp.minimum(x, 0.0) - jnp.log1p(jnp.exp(-jnp.abs(x)))


def _dot(a, b):
    return jnp.dot(a, b, preferred_element_type=F32)


def _dot_nt(a, b):
    return lax.dot_general(a, b, (((1,), (1,)), ((), ())), preferred_element_type=F32)


def _dot_tn(a, b):
    return lax.dot_general(a, b, (((0,), (0,)), ((), ())), preferred_element_type=F32)


def _pack_pairs(x):
    w = x.shape[1] // 2
    xb = x.astype(BF16).astype(F32)
    hi = pltpu.bitcast(xb[:, :w], jnp.uint32)
    lo = pltpu.bitcast(xb[:, w:], jnp.uint32)
    return hi | lax.shift_right_logical(lo, jnp.uint32(16))


def _unpack_pairs(p):
    hi = pltpu.bitcast(p & jnp.uint32(0xFFFF0000), F32)
    lo = pltpu.bitcast(lax.shift_left(p, jnp.uint32(16)), F32)
    return jnp.concatenate([hi, lo], axis=1)


def _mod_row(i, tpb):
    return jnp.where(i % tpb == 0, 2, i // tpb)


def _pack_kernel(wt_ref, o_ref):
    def put(src, dst, width):
        for c in range(0, width, TM):
            step = min(TM, width - c)
            o_ref[0, :, dst + c:dst + c + step] = wt_ref[0, src + c:src + c + step, :].T.astype(BF16)

    for src, dst, width in PACK_MOVES:
        put(src, dst, width)
    glr0 = GLR_END - 2 * GLA_RANK
    tile = wt_ref[0, glr0:glr0 + LANES, :].T
    lane = lax.broadcasted_iota(jnp.int32, tile.shape, 1)
    o_ref[0, :, glr0:glr0 + LANES] = jnp.where(lane < 2 * GLA_RANK, tile, 0.0).astype(BF16)


def _pack_w_in(w_in):
    depth, d, _ = w_in.shape
    return pl.pallas_call(
        _pack_kernel,
        grid=(depth, d // TM),
        in_specs=[pl.BlockSpec((1, IN_W, TM), lambda l, i: (l, 0, i))],
        out_specs=pl.BlockSpec((1, TM, Z_W), lambda l, i: (l, i, 0)),
        out_shape=jax.ShapeDtypeStruct((depth, d, Z_W), BF16),
        compiler_params=_cparams(("arbitrary", "arbitrary")),
        name="pack_w_in",
    )(jnp.swapaxes(w_in, 1, 2))


def _cast_kernel(w_ref, o_ref):
    o_ref[...] = w_ref[...].astype(BF16)


def _cast_w_out(w_out):
    depth, k, d = w_out.shape
    return pl.pallas_call(
        _cast_kernel,
        grid=(depth, k // TM),
        in_specs=[pl.BlockSpec((1, TM, d), lambda l, i: (l, i, 0))],
        out_specs=pl.BlockSpec((1, TM, d), lambda l, i: (l, i, 0)),
        out_shape=jax.ShapeDtypeStruct((depth, k, d), BF16),
        compiler_params=_cparams(("arbitrary", "arbitrary")),
        name="cast_w_out",
    )(w_out)


def _ada_kernel(s_ref, w_ref, b_ref, o_ref):
    a = _silu(s_ref[...]).astype(BF16)
    o_ref[0] = _dot(a, w_ref[0].astype(BF16)) + b_ref[0]


def _ada_call(s8, ada_w, ada_b):
    depth, d, nm = ada_w.shape
    return pl.pallas_call(
        _ada_kernel,
        grid=(depth, nm // ADA_TN),
        in_specs=[
            pl.BlockSpec((8, d), lambda l, j: (0, 0)),
            pl.BlockSpec((1, d, ADA_TN), lambda l, j: (l, 0, j)),
            pl.BlockSpec((1, 1, ADA_TN), lambda l, j: (l, 0, j)),
        ],
        out_specs=pl.BlockSpec((1, 8, ADA_TN), lambda l, j: (l, 0, j)),
        out_shape=jax.ShapeDtypeStruct((depth, 8, nm), F32),
        compiler_params=_cparams(("arbitrary", "arbitrary")),
        name="adaln",
    )(s8, ada_w, ada_b.reshape(depth, 1, nm))


def _combined(x_ref, y_ref, gate_ref, mod_ref):
    f = gate_ref[:, 0:1] * _unpack_pairs(y_ref[0]) + gate_ref[:, 1:2] * _unpack_pairs(y_ref[1])
    return x_ref[...] + mod_ref[0, 5:6, :] * f


def _inproj_kernel(*refs, first, tpb):
    if first:
        ctx_ref, lat_ref, mod_ref, g_ref, w_ref, zc_ref, zg_ref, zr_ref, xo_ref = refs
        x = jnp.where(pl.program_id(0) % tpb == 0, ctx_ref[...], lat_ref[...])
    else:
        x_ref, y_ref, gate_ref, pmod_ref, mod_ref, g_ref, w_ref, zc_ref, zg_ref, zr_ref, xo_ref = refs
        x = _combined(x_ref, y_ref, gate_ref, pmod_ref)
    xo_ref[...] = x
    y = x * lax.rsqrt(jnp.mean(x * x, axis=-1, keepdims=True) + EPS) * g_ref[...]
    h = (y * (1.0 + mod_ref[0, 1:2, :]) + mod_ref[0, 0:1, :]).astype(BF16)
    off = 0
    for ref, width in ((zc_ref, ZC_W), (zg_ref, ZG_W), (zr_ref, ZR_W)):
        c = 0
        while c < width:
            step = min(512, width - c)
            ref[:, c:c + step] = _dot(h, w_ref[0, :, off + c:off + c + step]).astype(BF16)
            c += step
        off += width


def _inproj_call(stream, mods, g1, w_packed, l, tpb, n):
    d = D_MODEL
    nt = n // TM
    nlat = tpb - 1
    mod_spec = pl.BlockSpec((1, N_MOD, d), lambda i: (_mod_row(i, tpb), 0, 0))
    first = len(stream) == 2
    if first:
        in_specs = [
            pl.BlockSpec((TM, d), lambda i: (i // tpb, 0)),
            pl.BlockSpec((TM, d), lambda i: ((i // tpb) * nlat + jnp.maximum(i % tpb - 1, 0), 0)),
        ]
    else:
        in_specs = [
            pl.BlockSpec((TM, d), lambda i: (i, 0)),
            pl.BlockSpec((TOP_K, TM, d // 2), lambda i: (0, i, 0)),
            pl.BlockSpec((TM, LANES), lambda i: (i, 0)),
            mod_spec,
        ]
    in_specs += [
        mod_spec,
        pl.BlockSpec((1, d), lambda i: (0, 0)),
        pl.BlockSpec((1, d, Z_W), lambda i: (l, 0, 0), pipeline_mode=pl.Buffered(1)),
    ]
    return pl.pallas_call(
        functools.partial(_inproj_kernel, first=first, tpb=tpb),
        grid=(nt,),
        in_specs=in_specs,
        out_specs=[
            pl.BlockSpec((TM, ZC_W), lambda i: (i, 0)),
            pl.BlockSpec((TM, ZG_W), lambda i: (i, 0)),
            pl.BlockSpec((TM, ZR_W), lambda i: (i, 0)),
            pl.BlockSpec((TM, d), lambda i: (i, 0)),
        ],
        out_shape=[
            jax.ShapeDtypeStruct((n, ZC_W), BF16),
            jax.ShapeDtypeStruct((n, ZG_W), BF16),
            jax.ShapeDtypeStruct((n, ZR_W), BF16),
            jax.ShapeDtypeStruct((n, d), F32),
        ],
        compiler_params=_cparams(("arbitrary",)),
        name="inproj",
    )(*stream, mods, g1, w_packed)


PAD_LEAD = 16
SEG = GRID_W
LAT_STRIDE = SEG + PAD_LEAD
PAD_ROWS = (TM // SEG) * LAT_STRIDE + PAD_LEAD


def _conv_kernel(zc_ref, prev_ref, next_ref, cfw_ref, cfb_ref, lng_ref, lnb_ref, scw_ref, o_ref, pad_ref, shift_ref,
                 *, tpb):
    j = pl.program_id(0) % tpb
    nseg = TM // SEG
    half = CF_KERNEL // 2
    zeros_lead = jnp.zeros((PAD_LEAD, GROUP_W), F32)

    def glu():
        cfa = zc_ref[:, 3 * GROUP_W:4 * GROUP_W].astype(F32)
        cfg = zc_ref[:, 4 * GROUP_W:5 * GROUP_W].astype(F32)
        return cfa * _sigmoid(cfg)

    def finish_cf(acc, s):
        y = acc + cfb_ref[...]
        yc = y - jnp.mean(y, axis=-1, keepdims=True)
        yn = yc * lax.rsqrt(jnp.mean(yc * yc, axis=-1, keepdims=True) + EPS)
        o_ref[s * SEG:(s + 1) * SEG, 0:GROUP_W] = _silu(yn * lng_ref[...] + lnb_ref[...]).astype(BF16)

    def conformer(stride):
        span = PAD_ROWS - SUBLANES
        for r in range(1, SUBLANES):
            shift_ref[r, 0:span, :] = pad_ref[r:r + span, :]
        for s in range(nseg):
            base = s * stride + PAD_LEAD - half
            acc = jnp.zeros((SEG, GROUP_W), F32)
            for k in range(CF_KERNEL):
                r = (base + k) % SUBLANES
                a = base + k - r
                win = pad_ref[a:a + SEG, :] if r == 0 else shift_ref[r, a:a + SEG, :]
                acc = acc + cfw_ref[k:k + 1, :] * win
            finish_cf(acc, s)

    def sc_products():
        scc = zc_ref[:, 0:GROUP_W].astype(F32)
        scv = zc_ref[:, GROUP_W:2 * GROUP_W].astype(F32)
        scb = zc_ref[:, 2 * GROUP_W:3 * GROUP_W].astype(F32)
        return scc * scv, scb

    @pl.when(j != 0)
    def _latent():
        u = glu()
        for s in range(nseg):
            pad_ref[s * LAT_STRIDE:s * LAT_STRIDE + PAD_LEAD, :] = zeros_lead
            pad_ref[s * LAT_STRIDE + PAD_LEAD:(s + 1) * LAT_STRIDE, :] = u[s * SEG:(s + 1) * SEG]
        pad_ref[nseg * LAT_STRIDE:nseg * LAT_STRIDE + PAD_LEAD, :] = zeros_lead
        conformer(LAT_STRIDE)
        usc, scb = sc_products()
        up = prev_ref[:, 0:GROUP_W].astype(F32) * prev_ref[:, GROUP_W:2 * GROUP_W].astype(F32)
        un = next_ref[:, 0:GROUP_W].astype(F32) * next_ref[:, GROUP_W:2 * GROUP_W].astype(F32)
        up = jnp.where(j == 1, 0.0, up)
        un = jnp.where(j == tpb - 1, 0.0, un)
        above = jnp.concatenate([up, usc[:TM - GRID_W]], axis=0)
        below = jnp.concatenate([usc[GRID_W:], un], axis=0)
        y = scb * (scw_ref[0:1, :] * above + scw_ref[1:2, :] * usc + scw_ref[2:3, :] * below)
        o_ref[:, GROUP_W:2 * GROUP_W] = y.astype(BF16)

    @pl.when(j == 0)
    def _context():
        u = glu()
        pad_ref[0:PAD_LEAD, :] = zeros_lead
        pad_ref[PAD_LEAD:PAD_LEAD + TM, :] = u
        pad_ref[PAD_LEAD + TM:PAD_ROWS, :] = jnp.zeros((PAD_ROWS - PAD_LEAD - TM, GROUP_W), F32)
        conformer(SEG)
        usc, scb = sc_products()
        pad_ref[PAD_LEAD:PAD_LEAD + TM, :] = usc
        before = pad_ref[PAD_LEAD - 1:PAD_LEAD - 1 + TM, :]
        after = pad_ref[PAD_LEAD + 1:PAD_LEAD + 1 + TM, :]
        y = scb * (scw_ref[0:1, :] * before + scw_ref[1:2, :] * usc + scw_ref[2:3, :] * after)
        o_ref[:, GROUP_W:2 * GROUP_W] = y.astype(BF16)


def _conv_call(zc, cfw, cfb, lng, lnb, scw, tpb):
    n = zc.shape[0]
    nt = n // TM
    r = TM // GRID_W
    nhalo = n // GRID_W
    return pl.pallas_call(
        functools.partial(_conv_kernel, tpb=tpb),
        grid=(nt,),
        in_specs=[
            pl.BlockSpec((TM, ZC_W), lambda i: (i, 0)),
            pl.BlockSpec((GRID_W, 2 * GROUP_W), lambda i: (jnp.maximum(i * r - 1, 0), 0)),
            pl.BlockSpec((GRID_W, 2 * GROUP_W), lambda i: (jnp.minimum(i * r + r, nhalo - 1), 0)),
            pl.BlockSpec((CF_KERNEL, GROUP_W), lambda i: (0, 0)),
            pl.BlockSpec((1, GROUP_W), lambda i: (0, 0)),
            pl.BlockSpec((1, GROUP_W), lambda i: (0, 0)),
            pl.BlockSpec((1, GROUP_W), lambda i: (0, 0)),
            pl.BlockSpec((SC_KERNEL, GROUP_W), lambda i: (0, 0)),
        ],
        out_specs=pl.BlockSpec((TM, 2 * GROUP_W), lambda i: (i, 0)),
        out_shape=jax.ShapeDtypeStruct((n, 2 * GROUP_W), BF16),
        scratch_shapes=[pltpu.VMEM((PAD_ROWS, GROUP_W), F32), pltpu.VMEM((SUBLANES, PAD_ROWS, GROUP_W), F32)],
        compiler_params=_cparams(("arbitrary",)),
        name="convmix",
    )(zc, zc, zc, cfw, cfb, lng, lnb, scw)


def _scan_block(d, j, tpb):
    return jnp.where(d == 0, j, jnp.where(j == 0, 0, tpb - j))


def _tri(rev, size):
    ri = lax.broadcasted_iota(jnp.int32, (size, size), 0)
    ci = lax.broadcasted_iota(jnp.int32, (size, size), 1)
    return (ri <= ci) if rev else (ri >= ci)


def _gla_kernel(zg_ref, w2_ref, b2_ref, o_ref, st_ref):
    d = pl.program_id(0)

    @pl.when(pl.program_id(1) == 0)
    def _init():
        st_ref[...] = jnp.zeros_like(st_ref)

    def run(rev):
        for b in range(zg_ref.shape[0]):
            run_batch(rev, zg_ref.at[b], o_ref.at[0, b], st_ref.at[b])

    def run_batch(rev, zg_ref, o_ref, st_ref):
        nk = N_HEADS * HEAD_DK
        zz = _dot(zg_ref[:, 3 * GROUP_W:3 * GROUP_W + LANES], w2_ref[0].astype(BF16)) + b2_ref[0]
        la = _log_sigmoid(zz) * (1.0 / GLA_TAU)
        tri = _tri(rev, GLA_CHUNK)
        trib = tri.astype(BF16)
        order = range(TM // GLA_CHUNK - 1, -1, -1) if rev else range(TM // GLA_CHUNK)
        states = [st_ref[h] for h in range(N_HEADS)]
        for c in order:
            rows = slice(c * GLA_CHUNK, (c + 1) * GLA_CHUNK)
            la_c = la[rows]
            hi = la_c.astype(BF16)
            lo = (la_c - hi.astype(F32)).astype(BF16)
            bc = _dot(trib, hi) + _dot(trib, lo)
            b_last = bc[0:1] if rev else bc[GLA_CHUNK - 1:GLA_CHUNK]
            b_mid = bc[GLA_CHUNK // 2:GLA_CHUNK // 2 + 1]
            q = zg_ref[rows, 0:nk].astype(F32) * (HEAD_DK ** -0.5)
            k = zg_ref[rows, nk:2 * nk].astype(F32)
            q_in = (q * jnp.exp(bc)).astype(BF16)
            q_mid = (q * jnp.exp(bc - b_mid)).astype(BF16)
            k_mid = (k * jnp.exp(b_mid - bc)).astype(BF16)
            k_st = (k * jnp.exp(b_last - bc)).astype(BF16)
            a_row = jnp.exp(b_last)
            for h in range(N_HEADS):
                hs = slice(h * HEAD_DK, (h + 1) * HEAD_DK)
                vs = slice(h * HEAD_DV, (h + 1) * HEAD_DV)
                vh = zg_ref[rows, GROUP_W + h * HEAD_DV:GROUP_W + (h + 1) * HEAD_DV]
                sc = jnp.where(tri, _dot_nt(q_mid[:, hs], k_mid[:, hs]), 0.0).astype(BF16)
                st = states[h]
                o_ref[rows, vs] = _dot(sc, vh) + _dot_nt(q_in[:, hs], st.astype(BF16))
                states[h] = st * a_row[:, hs] + _dot_tn(vh, k_st[:, hs])
        for h in range(N_HEADS):
            st_ref[h] = states[h]

    @pl.when(d == 0)
    def _fwd():
        run(False)

    @pl.when(d == 1)
    def _bwd():
        run(True)


def _gla_call(zg, w2pad, b2, bsz, tpb):
    n = zg.shape[0]
    nk = N_HEADS * HEAD_DK
    return pl.pallas_call(
        _gla_kernel,
        grid=(2, tpb),
        in_specs=[
            pl.BlockSpec((bsz, TM, ZG_W), lambda d, j: (0, _scan_block(d, j, tpb), 0)),
            pl.BlockSpec((1, LANES, nk), lambda d, j: (d, 0, 0)),
            pl.BlockSpec((1, 1, nk), lambda d, j: (d, 0, 0)),
        ],
        out_specs=pl.BlockSpec((1, bsz, TM, GROUP_W), lambda d, j: (d, 0, _scan_block(d, j, tpb), 0)),
        out_shape=jax.ShapeDtypeStruct((2, bsz, n // bsz, GROUP_W), F32),
        scratch_shapes=[pltpu.VMEM((bsz, N_HEADS, HEAD_DV, HEAD_DK), F32)],
        compiler_params=_cparams(("arbitrary", "arbitrary")),
        name="gla",
    )(zg.reshape(bsz, n // bsz, ZG_W), w2pad, b2).reshape(2, n, GROUP_W)


def _ret_kernel(zr_ref, cos_ref, sin_ref, lg_ref, o_ref, st_ref):
    d = pl.program_id(0)

    @pl.when(pl.program_id(1) == 0)
    def _init():
        st_ref[...] = jnp.zeros_like(st_ref)

    def run(rev):
        for b in range(zr_ref.shape[0]):
            run_batch(rev, zr_ref.at[b], o_ref.at[0, b], st_ref.at[b])

    def run_batch(rev, zr_ref, o_ref, st_ref):
        nk = N_HEADS * HEAD_DK
        lane = lax.broadcasted_iota(jnp.int32, (TM, nk), 1)
        first_half = (lane % HEAD_DK) < (HEAD_DK // 2)
        cosf = cos_ref[...]
        sinf = sin_ref[...]

        def rope(x):
            swapped = jnp.where(first_half, pltpu.roll(x, nk - HEAD_DK // 2, 1), pltpu.roll(x, HEAD_DK // 2, 1))
            return x * cosf + swapped * sinf

        q = rope(zr_ref[:, 0:nk].astype(F32)).astype(BF16)
        k = rope(zr_ref[:, nk:2 * nk].astype(F32)) * (HEAD_DK ** -0.5)
        lg = _log_sigmoid(lg_ref[0])
        tri = _tri(rev, TM)
        ri = lax.broadcasted_iota(jnp.int32, (TM, TM), 0)
        ci = lax.broadcasted_iota(jnp.int32, (TM, TM), 1)
        rel = ((ci - ri) if rev else (ri - ci)).astype(F32)
        pos = lax.broadcasted_iota(jnp.int32, (TM, LANES), 0).astype(F32)
        q_steps = (TM - pos) if rev else (pos + 1.0)
        k_steps = pos if rev else (TM - 1.0 - pos)
        lg_wide = jnp.concatenate([lg, lg], axis=1)
        for h in range(N_HEADS):
            hs = slice(h * HEAD_DK, (h + 1) * HEAD_DK)
            vs = slice(h * HEAD_DV, (h + 1) * HEAD_DV)
            lgh = lg[h:h + 1, :]
            decay = jnp.where(tri, jnp.exp(rel * lg_wide[h:h + 1, :]), 0.0)
            q_decay = jnp.exp(q_steps * lgh)
            k_decay = jnp.exp(k_steps[:, 0:HEAD_DK] * lgh[:, 0:HEAD_DK])
            chunk_decay = jnp.exp(float(TM) * lgh[:, 0:HEAD_DK])
            st = st_ref[h]
            qh = q[:, hs]
            kh = k[:, hs]
            vh = zr_ref[:, GROUP_W + h * HEAD_DV:GROUP_W + (h + 1) * HEAD_DV]
            sc = (_dot_nt(qh, kh.astype(BF16)) * decay).astype(BF16)
            o_ref[:, vs] = _dot(sc, vh) + _dot_nt(qh, st.astype(BF16)) * q_decay
            st_ref[h] = st * chunk_decay + _dot_tn(vh, (kh * k_decay).astype(BF16))

    @pl.when(d == 0)
    def _fwd():
        run(False)

    @pl.when(d == 1)
    def _bwd():
        run(True)


def _ret_call(zr, cos_t, sin_t, lg, bsz, tpb):
    n = zr.shape[0]
    nk = N_HEADS * HEAD_DK
    return pl.pallas_call(
        _ret_kernel,
        grid=(2, tpb),
        in_specs=[
            pl.BlockSpec((bsz, TM, ZR_W), lambda d, j: (0, _scan_block(d, j, tpb), 0)),
            pl.BlockSpec((TM, nk), lambda d, j: (_scan_block(d, j, tpb), 0)),
            pl.BlockSpec((TM, nk), lambda d, j: (_scan_block(d, j, tpb), 0)),
            pl.BlockSpec((1, N_HEADS, LANES), lambda d, j: (d, 0, 0)),
        ],
        out_specs=pl.BlockSpec((1, bsz, TM, GROUP_W), lambda d, j: (d, 0, _scan_block(d, j, tpb), 0)),
        out_shape=jax.ShapeDtypeStruct((2, bsz, n // bsz, GROUP_W), F32),
        scratch_shapes=[pltpu.VMEM((bsz, N_HEADS, HEAD_DV, HEAD_DK), F32)],
        compiler_params=_cparams(("arbitrary", "arbitrary")),
        name="retention",
    )(zr.reshape(bsz, n // bsz, ZR_W), cos_t, sin_t, lg).reshape(2, n, GROUP_W)


def _head_norm(o, g, center):
    outs = []
    for h in range(N_HEADS):
        oh = o[:, h * HEAD_DV:(h + 1) * HEAD_DV]
        if center:
            oh = oh - jnp.mean(oh, axis=-1, keepdims=True)
        outs.append(oh * lax.rsqrt(jnp.mean(oh * oh, axis=-1, keepdims=True) + EPS))
    return jnp.concatenate(outs, axis=1) * g


def _route(logits):
    lane = lax.broadcasted_iota(jnp.int32, logits.shape, 1).astype(F32)
    neg = jnp.float32(-jnp.inf)
    big = jnp.float32(LANES)
    gmask = (lane >= N_EXPERTS) & (lane < N_EXPERTS + N_GROUPS)
    gl = jnp.where(gmask, logits, neg)
    gmax = jnp.max(gl, axis=-1, keepdims=True)
    gidx = jnp.min(jnp.where(gl == gmax, lane - N_EXPERTS, big), axis=-1, keepdims=True)
    g_w = 1.0 / jnp.sum(jnp.where(gmask, jnp.exp(logits - gmax), 0.0), axis=-1, keepdims=True)
    emask = (lane >= gidx * EXPERTS_PER_GROUP) & (lane < (gidx + 1.0) * EXPERTS_PER_GROUP)
    el = jnp.where(emask, logits, neg)
    e1 = jnp.max(el, axis=-1, keepdims=True)
    i1 = jnp.min(jnp.where(el == e1, lane, big), axis=-1, keepdims=True)
    el2 = jnp.where(lane == i1, neg, el)
    e2 = jnp.max(el2, axis=-1, keepdims=True)
    i2 = jnp.min(jnp.where(el2 == e2, lane, big), axis=-1, keepdims=True)
    r = jnp.exp(e2 - e1)
    w1 = g_w / (1.0 + r)
    w2 = g_w * r / (1.0 + r)
    idx = jnp.where(lane == 0.0, i1, jnp.where(lane == 1.0, i2, 0.0)).astype(jnp.int32)
    gate = jnp.where(lane == 0.0, w1, jnp.where(lane == 1.0, w2, 0.0))
    return idx, gate


def _outproj_kernel(mixc_ref, og_ref, or_ref, gr_ref, rg_ref, gng_ref, rng_ref, wout_ref, x_ref, mod_ref, g2_ref,
                    wr_ref, br_ref, xo_ref, h_ref, idx_ref, gate_ref):
    j = pl.program_id(0)
    w = wr_ref[...]
    w_hi = w.astype(BF16)
    w_lo = (w - w_hi.astype(F32)).astype(BF16)
    for b in range(x_ref.shape[0]):
        mod = mod_ref[jnp.where(j == 0, mod_ref.shape[0] - 1, b)]
        gla = _head_norm(og_ref[0, b] + og_ref[1, b], gng_ref[...], False) * _silu(gr_ref[b].astype(F32))
        ret = _head_norm(or_ref[0, b] + or_ref[1, b], rng_ref[...], True) * _silu(rg_ref[b].astype(F32))
        y = (_dot(mixc_ref[b], wout_ref[0, 0:2 * GROUP_W, :])
             + _dot(gla.astype(BF16), wout_ref[0, 2 * GROUP_W:3 * GROUP_W, :])
             + _dot(ret.astype(BF16), wout_ref[0, 3 * GROUP_W:4 * GROUP_W, :]))
        x = x_ref[b] + mod[2:3, :] * y
        xo_ref[b] = x
        hn = x * lax.rsqrt(jnp.mean(x * x, axis=-1, keepdims=True) + EPS) * g2_ref[...]
        h = hn * (1.0 + mod[4:5, :]) + mod[3:4, :]
        h_ref[b] = _pack_pairs(h)
        hb = h.astype(BF16)
        h_lo = (h - hb.astype(F32)).astype(BF16)
        logits = _dot(hb, w_hi) + _dot(hb, w_lo) + _dot(h_lo, w_hi) + br_ref[...]
        idx, gate = _route(logits)
        idx_ref[b] = idx
        gate_ref[b] = gate


def _outproj_call(mixc, og, orr, zg, zr, gng, rng, wout, l, x, mods, g2, wr, br, bsz, tpb):
    n, d = x.shape
    t = n // bsz

    def rows(width, col=0):
        return pl.BlockSpec((bsz, TM, width), lambda j: (0, j, col))

    scan = pl.BlockSpec((2, bsz, TM, GROUP_W), lambda j: (0, 0, j, 0))
    outs = pl.pallas_call(
        _outproj_kernel,
        grid=(tpb,),
        in_specs=[
            rows(2 * GROUP_W),
            scan,
            scan,
            rows(GROUP_W, 2),
            rows(GROUP_W, 2),
            pl.BlockSpec((1, GROUP_W), lambda j: (0, 0)),
            pl.BlockSpec((1, GROUP_W), lambda j: (0, 0)),
            pl.BlockSpec((1, d, d), lambda j: (l, 0, 0), pipeline_mode=pl.Buffered(1)),
            rows(d),
            pl.BlockSpec((bsz + 1, N_MOD, d), lambda j: (0, 0, 0)),
            pl.BlockSpec((1, d), lambda j: (0, 0)),
            pl.BlockSpec((d, LANES), lambda j: (0, 0)),
            pl.BlockSpec((1, LANES), lambda j: (0, 0)),
        ],
        out_specs=[rows(d), rows(d // 2), rows(LANES), rows(LANES)],
        out_shape=[
            jax.ShapeDtypeStruct((bsz, t, d), F32),
            jax.ShapeDtypeStruct((bsz, t, d // 2), jnp.uint32),
            jax.ShapeDtypeStruct((bsz, t, LANES), jnp.int32),
            jax.ShapeDtypeStruct((bsz, t, LANES), F32),
        ],
        compiler_params=_cparams(("arbitrary",)),
        name="outproj",
    )(mixc.reshape(bsz, t, -1), og.reshape(2, bsz, t, -1), orr.reshape(2, bsz, t, -1), zg.reshape(bsz, t, -1),
      zr.reshape(bsz, t, -1), gng, rng, wout, x.reshape(bsz, t, d), mods, g2, wr, br)
    return [o.reshape(n, -1) for o in outs]


def _sc_gather(table, idx):
    n_idx = idx.shape[0]
    width = table.shape[1]
    assert n_idx % (SC_WINDOW * SC_WORKERS) == 0
    per = n_idx // SC_WORKERS
    n_steps = per // SC_WINDOW
    assert n_steps % 2 == 0
    per_pad = (per + LANES - 1) // LANES * LANES
    idx_w = jnp.pad(idx.reshape(SC_WORKERS, per), ((0, 0), (0, per_pad - per)))
    mesh = plsc.VectorSubcoreMesh(core_axis_name="core", subcore_axis_name="subcore")
    n_cores = SC_WORKERS // 16

    @functools.partial(
        pl.kernel,
        out_type=jax.ShapeDtypeStruct((n_idx, width), table.dtype),
        mesh=mesh,
        scratch_types=[
            pltpu.VMEM((per_pad,), jnp.int32),
            pltpu.VMEM((2, SC_WINDOW, width), table.dtype),
            pltpu.SemaphoreType.DMA((2,)),
            pltpu.SemaphoreType.DMA((2,)),
        ],
    )
    def gather_kernel(table_hbm, idx_hbm, out_hbm, idx_v, rows_v, sem_g, sem_w):
        wid = lax.axis_index("subcore") * n_cores + lax.axis_index("core")
        base = wid * per
        pltpu.sync_copy(idx_hbm.at[wid], idx_v)

        def gather(step, buf):
            return pltpu.make_async_copy(table_hbm.at[idx_v.at[pl.ds(step * SC_WINDOW, SC_WINDOW)]],
                                         rows_v.at[buf], sem_g.at[buf])

        def write(step, buf):
            return pltpu.make_async_copy(rows_v.at[buf], out_hbm.at[pl.ds(base + step * SC_WINDOW, SC_WINDOW)],
                                         sem_w.at[buf])

        gather(0, 0).start()

        @pl.loop(0, n_steps, step=2)
        def _(s):
            for buf in range(2):
                step = s + buf
                other = 1 - buf
                gather(step, buf).wait()
                write(step, buf).start()

                @pl.when(step >= 1)
                def _():
                    write(step - 1, other).wait()

                @pl.when(step + 1 < n_steps)
                def _():
                    gather(step + 1, other).start()

        write(n_steps - 1, 1).wait()

    return gather_kernel(table, idx_w)


def _expert_up_kernel(blk_e_ref, nvalid_ref, x_ref, w1_ref, w3_ref, h_ref):
    @pl.when(nvalid_ref[pl.program_id(0)] > 0)
    def _compute():
        x = _unpack_pairs(x_ref[...]).astype(BF16)
        h1 = _dot(x, w1_ref[0, 0].astype(BF16))
        h3 = _dot(x, w3_ref[0, 0].astype(BF16))
        h_ref[...] = (_silu(h1) * h3).astype(BF16)


def _expert_down_kernel(blk_e_ref, nvalid_ref, h_ref, w2_ref, y_ref):
    @pl.when(nvalid_ref[pl.program_id(0)] > 0)
    def _compute():
        y_ref[...] = _pack_pairs(_dot(h_ref[...], w2_ref[0, 0].astype(BF16)))


def _experts_call(blk_e, nvalid, xs, w1, w3, w2, l):
    n_slots = xs.shape[0]
    d = w1.shape[-2]
    nb = n_slots // MOE_TB
    hidden = w1.shape[-1]
    up = pl.pallas_call(
        _expert_up_kernel,
        grid_spec=pltpu.PrefetchScalarGridSpec(
            num_scalar_prefetch=2,
            grid=(nb,),
            in_specs=[
                pl.BlockSpec((MOE_TB, d // 2), lambda i, be, nv: (i, 0)),
                pl.BlockSpec((1, 1, d, hidden), lambda i, be, nv: (l, be[i], 0, 0)),
                pl.BlockSpec((1, 1, d, hidden), lambda i, be, nv: (l, be[i], 0, 0)),
            ],
            out_specs=pl.BlockSpec((MOE_TB, hidden), lambda i, be, nv: (i, 0)),
        ),
        out_shape=jax.ShapeDtypeStruct((n_slots, hidden), BF16),
        compiler_params=_cparams(("arbitrary",)),
        name="expert_up",
    )(blk_e, nvalid, xs, w1, w3)
    return pl.pallas_call(
        _expert_down_kernel,
        grid_spec=pltpu.PrefetchScalarGridSpec(
            num_scalar_prefetch=2,
            grid=(nb,),
            in_specs=[
                pl.BlockSpec((MOE_TB, hidden), lambda i, be, nv: (i, 0)),
                pl.BlockSpec((1, 1, hidden, d), lambda i, be, nv: (l, be[i], 0, 0)),
            ],
            out_specs=pl.BlockSpec((MOE_TB, d // 2), lambda i, be, nv: (i, 0)),
        ),
        out_shape=jax.ShapeDtypeStruct((n_slots, d // 2), jnp.uint32),
        compiler_params=_cparams(("arbitrary",)),
        name="expert_down",
    )(blk_e, nvalid, up, w2)


def _slot_plan(idx):
    n = idx.shape[0]
    n_asg = n * TOP_K
    flat_e = idx[:, :TOP_K].reshape(n_asg)
    order = jnp.argsort(flat_e).astype(jnp.int32)
    experts = jnp.arange(N_EXPERTS, dtype=jnp.int32)
    counts = jnp.sum(flat_e[:, None] == experts[None, :], axis=0, dtype=jnp.int32)
    padded = (counts + MOE_TB - 1) // MOE_TB * MOE_TB
    pad_end = jnp.cumsum(padded)
    pad_start = pad_end - padded
    start = jnp.cumsum(counts) - counts
    n_slots = (n_asg + MOE_TB - 1) // MOE_TB * MOE_TB + N_EXPERTS * MOE_TB
    nb = n_slots // MOE_TB
    blk0 = jnp.arange(nb, dtype=jnp.int32) * MOE_TB
    blk_e = jnp.minimum(jnp.sum(blk0[:, None] >= pad_end[None, :], axis=1, dtype=jnp.int32), N_EXPERTS - 1)
    sel = (blk_e[:, None] == experts[None, :]).astype(jnp.int32)
    blk_cnt = jnp.sum(sel * counts[None, :], axis=1)
    blk_pad0 = jnp.sum(sel * pad_start[None, :], axis=1)
    blk_start = jnp.sum(sel * start[None, :], axis=1)
    nvalid = jnp.clip(blk_cnt - (blk0 - blk_pad0), 0, MOE_TB).astype(jnp.int32)
    blk_w = jnp.where(nvalid > 0, blk_e, jnp.max(jnp.where(nvalid > 0, blk_e, 0)))
    within = jnp.arange(MOE_TB, dtype=jnp.int32)[None, :]
    valid = within < nvalid[:, None]
    pos = jnp.clip((blk_start + blk0 - blk_pad0)[:, None] + within, 0, n_asg - 1)
    asg = order[pos.reshape(n_slots)]
    filler = jnp.arange(n_slots, dtype=jnp.int32) % n
    slot_tok = jnp.where(valid.reshape(n_slots), lax.shift_right_logical(asg, 1), filler).astype(jnp.int32)
    rank = jnp.argsort(order).astype(jnp.int32)
    sel_a = (flat_e[:, None] == experts[None, :]).astype(jnp.int32)
    dest = rank + jnp.sum(sel_a * (pad_start - start)[None, :], axis=1)
    dest_ct = dest.reshape(n, TOP_K).T.reshape(n_asg)
    return blk_w, nvalid, slot_tok, dest_ct


def _final_kernel(x_ref, y_ref, gate_ref, mod_ref, g_ref, o_ref):
    x = _combined(x_ref, y_ref, gate_ref, mod_ref)
    o_ref[0] = x * lax.rsqrt(jnp.mean(x * x, axis=-1, keepdims=True) + EPS) * g_ref[...]


def _final_call(x, y, gate, mods, gf, bsz, tpb):
    n, d = x.shape
    lat = tpb - 1
    return pl.pallas_call(
        _final_kernel,
        grid=(bsz, lat),
        in_specs=[
            pl.BlockSpec((TM, d), lambda b, j: (b * tpb + j + 1, 0)),
            pl.BlockSpec((TOP_K, TM, d // 2), lambda b, j: (0, b * tpb + j + 1, 0)),
            pl.BlockSpec((TM, LANES), lambda b, j: (b * tpb + j + 1, 0)),
            pl.BlockSpec((1, N_MOD, d), lambda b, j: (b, 0, 0)),
            pl.BlockSpec((1, d), lambda b, j: (0, 0)),
        ],
        out_specs=pl.BlockSpec((1, TM, d), lambda b, j: (b, j, 0)),
        out_shape=jax.ShapeDtypeStruct((bsz, lat * TM, d), F32),
        compiler_params=_cparams(("arbitrary", "arbitrary")),
        name="final_norm",
    )(x, y, gate, mods, gf)


def _rope_tables(seq):
    n_freq = HEAD_DK // 4
    t = jnp.arange(seq)
    inv = ROPE_BASE ** (-jnp.arange(n_freq, dtype=F32) / n_freq)
    ang = jnp.concatenate([(t // GRID_W).astype(F32)[:, None] * inv, (t % GRID_W).astype(F32)[:, None] * inv], axis=-1)
    cos = jnp.concatenate([jnp.ones((TM, HEAD_DK // 2), F32), jnp.cos(ang)], axis=0)
    sin = jnp.concatenate([jnp.zeros((TM, HEAD_DK // 2), F32), jnp.sin(ang)], axis=0)
    cos_t = jnp.tile(jnp.concatenate([cos, cos], axis=-1), (1, N_HEADS))
    sin_t = jnp.tile(jnp.concatenate([-sin, sin], axis=-1), (1, N_HEADS))
    return cos_t, sin_t


def kernel(x, c, ctx, c_ctx, norm1_g, norm2_g, ada_w, ada_b, w_in, cf_dw, cf_b, cf_ln_g, cf_ln_b, sc_dw, gla_w2,
           gla_b2, gla_ng, ret_logit, ret_ng, w_out, w_grp, b_grp, w_rt, b_rt, e_w1, e_w3, e_w2, final_g):
    bsz, seq, d = x.shape
    depth = w_in.shape[0]
    assert d == D_MODEL and ctx.shape[1] == TM and seq % TM == 0 and bsz == 2 and TOP_K == 2
    assert w_in.shape[-1] == IN_W
    tpb = 1 + seq // TM
    n = bsz * tpb * TM
    nk = N_HEADS * HEAD_DK

    s8 = jnp.concatenate([c, c_ctx[None, :], jnp.zeros((8 - bsz - 1, d), F32)], axis=0)
    mods_all = _ada_call(s8, ada_w, ada_b)[:, :bsz + 1, :].reshape(depth, bsz + 1, N_MOD, d)

    w_in_p = _pack_w_in(w_in)
    w_out_b = _cast_w_out(w_out)
    cos_t, sin_t = _rope_tables(seq)
    w2pad = jnp.zeros((depth, 2, LANES, nk), F32)
    w2pad = w2pad.at[:, 0, 0:GLA_RANK, :].set(gla_w2[:, 0]).at[:, 1, GLA_RANK:2 * GLA_RANK, :].set(gla_w2[:, 1])
    wr_all = jnp.concatenate([w_rt, w_grp, jnp.zeros((depth, d, LANES - N_EXPERTS - N_GROUPS), F32)], axis=-1)
    br_all = jnp.concatenate([b_rt, b_grp, jnp.zeros((depth, LANES - N_EXPERTS - N_GROUPS), F32)], axis=-1)

    out = None
    stream = (ctx.reshape(bsz * TM, d), x.reshape(bsz * seq, d))
    for l in range(depth):
        mods = mods_all[l]
        zc, zg, zr, xa = _inproj_call(stream, mods, norm1_g[l][None, :], w_in_p, l, tpb, n)
        mixc = _conv_call(zc, cf_dw[l], cf_b[l][None, :], cf_ln_g[l][None, :], cf_ln_b[l][None, :], sc_dw[l], tpb)
        og = _gla_call(zg, w2pad[l], gla_b2[l][:, None, :], bsz, tpb)
        lg = jnp.broadcast_to(ret_logit[l][:, :, None], (2, N_HEADS, LANES))
        orr = _ret_call(zr, cos_t, sin_t, lg, bsz, tpb)
        xa, h2, idx, gate = _outproj_call(mixc, og, orr, zg, zr, gla_ng[l][None, :], ret_ng[l][None, :], w_out_b, l,
                                          xa, mods, norm2_g[l][None, :], wr_all[l], br_all[l][None, :], bsz, tpb)
        blk_e, nvalid, slot_tok, dest_ct = _slot_plan(idx)
        xs = _sc_gather(h2, slot_tok)
        ys = _experts_call(blk_e, nvalid, xs, e_w1, e_w3, e_w2, l)
        y = _sc_gather(ys, dest_ct).reshape(TOP_K, n, d // 2)
        if l == depth - 1:
            out = _final_call(xa, y, gate, mods, final_g[None, :], bsz, tpb)
        else:
            stream = (xa, y, gate, mods)
    return out
```

```python
import functools

import jax
import jax.numpy as jnp
from jax import lax
from jax.experimental import pallas as pl
from jax.experimental.pallas import tpu as pltpu
from jax.experimental.pallas import tpu_sc as plsc

F32 = jnp.float32
BF16 = jnp.bfloat16

D_MODEL = 2048
GRID_W = 64
GROUP_W = D_MODEL // 4
CF_KERNEL = 31
SC_KERNEL = 3
N_HEADS = 4
HEAD_DK = 64
HEAD_DV = 128
GLA_RANK = 16
GLA_TAU = 16.0
GLA_CHUNK = 128
ROPE_BASE = 10000.0
N_GROUPS = 4
EXPERTS_PER_GROUP = 4
N_EXPERTS = N_GROUPS * EXPERTS_PER_GROUP
TOP_K = 2
EXPERT_HIDDEN = D_MODEL // 2
N_MOD = 6
EPS = 1e-6

TM = 256
LANES = 128
SUBLANES = 8
ADA_TN = 1024
MOE_TB = 256
VMEM_LIMIT = 56 * 1024 * 1024
SC_WORKERS = 32
SC_WINDOW = 16

ZC_W = 5 * GROUP_W
ZG_W = 3 * GROUP_W + LANES
ZR_W = 3 * GROUP_W
Z_W = ZC_W + ZG_W + ZR_W
IN_W = Z_W - (LANES - 2 * GLA_RANK)
GLR_END = ZC_W + 3 * GROUP_W + 2 * GLA_RANK
PACK_MOVES = (
    (3 * GROUP_W, 0, 2 * GROUP_W),
    (2 * GROUP_W, 2 * GROUP_W, GROUP_W),
    (0, 3 * GROUP_W, 2 * GROUP_W),
    (ZC_W, ZC_W, 3 * GROUP_W),
    (GLR_END, ZC_W + ZG_W, ZR_W),
)


def _cparams(sem):
    return pltpu.CompilerParams(dimension_semantics=sem, vmem_limit_bytes=VMEM_LIMIT)


def _sigmoid(x):
    return 1.0 / (1.0 + jnp.exp(-x))


def _silu(x):
    return x * _sigmoid(x)


def _log_sigmoid(x):
    return jnp.minimum(x, 0.0) - jnp.log1p(jnp.exp(-jnp.abs(x)))


def _dot(a, b):
    return jnp.dot(a, b, preferred_element_type=F32)


def _dot_nt(a, b):
    return lax.dot_general(a, b, (((1,), (1,)), ((), ())), preferred_element_type=F32)


def _dot_tn(a, b):
    return lax.dot_general(a, b, (((0,), (0,)), ((), ())), preferred_element_type=F32)


def _pack_pairs(x):
    w = x.shape[1] // 2
    xb = x.astype(BF16).astype(F32)
    hi = pltpu.bitcast(xb[:, :w], jnp.uint32)
    lo = pltpu.bitcast(xb[:, w:], jnp.uint32)
    return hi | lax.shift_right_logical(lo, jnp.uint32(16))


def _unpack_pairs(p):
    hi = pltpu.bitcast(p & jnp.uint32(0xFFFF0000), F32)
    lo = pltpu.bitcast(lax.shift_left(p, jnp.uint32(16)), F32)
    return jnp.concatenate([hi, lo], axis=1)


def _mod_row(i, tpb):
    return jnp.where(i % tpb == 0, 2, i // tpb)


def _pack_kernel(wt_ref, o_ref):
    def put(src, dst, width):
        for c in range(0, width, TM):
            step = min(TM, width - c)
            o_ref[0, :, dst + c:dst + c + step] = wt_ref[0, src + c:src + c + step, :].T.astype(BF16)

    for src, dst, width in PACK_MOVES:
        put(src, dst, width)
    glr0 = GLR_END - 2 * GLA_RANK
    tile = wt_ref[0, glr0:glr0 + LANES, :].T
    lane = lax.broadcasted_iota(jnp.int32, tile.shape, 1)
    o_ref[0, :, glr0:glr0 + LANES] = jnp.where(lane < 2 * GLA_RANK, tile, 0.0).astype(BF16)


def _pack_w_in(w_in):
    depth, d, _ = w_in.shape
    return pl.pallas_call(
        _pack_kernel,
        grid=(depth, d // TM),
        in_specs=[pl.BlockSpec((1, IN_W, TM), lambda l, i: (l, 0, i))],
        out_specs=pl.BlockSpec((1, TM, Z_W), lambda l, i: (l, i, 0)),
        out_shape=jax.ShapeDtypeStruct((depth, d, Z_W), BF16),
        compiler_params=_cparams(("arbitrary", "arbitrary")),
        name="pack_w_in",
    )(jnp.swapaxes(w_in, 1, 2))


def _cast_kernel(w_ref, o_ref):
    o_ref[...] = w_ref[...].astype(BF16)


def _cast_w_out(w_out):
    depth, k, d = w_out.shape
    return pl.pallas_call(
        _cast_kernel,
        grid=(depth, k // TM),
        in_specs=[pl.BlockSpec((1, TM, d), lambda l, i: (l, i, 0))],
        out_specs=pl.BlockSpec((1, TM, d), lambda l, i: (l, i, 0)),
        out_shape=jax.ShapeDtypeStruct((depth, k, d), BF16),
        compiler_params=_cparams(("arbitrary", "arbitrary")),
        name="cast_w_out",
    )(w_out)


def _ada_kernel(s_ref, w_ref, b_ref, o_ref):
    a = _silu(s_ref[...]).astype(BF16)
    o_ref[0] = _dot(a, w_ref[0].astype(BF16)) + b_ref[0]


def _ada_call(s8, ada_w, ada_b):
    depth, d, nm = ada_w.shape
    return pl.pallas_call(
        _ada_kernel,
        grid=(depth, nm // ADA_TN),
        in_specs=[
            pl.BlockSpec((8, d), lambda l, j: (0, 0)),
            pl.BlockSpec((1, d, ADA_TN), lambda l, j: (l, 0, j)),
            pl.BlockSpec((1, 1, ADA_TN), lambda l, j: (l, 0, j)),
        ],
        out_specs=pl.BlockSpec((1, 8, ADA_TN), lambda l, j: (l, 0, j)),
        out_shape=jax.ShapeDtypeStruct((depth, 8, nm), F32),
        compiler_params=_cparams(("arbitrary", "arbitrary")),
        name="adaln",
    )(s8, ada_w, ada_b.reshape(depth, 1, nm))


def _combined(x_ref, y_ref, gate_ref, mod_ref):
    f = gate_ref[:, 0:1] * _unpack_pairs(y_ref[0]) + gate_ref[:, 1:2] * _unpack_pairs(y_ref[1])
    return x_ref[...] + mod_ref[0, 5:6, :] * f


def _inproj_kernel(*refs, first, tpb):
    if first:
        ctx_ref, lat_ref, mod_ref, g_ref, w_ref, zc_ref, zg_ref, zr_ref, xo_ref = refs
        x = jnp.where(pl.program_id(0) % tpb == 0, ctx_ref[...], lat_ref[...])
    else:
        x_ref, y_ref, gate_ref, pmod_ref, mod_ref, g_ref, w_ref, zc_ref, zg_ref, zr_ref, xo_ref = refs
        x = _combined(x_ref, y_ref, gate_ref, pmod_ref)
    xo_ref[...] = x
    y = x * lax.rsqrt(jnp.mean(x * x, axis=-1, keepdims=True) + EPS) * g_ref[...]
    h = (y * (1.0 + mod_ref[0, 1:2, :]) + mod_ref[0, 0:1, :]).astype(BF16)
    off = 0
    for ref, width in ((zc_ref, ZC_W), (zg_ref, ZG_W), (zr_ref, ZR_W)):
        c = 0
        while c < width:
            step = min(512, width - c)
            ref[:, c:c + step] = _dot(h, w_ref[0, :, off + c:off + c + step]).astype(BF16)
            c += step
        off += width


def _inproj_call(stream, mods, g1, w_packed, l, tpb, n):
    d = D_MODEL
    nt = n // TM
    nlat = tpb - 1
    mod_spec = pl.BlockSpec((1, N_MOD, d), lambda i: (_mod_row(i, tpb), 0, 0))
    first = len(stream) == 2
    if first:
        in_specs = [
            pl.BlockSpec((TM, d), lambda i: (i // tpb, 0)),
            pl.BlockSpec((TM, d), lambda i: ((i // tpb) * nlat + jnp.maximum(i % tpb - 1, 0), 0)),
        ]
    else:
        in_specs = [
            pl.BlockSpec((TM, d), lambda i: (i, 0)),
            pl.BlockSpec((TOP_K, TM, d // 2), lambda i: (0, i, 0)),
            pl.BlockSpec((TM, LANES), lambda i: (i, 0)),
            mod_spec,
        ]
    in_specs += [
        mod_spec,
        pl.BlockSpec((1, d), lambda i: (0, 0)),
        pl.BlockSpec((1, d, Z_W), lambda i: (l, 0, 0), pipeline_mode=pl.Buffered(1)),
    ]
    return pl.pallas_call(
        functools.partial(_inproj_kernel, first=first, tpb=tpb),
        grid=(nt,),
        in_specs=in_specs,
        out_specs=[
            pl.BlockSpec((TM, ZC_W), lambda i: (i, 0)),
            pl.BlockSpec((TM, ZG_W), lambda i: (i, 0)),
            pl.BlockSpec((TM, ZR_W), lambda i: (i, 0)),
            pl.BlockSpec((TM, d), lambda i: (i, 0)),
        ],
        out_shape=[
            jax.ShapeDtypeStruct((n, ZC_W), BF16),
            jax.ShapeDtypeStruct((n, ZG_W), BF16),
            jax.ShapeDtypeStruct((n, ZR_W), BF16),
            jax.ShapeDtypeStruct((n, d), F32),
        ],
        compiler_params=_cparams(("arbitrary",)),
        name="inproj",
    )(*stream, mods, g1, w_packed)


PAD_LEAD = 16
SEG = GRID_W
LAT_STRIDE = SEG + PAD_LEAD
PAD_ROWS = (TM // SEG) * LAT_STRIDE + PAD_LEAD


def _conv_kernel(zc_ref, prev_ref, next_ref, cfw_ref, cfb_ref, lng_ref, lnb_ref, scw_ref, o_ref, pad_ref, shift_ref,
                 *, tpb):
    j = pl.program_id(0) % tpb
    nseg = TM // SEG
    half = CF_KERNEL // 2
    zeros_lead = jnp.zeros((PAD_LEAD, GROUP_W), F32)

    def glu():
        cfa = zc_ref[:, 3 * GROUP_W:4 * GROUP_W].astype(F32)
        cfg = zc_ref[:, 4 * GROUP_W:5 * GROUP_W].astype(F32)
        return cfa * _sigmoid(cfg)

    def finish_cf(acc, s):
        y = acc + cfb_ref[...]
        yc = y - jnp.mean(y, axis=-1, keepdims=True)
        yn = yc * lax.rsqrt(jnp.mean(yc * yc, axis=-1, keepdims=True) + EPS)
        o_ref[s * SEG:(s + 1) * SEG, 0:GROUP_W] = _silu(yn * lng_ref[...] + lnb_ref[...]).astype(BF16)

    def conformer(stride):
        span = PAD_ROWS - SUBLANES
        for r in range(1, SUBLANES):
            shift_ref[r, 0:span, :] = pad_ref[r:r + span, :]
        for s in range(nseg):
            base = s * stride + PAD_LEAD - half
            acc = jnp.zeros((SEG, GROUP_W), F32)
            for k in range(CF_KERNEL):
                r = (base + k) % SUBLANES
                a = base + k - r
                win = pad_ref[a:a + SEG, :] if r == 0 else shift_ref[r, a:a + SEG, :]
                acc = acc + cfw_ref[k:k + 1, :] * win
            finish_cf(acc, s)

    def sc_products():
        scc = zc_ref[:, 0:GROUP_W].astype(F32)
        scv = zc_ref[:, GROUP_W:2 * GROUP_W].astype(F32)
        scb = zc_ref[:, 2 * GROUP_W:3 * GROUP_W].astype(F32)
        return scc * scv, scb

    @pl.when(j != 0)
    def _latent():
        u = glu()
        for s in range(nseg):
            pad_ref[s * LAT_STRIDE:s * LAT_STRIDE + PAD_LEAD, :] = zeros_lead
            pad_ref[s * LAT_STRIDE + PAD_LEAD:(s + 1) * LAT_STRIDE, :] = u[s * SEG:(s + 1) * SEG]
        pad_ref[nseg * LAT_STRIDE:nseg * LAT_STRIDE + PAD_LEAD, :] = zeros_lead
        conformer(LAT_STRIDE)
        usc, scb = sc_products()
        up = prev_ref[:, 0:GROUP_W].astype(F32) * prev_ref[:, GROUP_W:2 * GROUP_W].astype(F32)
        un = next_ref[:, 0:GROUP_W].astype(F32) * next_ref[:, GROUP_W:2 * GROUP_W].astype(F32)
        up = jnp.where(j == 1, 0.0, up)
        un = jnp.where(j == tpb - 1, 0.0, un)
        above = jnp.concatenate([up, usc[:TM - GRID_W]], axis=0)
        below = jnp.concatenate([usc[GRID_W:], un], axis=0)
        y = scb * (scw_ref[0:1, :] * above + scw_ref[1:2, :] * usc + scw_ref[2:3, :] * below)
        o_ref[:, GROUP_W:2 * GROUP_W] = y.astype(BF16)

    @pl.when(j == 0)
    def _context():
        u = glu()
        pad_ref[0:PAD_LEAD, :] = zeros_lead
        pad_ref[PAD_LEAD:PAD_LEAD + TM, :] = u
        pad_ref[PAD_LEAD + TM:PAD_ROWS, :] = jnp.zeros((PAD_ROWS - PAD_LEAD - TM, GROUP_W), F32)
        conformer(SEG)
        usc, scb = sc_products()
        pad_ref[PAD_LEAD:PAD_LEAD + TM, :] = usc
        before = pad_ref[PAD_LEAD - 1:PAD_LEAD - 1 + TM, :]
        after = pad_ref[PAD_LEAD + 1:PAD_LEAD + 1 + TM, :]
        y = scb * (scw_ref[0:1, :] * before + scw_ref[1:2, :] * usc + scw_ref[2:3, :] * after)
        o_ref[:, GROUP_W:2 * GROUP_W] = y.astype(BF16)


def _conv_call(zc, cfw, cfb, lng, lnb, scw, tpb):
    n = zc.shape[0]
    nt = n // TM
    r = TM // GRID_W
    nhalo = n // GRID_W
    return pl.pallas_call(
        functools.partial(_conv_kernel, tpb=tpb),
        grid=(nt,),
        in_specs=[
            pl.BlockSpec((TM, ZC_W), lambda i: (i, 0)),
            pl.BlockSpec((GRID_W, 2 * GROUP_W), lambda i: (jnp.maximum(i * r - 1, 0), 0)),
            pl.BlockSpec((GRID_W, 2 * GROUP_W), lambda i: (jnp.minimum(i * r + r, nhalo - 1), 0)),
            pl.BlockSpec((CF_KERNEL, GROUP_W), lambda i: (0, 0)),
            pl.BlockSpec((1, GROUP_W), lambda i: (0, 0)),
            pl.BlockSpec((1, GROUP_W), lambda i: (0, 0)),
            pl.BlockSpec((1, GROUP_W), lambda i: (0, 0)),
            pl.BlockSpec((SC_KERNEL, GROUP_W), lambda i: (0, 0)),
        ],
        out_specs=pl.BlockSpec((TM, 2 * GROUP_W), lambda i: (i, 0)),
        out_shape=jax.ShapeDtypeStruct((n, 2 * GROUP_W), BF16),
        scratch_shapes=[pltpu.VMEM((PAD_ROWS, GROUP_W), F32), pltpu.VMEM((SUBLANES, PAD_ROWS, GROUP_W), F32)],
        compiler_params=_cparams(("arbitrary",)),
        name="convmix",
    )(zc, zc, zc, cfw, cfb, lng, lnb, scw)


def _scan_block(d, j, tpb):
    return jnp.where(d == 0, j, jnp.where(j == 0, 0, tpb - j))


def _tri(rev, size):
    ri = lax.broadcasted_iota(jnp.int32, (size, size), 0)
    ci = lax.broadcasted_iota(jnp.int32, (size, size), 1)
    return (ri <= ci) if rev else (ri >= ci)


def _gla_kernel(zg_ref, w2_ref, b2_ref, o_ref, st_ref):
    d = pl.program_id(0)

    @pl.when(pl.program_id(1) == 0)
    def _init():
        st_ref[...] = jnp.zeros_like(st_ref)

    def run(rev):
        for b in range(zg_ref.shape[0]):
            run_batch(rev, zg_ref.at[b], o_ref.at[0, b], st_ref.at[b])

    def run_batch(rev, zg_ref, o_ref, st_ref):
        nk = N_HEADS * HEAD_DK
        zz = _dot(zg_ref[:, 3 * GROUP_W:3 * GROUP_W + LANES], w2_ref[0].astype(BF16)) + b2_ref[0]
        la = _log_sigmoid(zz) * (1.0 / GLA_TAU)
        tri = _tri(rev, GLA_CHUNK)
        trib = tri.astype(BF16)
        order = range(TM // GLA_CHUNK - 1, -1, -1) if rev else range(TM // GLA_CHUNK)
        states = [st_ref[h] for h in range(N_HEADS)]
        for c in order:
            rows = slice(c * GLA_CHUNK, (c + 1) * GLA_CHUNK)
            la_c = la[rows]
            hi = la_c.astype(BF16)
            lo = (la_c - hi.astype(F32)).astype(BF16)
            bc = _dot(trib, hi) + _dot(trib, lo)
            b_last = bc[0:1] if rev else bc[GLA_CHUNK - 1:GLA_CHUNK]
            b_mid = bc[GLA_CHUNK // 2:GLA_CHUNK // 2 + 1]
            q = zg_ref[rows, 0:nk].astype(F32) * (HEAD_DK ** -0.5)
            k = zg_ref[rows, nk:2 * nk].astype(F32)
            q_in = (q * jnp.exp(bc)).astype(BF16)
            q_mid = (q * jnp.exp(bc - b_mid)).astype(BF16)
            k_mid = (k * jnp.exp(b_mid - bc)).astype(BF16)
            k_st = (k * jnp.exp(b_last - bc)).astype(BF16)
            a_row = jnp.exp(b_last)
            for h in range(N_HEADS):
                hs = slice(h * HEAD_DK, (h + 1) * HEAD_DK)
                vs = slice(h * HEAD_DV, (h + 1) * HEAD_DV)
                vh = zg_ref[rows, GROUP_W + h * HEAD_DV:GROUP_W + (h + 1) * HEAD_DV]
                sc = jnp.where(tri, _dot_nt(q_mid[:, hs], k_mid[:, hs]), 0.0).astype(BF16)
                st = states[h]
                o_ref[rows, vs] = _dot(sc, vh) + _dot_nt(q_in[:, hs], st.astype(BF16))
                states[h] = st * a_row[:, hs] + _dot_tn(vh, k_st[:, hs])
        for h in range(N_HEADS):
            st_ref[h] = states[h]

    @pl.when(d == 0)
    def _fwd():
        run(False)

    @pl.when(d == 1)
    def _bwd():
        run(True)


def _gla_call(zg, w2pad, b2, bsz, tpb):
    n = zg.shape[0]
    nk = N_HEADS * HEAD_DK
    return pl.pallas_call(
        _gla_kernel,
        grid=(2, tpb),
        in_specs=[
            pl.BlockSpec((bsz, TM, ZG_W), lambda d, j: (0, _scan_block(d, j, tpb), 0)),
            pl.BlockSpec((1, LANES, nk), lambda d, j: (d, 0, 0)),
            pl.BlockSpec((1, 1, nk), lambda d, j: (d, 0, 0)),
        ],
        out_specs=pl.BlockSpec((1, bsz, TM, GROUP_W), lambda d, j: (d, 0, _scan_block(d, j, tpb), 0)),
        out_shape=jax.ShapeDtypeStruct((2, bsz, n // bsz, GROUP_W), F32),
        scratch_shapes=[pltpu.VMEM((bsz, N_HEADS, HEAD_DV, HEAD_DK), F32)],
        compiler_params=_cparams(("arbitrary", "arbitrary")),
        name="gla",
    )(zg.reshape(bsz, n // bsz, ZG_W), w2pad, b2).reshape(2, n, GROUP_W)


def _ret_kernel(zr_ref, cos_ref, sin_ref, lg_ref, o_ref, st_ref):
    d = pl.program_id(0)

    @pl.when(pl.program_id(1) == 0)
    def _init():
        st_ref[...] = jnp.zeros_like(st_ref)

    def run(rev):
        for b in range(zr_ref.shape[0]):
            run_batch(rev, zr_ref.at[b], o_ref.at[0, b], st_ref.at[b])

    def run_batch(rev, zr_ref, o_ref, st_ref):
        nk = N_HEADS * HEAD_DK
        lane = lax.broadcasted_iota(jnp.int32, (TM, nk), 1)
        first_half = (lane % HEAD_DK) < (HEAD_DK // 2)
        cosf = cos_ref[...]
        sinf = sin_ref[...]

        def rope(x):
            swapped = jnp.where(first_half, pltpu.roll(x, nk - HEAD_DK // 2, 1), pltpu.roll(x, HEAD_DK // 2, 1))
            return x * cosf + swapped * sinf

        q = rope(zr_ref[:, 0:nk].astype(F32)).astype(BF16)
        k = rope(zr_ref[:, nk:2 * nk].astype(F32)) * (HEAD_DK ** -0.5)
        lg = _log_sigmoid(lg_ref[0])
        tri = _tri(rev, TM)
        ri = lax.broadcasted_iota(jnp.int32, (TM, TM), 0)
        ci = lax.broadcasted_iota(jnp.int32, (TM, TM), 1)
        rel = ((ci - ri) if rev else (ri - ci)).astype(F32)
        pos = lax.broadcasted_iota(jnp.int32, (TM, LANES), 0).astype(F32)
        q_steps = (TM - pos) if rev else (pos + 1.0)
        k_steps = pos if rev else (TM - 1.0 - pos)
        lg_wide = jnp.concatenate([lg, lg], axis=1)
        for h in range(N_HEADS):
            hs = slice(h * HEAD_DK, (h + 1) * HEAD_DK)
            vs = slice(h * HEAD_DV, (h + 1) * HEAD_DV)
            lgh = lg[h:h + 1, :]
            decay = jnp.where(tri, jnp.exp(rel * lg_wide[h:h + 1, :]), 0.0)
            q_decay = jnp.exp(q_steps * lgh)
            k_decay = jnp.exp(k_steps[:, 0:HEAD_DK] * lgh[:, 0:HEAD_DK])
            chunk_decay = jnp.exp(float(TM) * lgh[:, 0:HEAD_DK])
            st = st_ref[h]
            qh = q[:, hs]
            kh = k[:, hs]
            vh = zr_ref[:, GROUP_W + h * HEAD_DV:GROUP_W + (h + 1) * HEAD_DV]
            sc = (_dot_nt(qh, kh.astype(BF16)) * decay).astype(BF16)
            o_ref[:, vs] = _dot(sc, vh) + _dot_nt(qh, st.astype(BF16)) * q_decay
            st_ref[h] = st * chunk_decay + _dot_tn(vh, (kh * k_decay).astype(BF16))

    @pl.when(d == 0)
    def _fwd():
        run(False)

    @pl.when(d == 1)
    def _bwd():
        run(True)


def _ret_call(zr, cos_t, sin_t, lg, bsz, tpb):
    n = zr.shape[0]
    nk = N_HEADS * HEAD_DK
    return pl.pallas_call(
        _ret_kernel,
        grid=(2, tpb),
        in_specs=[
            pl.BlockSpec((bsz, TM, ZR_W), lambda d, j: (0, _scan_block(d, j, tpb), 0)),
            pl.BlockSpec((TM, nk), lambda d, j: (_scan_block(d, j, tpb), 0)),
            pl.BlockSpec((TM, nk), lambda d, j: (_scan_block(d, j, tpb), 0)),
            pl.BlockSpec((1, N_HEADS, LANES), lambda d, j: (d, 0, 0)),
        ],
        out_specs=pl.BlockSpec((1, bsz, TM, GROUP_W), lambda d, j: (d, 0, _scan_block(d, j, tpb), 0)),
        out_shape=jax.ShapeDtypeStruct((2, bsz, n // bsz, GROUP_W), F32),
        scratch_shapes=[pltpu.VMEM((bsz, N_HEADS, HEAD_DV, HEAD_DK), F32)],
        compiler_params=_cparams(("arbitrary", "arbitrary")),
        name="retention",
    )(zr.reshape(bsz, n // bsz, ZR_W), cos_t, sin_t, lg).reshape(2, n, GROUP_W)


def _head_norm(o, g, center):
    outs = []
    for h in range(N_HEADS):
        oh = o[:, h * HEAD_DV:(h + 1) * HEAD_DV]
        if center:
            oh = oh - jnp.mean(oh, axis=-1, keepdims=True)
        outs.append(oh * lax.rsqrt(jnp.mean(oh * oh, axis=-1, keepdims=True) + EPS))
    return jnp.concatenate(outs, axis=1) * g


def _route(logits):
    lane = lax.broadcasted_iota(jnp.int32, logits.shape, 1).astype(F32)
    neg = jnp.float32(-jnp.inf)
    big = jnp.float32(LANES)
    gmask = (lane >= N_EXPERTS) & (lane < N_EXPERTS + N_GROUPS)
    gl = jnp.where(gmask, logits, neg)
    gmax = jnp.max(gl, axis=-1, keepdims=True)
    gidx = jnp.min(jnp.where(gl == gmax, lane - N_EXPERTS, big), axis=-1, keepdims=True)
    g_w = 1.0 / jnp.sum(jnp.where(gmask, jnp.exp(logits - gmax), 0.0), axis=-1, keepdims=True)
    emask = (lane >= gidx * EXPERTS_PER_GROUP) & (lane < (gidx + 1.0) * EXPERTS_PER_GROUP)
    el = jnp.where(emask, logits, neg)
    e1 = jnp.max(el, axis=-1, keepdims=True)
    i1 = jnp.min(jnp.where(el == e1, lane, big), axis=-1, keepdims=True)
    el2 = jnp.where(lane == i1, neg, el)
    e2 = jnp.max(el2, axis=-1, keepdims=True)
    i2 = jnp.min(jnp.where(el2 == e2, lane, big), axis=-1, keepdims=True)
    r = jnp.exp(e2 - e1)
    w1 = g_w / (1.0 + r)
    w2 = g_w * r / (1.0 + r)
    idx = jnp.where(lane == 0.0, i1, jnp.where(lane == 1.0, i2, 0.0)).astype(jnp.int32)
    gate = jnp.where(lane == 0.0, w1, jnp.where(lane == 1.0, w2, 0.0))
    return idx, gate


def _outproj_kernel(mixc_ref, og_ref, or_ref, gr_ref, rg_ref, gng_ref, rng_ref, wout_ref, x_ref, mod_ref, g2_ref,
                    wr_ref, br_ref, xo_ref, h_ref, idx_ref, gate_ref):
    j = pl.program_id(0)
    w = wr_ref[...]
    w_hi = w.astype(BF16)
    w_lo = (w - w_hi.astype(F32)).astype(BF16)
    for b in range(x_ref.shape[0]):
        mod = mod_ref[jnp.where(j == 0, mod_ref.shape[0] - 1, b)]
        gla = _head_norm(og_ref[0, b] + og_ref[1, b], gng_ref[...], False) * _silu(gr_ref[b].astype(F32))
        ret = _head_norm(or_ref[0, b] + or_ref[1, b], rng_ref[...], True) * _silu(rg_ref[b].astype(F32))
        y = (_dot(mixc_ref[b], wout_ref[0, 0:2 * GROUP_W, :])
             + _dot(gla.astype(BF16), wout_ref[0, 2 * GROUP_W:3 * GROUP_W, :])
             + _dot(ret.astype(BF16), wout_ref[0, 3 * GROUP_W:4 * GROUP_W, :]))
        x = x_ref[b] + mod[2:3, :] * y
        xo_ref[b] = x
        hn = x * lax.rsqrt(jnp.mean(x * x, axis=-1, keepdims=True) + EPS) * g2_ref[...]
        h = hn * (1.0 + mod[4:5, :]) + mod[3:4, :]
        h_ref[b] = _pack_pairs(h)
        hb = h.astype(BF16)
        h_lo = (h - hb.astype(F32)).astype(BF16)
        logits = _dot(hb, w_hi) + _dot(hb, w_lo) + _dot(h_lo, w_hi) + br_ref[...]
        idx, gate = _route(logits)
        idx_ref[b] = idx
        gate_ref[b] = gate


def _outproj_call(mixc, og, orr, zg, zr, gng, rng, wout, l, x, mods, g2, wr, br, bsz, tpb):
    n, d = x.shape
    t = n // bsz

    def rows(width, col=0):
        return pl.BlockSpec((bsz, TM, width), lambda j: (0, j, col))

    scan = pl.BlockSpec((2, bsz, TM, GROUP_W), lambda j: (0, 0, j, 0))
    outs = pl.pallas_call(
        _outproj_kernel,
        grid=(tpb,),
        in_specs=[
            rows(2 * GROUP_W),
            scan,
            scan,
            rows(GROUP_W, 2),
            rows(GROUP_W, 2),
            pl.BlockSpec((1, GROUP_W), lambda j: (0, 0)),
            pl.BlockSpec((1, GROUP_W), lambda j: (0, 0)),
            pl.BlockSpec((1, d, d), lambda j: (l, 0, 0), pipeline_mode=pl.Buffered(1)),
            rows(d),
            pl.BlockSpec((bsz + 1, N_MOD, d), lambda j: (0, 0, 0)),
            pl.BlockSpec((1, d), lambda j: (0, 0)),
            pl.BlockSpec((d, LANES), lambda j: (0, 0)),
            pl.BlockSpec((1, LANES), lambda j: (0, 0)),
        ],
        out_specs=[rows(d), rows(d // 2), rows(LANES), rows(LANES)],
        out_shape=[
            jax.ShapeDtypeStruct((bsz, t, d), F32),
            jax.ShapeDtypeStruct((bsz, t, d // 2), jnp.uint32),
            jax.ShapeDtypeStruct((bsz, t, LANES), jnp.int32),
            jax.ShapeDtypeStruct((bsz, t, LANES), F32),
        ],
        compiler_params=_cparams(("arbitrary",)),
        name="outproj",
    )(mixc.reshape(bsz, t, -1), og.reshape(2, bsz, t, -1), orr.reshape(2, bsz, t, -1), zg.reshape(bsz, t, -1),
      zr.reshape(bsz, t, -1), gng, rng, wout, x.reshape(bsz, t, d), mods, g2, wr, br)
    return [o.reshape(n, -1) for o in outs]


def _sc_gather(table, idx):
    n_idx = idx.shape[0]
    width = table.shape[1]
    assert n_idx % (SC_WINDOW * SC_WORKERS) == 0
    per = n_idx // SC_WORKERS
    n_steps = per // SC_WINDOW
    assert n_steps % 2 == 0
    per_pad = (per + LANES - 1) // LANES * LANES
    idx_w = jnp.pad(idx.reshape(SC_WORKERS, per), ((0, 0), (0, per_pad - per)))
    mesh = plsc.VectorSubcoreMesh(core_axis_name="core", subcore_axis_name="subcore")
    n_cores = SC_WORKERS // 16

    @functools.partial(
        pl.kernel,
        out_type=jax.ShapeDtypeStruct((n_idx, width), table.dtype),
        mesh=mesh,
        scratch_types=[
            pltpu.VMEM((per_pad,), jnp.int32),
            pltpu.VMEM((2, SC_WINDOW, width), table.dtype),
            pltpu.SemaphoreType.DMA((2,)),
            pltpu.SemaphoreType.DMA((2,)),
        ],
    )
    def gather_kernel(table_hbm, idx_hbm, out_hbm, idx_v, rows_v, sem_g, sem_w):
        wid = lax.axis_index("subcore") * n_cores + lax.axis_index("core")
        base = wid * per
        pltpu.sync_copy(idx_hbm.at[wid], idx_v)

        def gather(step, buf):
            return pltpu.make_async_copy(table_hbm.at[idx_v.at[pl.ds(step * SC_WINDOW, SC_WINDOW)]],
                                         rows_v.at[buf], sem_g.at[buf])

        def write(step, buf):
            return pltpu.make_async_copy(rows_v.at[buf], out_hbm.at[pl.ds(base + step * SC_WINDOW, SC_WINDOW)],
                                         sem_w.at[buf])

        gather(0, 0).start()

        @pl.loop(0, n_steps, step=2)
        def _(s):
            for buf in range(2):
                step = s + buf
                other = 1 - buf
                gather(step, buf).wait()
                write(step, buf).start()

                @pl.when(step >= 1)
                def _():
                    write(step - 1, other).wait()

                @pl.when(step + 1 < n_steps)
                def _():
                    gather(step + 1, other).start()

        write(n_steps - 1, 1).wait()

    return gather_kernel(table, idx_w)


def _new_expert(blk_e_ref):
    i = pl.program_id(0)
    return (i == 0) | (blk_e_ref[i] != blk_e_ref[jnp.maximum(i - 1, 0)])


def _expert_up_kernel(blk_e_ref, nvalid_ref, x_ref, w1_ref, w3_ref, h_ref, w1b_ref, w3b_ref):
    @pl.when(nvalid_ref[pl.program_id(0)] > 0)
    def _compute():
        @pl.when(_new_expert(blk_e_ref))
        def _cast():
            w1b_ref[...] = w1_ref[0, 0].astype(BF16)
            w3b_ref[...] = w3_ref[0, 0].astype(BF16)

        x = _unpack_pairs(x_ref[...]).astype(BF16)
        h1 = _dot(x, w1b_ref[...])
        h3 = _dot(x, w3b_ref[...])
        h_ref[...] = (_silu(h1) * h3).astype(BF16)


def _expert_down_kernel(blk_e_ref, nvalid_ref, h_ref, w2_ref, y_ref, w2b_ref):
    @pl.when(nvalid_ref[pl.program_id(0)] > 0)
    def _compute():
        @pl.when(_new_expert(blk_e_ref))
        def _cast():
            w2b_ref[...] = w2_ref[0, 0].astype(BF16)

        y_ref[...] = _pack_pairs(_dot(h_ref[...], w2b_ref[...]))


def _experts_call(blk_e, nvalid, xs, w1, w3, w2, l):
    n_slots = xs.shape[0]
    d = w1.shape[-2]
    nb = n_slots // MOE_TB
    hidden = w1.shape[-1]
    up = pl.pallas_call(
        _expert_up_kernel,
        grid_spec=pltpu.PrefetchScalarGridSpec(
            num_scalar_prefetch=2,
            grid=(nb,),
            in_specs=[
                pl.BlockSpec((MOE_TB, d // 2), lambda i, be, nv: (i, 0)),
                pl.BlockSpec((1, 1, d, hidden), lambda i, be, nv: (l, be[i], 0, 0)),
                pl.BlockSpec((1, 1, d, hidden), lambda i, be, nv: (l, be[i], 0, 0)),
            ],
            out_specs=pl.BlockSpec((MOE_TB, hidden), lambda i, be, nv: (i, 0)),
            scratch_shapes=[pltpu.VMEM((d, hidden), BF16), pltpu.VMEM((d, hidden), BF16)],
        ),
        out_shape=jax.ShapeDtypeStruct((n_slots, hidden), BF16),
        compiler_params=_cparams(("arbitrary",)),
        name="expert_up",
    )(blk_e, nvalid, xs, w1, w3)
    return pl.pallas_call(
        _expert_down_kernel,
        grid_spec=pltpu.PrefetchScalarGridSpec(
            num_scalar_prefetch=2,
            grid=(nb,),
            in_specs=[
                pl.BlockSpec((MOE_TB, hidden), lambda i, be, nv: (i, 0)),
                pl.BlockSpec((1, 1, hidden, d), lambda i, be, nv: (l, be[i], 0, 0)),
            ],
            out_specs=pl.BlockSpec((MOE_TB, d // 2), lambda i, be, nv: (i, 0)),
            scratch_shapes=[pltpu.VMEM((hidden, d), BF16)],
        ),
        out_shape=jax.ShapeDtypeStruct((n_slots, d // 2), jnp.uint32),
        compiler_params=_cparams(("arbitrary",)),
        name="expert_down",
    )(blk_e, nvalid, up, w2)


def _slot_plan(idx):
    n = idx.shape[0]
    n_asg = n * TOP_K
    flat_e = idx[:, :TOP_K].reshape(n_asg)
    order = jnp.argsort(flat_e).astype(jnp.int32)
    experts = jnp.arange(N_EXPERTS, dtype=jnp.int32)
    counts = jnp.sum(flat_e[:, None] == experts[None, :], axis=0, dtype=jnp.int32)
    padded = (counts + MOE_TB - 1) // MOE_TB * MOE_TB
    pad_end = jnp.cumsum(padded)
    pad_start = pad_end - padded
    start = jnp.cumsum(counts) - counts
    n_slots = (n_asg + MOE_TB - 1) // MOE_TB * MOE_TB + N_EXPERTS * MOE_TB
    nb = n_slots // MOE_TB
    blk0 = jnp.arange(nb, dtype=jnp.int32) * MOE_TB
    blk_e = jnp.minimum(jnp.sum(blk0[:, None] >= pad_end[None, :], axis=1, dtype=jnp.int32), N_EXPERTS - 1)
    sel = (blk_e[:, None] == experts[None, :]).astype(jnp.int32)
    blk_cnt = jnp.sum(sel * counts[None, :], axis=1)
    blk_pad0 = jnp.sum(sel * pad_start[None, :], axis=1)
    blk_start = jnp.sum(sel * start[None, :], axis=1)
    nvalid = jnp.clip(blk_cnt - (blk0 - blk_pad0), 0, MOE_TB).astype(jnp.int32)
    blk_w = jnp.where(nvalid > 0, blk_e, jnp.max(jnp.where(nvalid > 0, blk_e, 0)))
    within = jnp.arange(MOE_TB, dtype=jnp.int32)[None, :]
    valid = within < nvalid[:, None]
    pos = jnp.clip((blk_start + blk0 - blk_pad0)[:, None] + within, 0, n_asg - 1)
    asg = order[pos.reshape(n_slots)]
    filler = jnp.arange(n_slots, dtype=jnp.int32) % n
    slot_tok = jnp.where(valid.reshape(n_slots), lax.shift_right_logical(asg, 1), filler).astype(jnp.int32)
    rank = jnp.argsort(order).astype(jnp.int32)
    sel_a = (flat_e[:, None] == experts[None, :]).astype(jnp.int32)
    dest = rank + jnp.sum(sel_a * (pad_start - start)[None, :], axis=1)
    dest_ct = dest.reshape(n, TOP_K).T.reshape(n_asg)
    return blk_w, nvalid, slot_tok, dest_ct


def _final_kernel(x_ref, y_ref, gate_ref, mod_ref, g_ref, o_ref):
    x = _combined(x_ref, y_ref, gate_ref, mod_ref)
    o_ref[0] = x * lax.rsqrt(jnp.mean(x * x, axis=-1, keepdims=True) + EPS) * g_ref[...]


def _final_call(x, y, gate, mods, gf, bsz, tpb):
    n, d = x.shape
    lat = tpb - 1
    return pl.pallas_call(
        _final_kernel,
        grid=(bsz, lat),
        in_specs=[
            pl.BlockSpec((TM, d), lambda b, j: (b * tpb + j + 1, 0)),
            pl.BlockSpec((TOP_K, TM, d // 2), lambda b, j: (0, b * tpb + j + 1, 0)),
            pl.BlockSpec((TM, LANES), lambda b, j: (b * tpb + j + 1, 0)),
            pl.BlockSpec((1, N_MOD, d), lambda b, j: (b, 0, 0)),
            pl.BlockSpec((1, d), lambda b, j: (0, 0)),
        ],
        out_specs=pl.BlockSpec((1, TM, d), lambda b, j: (b, j, 0)),
        out_shape=jax.ShapeDtypeStruct((bsz, lat * TM, d), F32),
        compiler_params=_cparams(("arbitrary", "arbitrary")),
        name="final_norm",
    )(x, y, gate, mods, gf)


def _rope_tables(seq):
    n_freq = HEAD_DK // 4
    t = jnp.arange(seq)
    inv = ROPE_BASE ** (-jnp.arange(n_freq, dtype=F32) / n_freq)
    ang = jnp.concatenate([(t // GRID_W).astype(F32)[:, None] * inv, (t % GRID_W).astype(F32)[:, None] * inv], axis=-1)
    cos = jnp.concatenate([jnp.ones((TM, HEAD_DK // 2), F32), jnp.cos(ang)], axis=0)
    sin = jnp.concatenate([jnp.zeros((TM, HEAD_DK // 2), F32), jnp.sin(ang)], axis=0)
    cos_t = jnp.tile(jnp.concatenate([cos, cos], axis=-1), (1, N_HEADS))
    sin_t = jnp.tile(jnp.concatenate([-sin, sin], axis=-1), (1, N_HEADS))
    return cos_t, sin_t


def kernel(x, c, ctx, c_ctx, norm1_g, norm2_g, ada_w, ada_b, w_in, cf_dw, cf_b, cf_ln_g, cf_ln_b, sc_dw, gla_w2,
           gla_b2, gla_ng, ret_logit, ret_ng, w_out, w_grp, b_grp, w_rt, b_rt, e_w1, e_w3, e_w2, final_g):
    bsz, seq, d = x.shape
    depth = w_in.shape[0]
    assert d == D_MODEL and ctx.shape[1] == TM and seq % TM == 0 and bsz == 2 and TOP_K == 2
    assert w_in.shape[-1] == IN_W
    tpb = 1 + seq // TM
    n = bsz * tpb * TM
    nk = N_HEADS * HEAD_DK

    s8 = jnp.concatenate([c, c_ctx[None, :], jnp.zeros((8 - bsz - 1, d), F32)], axis=0)
    mods_all = _ada_call(s8, ada_w, ada_b)[:, :bsz + 1, :].reshape(depth, bsz + 1, N_MOD, d)

    w_in_p = _pack_w_in(w_in)
    w_out_b = _cast_w_out(w_out)
    cos_t, sin_t = _rope_tables(seq)
    w2pad = jnp.zeros((depth, 2, LANES, nk), F32)
    w2pad = w2pad.at[:, 0, 0:GLA_RANK, :].set(gla_w2[:, 0]).at[:, 1, GLA_RANK:2 * GLA_RANK, :].set(gla_w2[:, 1])
    wr_all = jnp.concatenate([w_rt, w_grp, jnp.zeros((depth, d, LANES - N_EXPERTS - N_GROUPS), F32)], axis=-1)
    br_all = jnp.concatenate([b_rt, b_grp, jnp.zeros((depth, LANES - N_EXPERTS - N_GROUPS), F32)], axis=-1)

    out = None
    stream = (ctx.reshape(bsz * TM, d), x.reshape(bsz * seq, d))
    for l in range(depth):
        mods = mods_all[l]
        zc, zg, zr, xa = _inproj_call(stream, mods, norm1_g[l][None, :], w_in_p, l, tpb, n)
        mixc = _conv_call(zc, cf_dw[l], cf_b[l][None, :], cf_ln_g[l][None, :], cf_ln_b[l][None, :], sc_dw[l], tpb)
        og = _gla_call(zg, w2pad[l], gla_b2[l][:, None, :], bsz, tpb)
        lg = jnp.broadcast_to(ret_logit[l][:, :, None], (2, N_HEADS, LANES))
        orr = _ret_call(zr, cos_t, sin_t, lg, bsz, tpb)
        xa, h2, idx, gate = _outproj_call(mixc, og, orr, zg, zr, gla_ng[l][None, :], ret_ng[l][None, :], w_out_b, l,
                                          xa, mods, norm2_g[l][None, :], wr_all[l], br_all[l][None, :], bsz, tpb)
        blk_e, nvalid, slot_tok, dest_ct = _slot_plan(idx)
        xs = _sc_gather(h2, slot_tok)
        ys = _experts_call(blk_e, nvalid, xs, e_w1, e_w3, e_w2, l)
        y = _sc_gather(ys, dest_ct).reshape(TOP_K, n, d // 2)
        if l == depth - 1:
            out = _final_call(xa, y, gate, mods, final_g[None, :], bsz, tpb)
        else:
            stream = (xa, y, gate, mods)
    return out
```

```python
import functools

import jax
import jax.numpy as jnp
from jax import lax
from jax.experimental import pallas as pl
from jax.experimental.pallas import tpu as pltpu
from jax.experimental.pallas import tpu_sc as plsc

F32 = jnp.float32
BF16 = jnp.bfloat16

D_MODEL = 2048
GRID_W = 64
GROUP_W = D_MODEL // 4
CF_KERNEL = 31
SC_KERNEL = 3
N_HEADS = 4
HEAD_DK = 64
HEAD_DV = 128
GLA_RANK = 16
GLA_TAU = 16.0
GLA_CHUNK = 128
ROPE_BASE = 10000.0
N_GROUPS = 4
EXPERTS_PER_GROUP = 4
N_EXPERTS = N_GROUPS * EXPERTS_PER_GROUP
TOP_K = 2
EXPERT_HIDDEN = D_MODEL // 2
N_MOD = 6
EPS = 1e-6

TM = 256
LANES = 128
SUBLANES = 8
ADA_TN = 1024
MOE_TB = 512
VMEM_LIMIT = 56 * 1024 * 1024
SC_WORKERS = 32
SC_WINDOW = 16

ZC_W = 5 * GROUP_W
ZG_W = 3 * GROUP_W + LANES
ZR_W = 3 * GROUP_W
Z_W = ZC_W + ZG_W + ZR_W
IN_W = Z_W - (LANES - 2 * GLA_RANK)
GLR_END = ZC_W + 3 * GROUP_W + 2 * GLA_RANK
PACK_MOVES = (
    (3 * GROUP_W, 0, 2 * GROUP_W),
    (2 * GROUP_W, 2 * GROUP_W, GROUP_W),
    (0, 3 * GROUP_W, 2 * GROUP_W),
    (ZC_W, ZC_W, 3 * GROUP_W),
    (GLR_END, ZC_W + ZG_W, ZR_W),
)


def _cparams(sem):
    return pltpu.CompilerParams(dimension_semantics=sem, vmem_limit_bytes=VMEM_LIMIT)


def _sigmoid(x):
    return 1.0 / (1.0 + jnp.exp(-x))


def _silu(x):
    return x * _sigmoid(x)


def _log_sigmoid(x):
    return jnp.minimum(x, 0.0) - jnp.log1p(jnp.exp(-jnp.abs(x)))


def _dot(a, b):
    return jnp.dot(a, b, preferred_element_type=F32)


def _dot_nt(a, b):
    return lax.dot_general(a, b, (((1,), (1,)), ((), ())), preferred_element_type=F32)


def _dot_tn(a, b):
    return lax.dot_general(a, b, (((0,), (0,)), ((), ())), preferred_element_type=F32)


def _pack_pairs(x):
    w = x.shape[1] // 2
    xb = x.astype(BF16).astype(F32)
    hi = pltpu.bitcast(xb[:, :w], jnp.uint32)
    lo = pltpu.bitcast(xb[:, w:], jnp.uint32)
    return hi | lax.shift_right_logical(lo, jnp.uint32(16))


def _unpack_pairs(p):
    hi = pltpu.bitcast(p & jnp.uint32(0xFFFF0000), F32)
    lo = pltpu.bitcast(lax.shift_left(p, jnp.uint32(16)), F32)
    return jnp.concatenate([hi, lo], axis=1)


def _mod_row(i, tpb):
    return jnp.where(i % tpb == 0, 2, i // tpb)


def _pack_kernel(wt_ref, o_ref):
    def put(src, dst, width):
        for c in range(0, width, TM):
            step = min(TM, width - c)
            o_ref[0, :, dst + c:dst + c + step] = wt_ref[0, src + c:src + c + step, :].T.astype(BF16)

    for src, dst, width in PACK_MOVES:
        put(src, dst, width)
    glr0 = GLR_END - 2 * GLA_RANK
    tile = wt_ref[0, glr0:glr0 + LANES, :].T
    lane = lax.broadcasted_iota(jnp.int32, tile.shape, 1)
    o_ref[0, :, glr0:glr0 + LANES] = jnp.where(lane < 2 * GLA_RANK, tile, 0.0).astype(BF16)


def _pack_w_in(w_in):
    depth, d, _ = w_in.shape
    return pl.pallas_call(
        _pack_kernel,
        grid=(depth, d // TM),
        in_specs=[pl.BlockSpec((1, IN_W, TM), lambda l, i: (l, 0, i))],
        out_specs=pl.BlockSpec((1, TM, Z_W), lambda l, i: (l, i, 0)),
        out_shape=jax.ShapeDtypeStruct((depth, d, Z_W), BF16),
        compiler_params=_cparams(("arbitrary", "arbitrary")),
        name="pack_w_in",
    )(jnp.swapaxes(w_in, 1, 2))


def _cast_kernel(w_ref, o_ref):
    o_ref[...] = w_ref[...].astype(BF16)


def _cast_w_out(w_out):
    depth, k, d = w_out.shape
    return pl.pallas_call(
        _cast_kernel,
        grid=(depth, k // TM),
        in_specs=[pl.BlockSpec((1, TM, d), lambda l, i: (l, i, 0))],
        out_specs=pl.BlockSpec((1, TM, d), lambda l, i: (l, i, 0)),
        out_shape=jax.ShapeDtypeStruct((depth, k, d), BF16),
        compiler_params=_cparams(("arbitrary", "arbitrary")),
        name="cast_w_out",
    )(w_out)


def _ada_kernel(s_ref, w_ref, b_ref, o_ref):
    a = _silu(s_ref[...]).astype(BF16)
    o_ref[0] = _dot(a, w_ref[0].astype(BF16)) + b_ref[0]


def _ada_call(s8, ada_w, ada_b):
    depth, d, nm = ada_w.shape
    return pl.pallas_call(
        _ada_kernel,
        grid=(depth, nm // ADA_TN),
        in_specs=[
            pl.BlockSpec((8, d), lambda l, j: (0, 0)),
            pl.BlockSpec((1, d, ADA_TN), lambda l, j: (l, 0, j)),
            pl.BlockSpec((1, 1, ADA_TN), lambda l, j: (l, 0, j)),
        ],
        out_specs=pl.BlockSpec((1, 8, ADA_TN), lambda l, j: (l, 0, j)),
        out_shape=jax.ShapeDtypeStruct((depth, 8, nm), F32),
        compiler_params=_cparams(("arbitrary", "arbitrary")),
        name="adaln",
    )(s8, ada_w, ada_b.reshape(depth, 1, nm))


def _combined(x_ref, y_ref, gate_ref, mod_ref):
    f = gate_ref[:, 0:1] * _unpack_pairs(y_ref[0]) + gate_ref[:, 1:2] * _unpack_pairs(y_ref[1])
    return x_ref[...] + mod_ref[0, 5:6, :] * f


def _inproj_kernel(*refs, first, tpb):
    if first:
        ctx_ref, lat_ref, mod_ref, g_ref, w_ref, zc_ref, zg_ref, zr_ref, xo_ref = refs
        x = jnp.where(pl.program_id(0) % tpb == 0, ctx_ref[...], lat_ref[...])
    else:
        x_ref, y_ref, gate_ref, pmod_ref, mod_ref, g_ref, w_ref, zc_ref, zg_ref, zr_ref, xo_ref = refs
        x = _combined(x_ref, y_ref, gate_ref, pmod_ref)
    xo_ref[...] = x
    y = x * lax.rsqrt(jnp.mean(x * x, axis=-1, keepdims=True) + EPS) * g_ref[...]
    h = (y * (1.0 + mod_ref[0, 1:2, :]) + mod_ref[0, 0:1, :]).astype(BF16)
    off = 0
    for ref, width in ((zc_ref, ZC_W), (zg_ref, ZG_W), (zr_ref, ZR_W)):
        c = 0
        while c < width:
            step = min(512, width - c)
            ref[:, c:c + step] = _dot(h, w_ref[0, :, off + c:off + c + step]).astype(BF16)
            c += step
        off += width


def _inproj_call(stream, mods, g1, w_packed, l, tpb, n):
    d = D_MODEL
    nt = n // TM
    nlat = tpb - 1
    mod_spec = pl.BlockSpec((1, N_MOD, d), lambda i: (_mod_row(i, tpb), 0, 0))
    first = len(stream) == 2
    if first:
        in_specs = [
            pl.BlockSpec((TM, d), lambda i: (i // tpb, 0)),
            pl.BlockSpec((TM, d), lambda i: ((i // tpb) * nlat + jnp.maximum(i % tpb - 1, 0), 0)),
        ]
    else:
        in_specs = [
            pl.BlockSpec((TM, d), lambda i: (i, 0)),
            pl.BlockSpec((TOP_K, TM, d // 2), lambda i: (0, i, 0)),
            pl.BlockSpec((TM, LANES), lambda i: (i, 0)),
            mod_spec,
        ]
    in_specs += [
        mod_spec,
        pl.BlockSpec((1, d), lambda i: (0, 0)),
        pl.BlockSpec((1, d, Z_W), lambda i: (l, 0, 0), pipeline_mode=pl.Buffered(1)),
    ]
    return pl.pallas_call(
        functools.partial(_inproj_kernel, first=first, tpb=tpb),
        grid=(nt,),
        in_specs=in_specs,
        out_specs=[
            pl.BlockSpec((TM, ZC_W), lambda i: (i, 0)),
            pl.BlockSpec((TM, ZG_W), lambda i: (i, 0)),
            pl.BlockSpec((TM, ZR_W), lambda i: (i, 0)),
            pl.BlockSpec((TM, d), lambda i: (i, 0)),
        ],
        out_shape=[
            jax.ShapeDtypeStruct((n, ZC_W), BF16),
            jax.ShapeDtypeStruct((n, ZG_W), BF16),
            jax.ShapeDtypeStruct((n, ZR_W), BF16),
            jax.ShapeDtypeStruct((n, d), F32),
        ],
        compiler_params=_cparams(("arbitrary",)),
        name="inproj",
    )(*stream, mods, g1, w_packed)


PAD_LEAD = 16
SEG = GRID_W
LAT_STRIDE = SEG + PAD_LEAD
PAD_ROWS = (TM // SEG) * LAT_STRIDE + PAD_LEAD


def _conv_kernel(zc_ref, prev_ref, next_ref, cfw_ref, cfb_ref, lng_ref, lnb_ref, scw_ref, o_ref, pad_ref, shift_ref,
                 *, tpb):
    j = pl.program_id(0) % tpb
    nseg = TM // SEG
    half = CF_KERNEL // 2
    zeros_lead = jnp.zeros((PAD_LEAD, GROUP_W), F32)

    def glu():
        cfa = zc_ref[:, 3 * GROUP_W:4 * GROUP_W].astype(F32)
        cfg = zc_ref[:, 4 * GROUP_W:5 * GROUP_W].astype(F32)
        return cfa * _sigmoid(cfg)

    def finish_cf(acc, s):
        y = acc + cfb_ref[...]
        yc = y - jnp.mean(y, axis=-1, keepdims=True)
        yn = yc * lax.rsqrt(jnp.mean(yc * yc, axis=-1, keepdims=True) + EPS)
        o_ref[s * SEG:(s + 1) * SEG, 0:GROUP_W] = _silu(yn * lng_ref[...] + lnb_ref[...]).astype(BF16)

    def conformer(stride):
        span = PAD_ROWS - SUBLANES
        for r in range(1, SUBLANES):
            shift_ref[r, 0:span, :] = pad_ref[r:r + span, :]
        for s in range(nseg):
            base = s * stride + PAD_LEAD - half
            acc = jnp.zeros((SEG, GROUP_W), F32)
            for k in range(CF_KERNEL):
                r = (base + k) % SUBLANES
                a = base + k - r
                win = pad_ref[a:a + SEG, :] if r == 0 else shift_ref[r, a:a + SEG, :]
                acc = acc + cfw_ref[k:k + 1, :] * win
            finish_cf(acc, s)

    def sc_products():
        scc = zc_ref[:, 0:GROUP_W].astype(F32)
        scv = zc_ref[:, GROUP_W:2 * GROUP_W].astype(F32)
        scb = zc_ref[:, 2 * GROUP_W:3 * GROUP_W].astype(F32)
        return scc * scv, scb

    @pl.when(j != 0)
    def _latent():
        u = glu()
        for s in range(nseg):
            pad_ref[s * LAT_STRIDE:s * LAT_STRIDE + PAD_LEAD, :] = zeros_lead
            pad_ref[s * LAT_STRIDE + PAD_LEAD:(s + 1) * LAT_STRIDE, :] = u[s * SEG:(s + 1) * SEG]
        pad_ref[nseg * LAT_STRIDE:nseg * LAT_STRIDE + PAD_LEAD, :] = zeros_lead
        conformer(LAT_STRIDE)
        usc, scb = sc_products()
        up = prev_ref[:, 0:GROUP_W].astype(F32) * prev_ref[:, GROUP_W:2 * GROUP_W].astype(F32)
        un = next_ref[:, 0:GROUP_W].astype(F32) * next_ref[:, GROUP_W:2 * GROUP_W].astype(F32)
        up = jnp.where(j == 1, 0.0, up)
        un = jnp.where(j == tpb - 1, 0.0, un)
        above = jnp.concatenate([up, usc[:TM - GRID_W]], axis=0)
        below = jnp.concatenate([usc[GRID_W:], un], axis=0)
        y = scb * (scw_ref[0:1, :] * above + scw_ref[1:2, :] * usc + scw_ref[2:3, :] * below)
        o_ref[:, GROUP_W:2 * GROUP_W] = y.astype(BF16)

    @pl.when(j == 0)
    def _context():
        u = glu()
        pad_ref[0:PAD_LEAD, :] = zeros_lead
        pad_ref[PAD_LEAD:PAD_LEAD + TM, :] = u
        pad_ref[PAD_LEAD + TM:PAD_ROWS, :] = jnp.zeros((PAD_ROWS - PAD_LEAD - TM, GROUP_W), F32)
        conformer(SEG)
        usc, scb = sc_products()
        pad_ref[PAD_LEAD:PAD_LEAD + TM, :] = usc
        before = pad_ref[PAD_LEAD - 1:PAD_LEAD - 1 + TM, :]
        after = pad_ref[PAD_LEAD + 1:PAD_LEAD + 1 + TM, :]
        y = scb * (scw_ref[0:1, :] * before + scw_ref[1:2, :] * usc + scw_ref[2:3, :] * after)
        o_ref[:, GROUP_W:2 * GROUP_W] = y.astype(BF16)


def _conv_call(zc, cfw, cfb, lng, lnb, scw, tpb):
    n = zc.shape[0]
    nt = n // TM
    r = TM // GRID_W
    nhalo = n // GRID_W
    return pl.pallas_call(
        functools.partial(_conv_kernel, tpb=tpb),
        grid=(nt,),
        in_specs=[
            pl.BlockSpec((TM, ZC_W), lambda i: (i, 0)),
            pl.BlockSpec((GRID_W, 2 * GROUP_W), lambda i: (jnp.maximum(i * r - 1, 0), 0)),
            pl.BlockSpec((GRID_W, 2 * GROUP_W), lambda i: (jnp.minimum(i * r + r, nhalo - 1), 0)),
            pl.BlockSpec((CF_KERNEL, GROUP_W), lambda i: (0, 0)),
            pl.BlockSpec((1, GROUP_W), lambda i: (0, 0)),
            pl.BlockSpec((1, GROUP_W), lambda i: (0, 0)),
            pl.BlockSpec((1, GROUP_W), lambda i: (0, 0)),
            pl.BlockSpec((SC_KERNEL, GROUP_W), lambda i: (0, 0)),
        ],
        out_specs=pl.BlockSpec((TM, 2 * GROUP_W), lambda i: (i, 0)),
        out_shape=jax.ShapeDtypeStruct((n, 2 * GROUP_W), BF16),
        scratch_shapes=[pltpu.VMEM((PAD_ROWS, GROUP_W), F32), pltpu.VMEM((SUBLANES, PAD_ROWS, GROUP_W), F32)],
        compiler_params=_cparams(("arbitrary",)),
        name="convmix",
    )(zc, zc, zc, cfw, cfb, lng, lnb, scw)


def _scan_block(d, j, tpb):
    return jnp.where(d == 0, j, jnp.where(j == 0, 0, tpb - j))


def _tri(rev, size):
    ri = lax.broadcasted_iota(jnp.int32, (size, size), 0)
    ci = lax.broadcasted_iota(jnp.int32, (size, size), 1)
    return (ri <= ci) if rev else (ri >= ci)


def _gla_kernel(zg_ref, w2_ref, b2_ref, o_ref, st_ref):
    d = pl.program_id(0)

    @pl.when(pl.program_id(1) == 0)
    def _init():
        st_ref[...] = jnp.zeros_like(st_ref)

    def run(rev):
        nk = N_HEADS * HEAD_DK
        n_b = zg_ref.shape[0]
        tri = _tri(rev, GLA_CHUNK)
        trib = tri.astype(BF16)
        order = range(TM // GLA_CHUNK - 1, -1, -1) if rev else range(TM // GLA_CHUNK)
        w2b = w2_ref[0].astype(BF16)
        la = []
        for b in range(n_b):
            zz = _dot(zg_ref[b, :, 3 * GROUP_W:3 * GROUP_W + LANES], w2b) + b2_ref[0]
            la.append(_log_sigmoid(zz) * (1.0 / GLA_TAU))
        pre = {}
        for b in range(n_b):
            for c in order:
                rows = slice(c * GLA_CHUNK, (c + 1) * GLA_CHUNK)
                la_c = la[b][rows]
                hi = la_c.astype(BF16)
                lo = (la_c - hi.astype(F32)).astype(BF16)
                bc = _dot(trib, hi) + _dot(trib, lo)
                b_last = bc[0:1] if rev else bc[GLA_CHUNK - 1:GLA_CHUNK]
                b_mid = bc[GLA_CHUNK // 2:GLA_CHUNK // 2 + 1]
                q = zg_ref[b, rows, 0:nk].astype(F32) * (HEAD_DK ** -0.5)
                k = zg_ref[b, rows, nk:2 * nk].astype(F32)
                pre[b, c] = dict(
                    q_in=(q * jnp.exp(bc)).astype(BF16),
                    q_mid=(q * jnp.exp(bc - b_mid)).astype(BF16),
                    k_mid=(k * jnp.exp(b_mid - bc)).astype(BF16),
                    k_st=(k * jnp.exp(b_last - bc)).astype(BF16),
                    a_row=jnp.exp(b_last),
                )
        scores = {}
        for b in range(n_b):
            for c in order:
                p = pre[b, c]
                for h in range(N_HEADS):
                    hs = slice(h * HEAD_DK, (h + 1) * HEAD_DK)
                    scores[b, c, h] = jnp.where(tri, _dot_nt(p["q_mid"][:, hs], p["k_mid"][:, hs]), 0.0).astype(BF16)
        for b in range(n_b):
            states = [st_ref[b, h] for h in range(N_HEADS)]
            for c in order:
                rows = slice(c * GLA_CHUNK, (c + 1) * GLA_CHUNK)
                p = pre[b, c]
                for h in range(N_HEADS):
                    hs = slice(h * HEAD_DK, (h + 1) * HEAD_DK)
                    vs = slice(h * HEAD_DV, (h + 1) * HEAD_DV)
                    vh = zg_ref[b, rows, GROUP_W + h * HEAD_DV:GROUP_W + (h + 1) * HEAD_DV]
                    st = states[h]
                    o_ref[0, b, rows, vs] = _dot(scores[b, c, h], vh) + _dot_nt(p["q_in"][:, hs], st.astype(BF16))
                    states[h] = st * p["a_row"][:, hs] + _dot_tn(vh, p["k_st"][:, hs])
            for h in range(N_HEADS):
                st_ref[b, h] = states[h]

    @pl.when(d == 0)
    def _fwd():
        run(False)

    @pl.when(d == 1)
    def _bwd():
        run(True)


def _gla_call(zg, w2pad, b2, bsz, tpb):
    n = zg.shape[0]
    nk = N_HEADS * HEAD_DK
    return pl.pallas_call(
        _gla_kernel,
        grid=(2, tpb),
        in_specs=[
            pl.BlockSpec((bsz, TM, ZG_W), lambda d, j: (0, _scan_block(d, j, tpb), 0)),
            pl.BlockSpec((1, LANES, nk), lambda d, j: (d, 0, 0)),
            pl.BlockSpec((1, 1, nk), lambda d, j: (d, 0, 0)),
        ],
        out_specs=pl.BlockSpec((1, bsz, TM, GROUP_W), lambda d, j: (d, 0, _scan_block(d, j, tpb), 0)),
        out_shape=jax.ShapeDtypeStruct((2, bsz, n // bsz, GROUP_W), F32),
        scratch_shapes=[pltpu.VMEM((bsz, N_HEADS, HEAD_DV, HEAD_DK), F32)],
        compiler_params=_cparams(("arbitrary", "arbitrary")),
        name="gla",
    )(zg.reshape(bsz, n // bsz, ZG_W), w2pad, b2).reshape(2, n, GROUP_W)


def _ret_kernel(zr_ref, cos_ref, sin_ref, lg_ref, o_ref, st_ref):
    d = pl.program_id(0)

    @pl.when(pl.program_id(1) == 0)
    def _init():
        st_ref[...] = jnp.zeros_like(st_ref)

    def run(rev):
        nk = N_HEADS * HEAD_DK
        batches = range(zr_ref.shape[0])
        lane = lax.broadcasted_iota(jnp.int32, (TM, nk), 1)
        first_half = (lane % HEAD_DK) < (HEAD_DK // 2)
        cosf = cos_ref[...]
        sinf = sin_ref[...]

        def rope(x):
            swapped = jnp.where(first_half, pltpu.roll(x, nk - HEAD_DK // 2, 1), pltpu.roll(x, HEAD_DK // 2, 1))
            return x * cosf + swapped * sinf

        qs = [rope(zr_ref[b, :, 0:nk].astype(F32)).astype(BF16) for b in batches]
        ks = [rope(zr_ref[b, :, nk:2 * nk].astype(F32)) * (HEAD_DK ** -0.5) for b in batches]
        lg = _log_sigmoid(lg_ref[0])
        tri = _tri(rev, TM)
        ri = lax.broadcasted_iota(jnp.int32, (TM, TM), 0)
        ci = lax.broadcasted_iota(jnp.int32, (TM, TM), 1)
        rel = ((ci - ri) if rev else (ri - ci)).astype(F32)
        pos = lax.broadcasted_iota(jnp.int32, (TM, LANES), 0).astype(F32)
        q_steps = (TM - pos) if rev else (pos + 1.0)
        k_steps = pos if rev else (TM - 1.0 - pos)
        lg_wide = jnp.concatenate([lg, lg], axis=1)
        scores = {}
        for h in range(N_HEADS):
            hs = slice(h * HEAD_DK, (h + 1) * HEAD_DK)
            decay = jnp.where(tri, jnp.exp(rel * lg_wide[h:h + 1, :]), 0.0)
            for b in batches:
                scores[b, h] = (_dot_nt(qs[b][:, hs], ks[b][:, hs].astype(BF16)) * decay).astype(BF16)
        for h in range(N_HEADS):
            hs = slice(h * HEAD_DK, (h + 1) * HEAD_DK)
            vs = slice(h * HEAD_DV, (h + 1) * HEAD_DV)
            lgh = lg[h:h + 1, :]
            q_decay = jnp.exp(q_steps * lgh)
            k_decay = jnp.exp(k_steps[:, 0:HEAD_DK] * lgh[:, 0:HEAD_DK])
            chunk_decay = jnp.exp(float(TM) * lgh[:, 0:HEAD_DK])
            for b in batches:
                st = st_ref[b, h]
                vh = zr_ref[b, :, GROUP_W + h * HEAD_DV:GROUP_W + (h + 1) * HEAD_DV]
                o_ref[0, b, :, vs] = _dot(scores[b, h], vh) + _dot_nt(qs[b][:, hs], st.astype(BF16)) * q_decay
                st_ref[b, h] = st * chunk_decay + _dot_tn(vh, (ks[b][:, hs] * k_decay).astype(BF16))

    @pl.when(d == 0)
    def _fwd():
        run(False)

    @pl.when(d == 1)
    def _bwd():
        run(True)


def _ret_call(zr, cos_t, sin_t, lg, bsz, tpb):
    n = zr.shape[0]
    nk = N_HEADS * HEAD_DK
    return pl.pallas_call(
        _ret_kernel,
        grid=(2, tpb),
        in_specs=[
            pl.BlockSpec((bsz, TM, ZR_W), lambda d, j: (0, _scan_block(d, j, tpb), 0)),
            pl.BlockSpec((TM, nk), lambda d, j: (_scan_block(d, j, tpb), 0)),
            pl.BlockSpec((TM, nk), lambda d, j: (_scan_block(d, j, tpb), 0)),
            pl.BlockSpec((1, N_HEADS, LANES), lambda d, j: (d, 0, 0)),
        ],
        out_specs=pl.BlockSpec((1, bsz, TM, GROUP_W), lambda d, j: (d, 0, _scan_block(d, j, tpb), 0)),
        out_shape=jax.ShapeDtypeStruct((2, bsz, n // bsz, GROUP_W), F32),
        scratch_shapes=[pltpu.VMEM((bsz, N_HEADS, HEAD_DV, HEAD_DK), F32)],
        compiler_params=_cparams(("arbitrary", "arbitrary")),
        name="retention",
    )(zr.reshape(bsz, n // bsz, ZR_W), cos_t, sin_t, lg).reshape(2, n, GROUP_W)


def _head_norm(o, g, center):
    outs = []
    for h in range(N_HEADS):
        oh = o[:, h * HEAD_DV:(h + 1) * HEAD_DV]
        if center:
            oh = oh - jnp.mean(oh, axis=-1, keepdims=True)
        outs.append(oh * lax.rsqrt(jnp.mean(oh * oh, axis=-1, keepdims=True) + EPS))
    return jnp.concatenate(outs, axis=1) * g


def _route(logits):
    lane = lax.broadcasted_iota(jnp.int32, logits.shape, 1).astype(F32)
    neg = jnp.float32(-jnp.inf)
    big = jnp.float32(LANES)
    gmask = (lane >= N_EXPERTS) & (lane < N_EXPERTS + N_GROUPS)
    gl = jnp.where(gmask, logits, neg)
    gmax = jnp.max(gl, axis=-1, keepdims=True)
    gidx = jnp.min(jnp.where(gl == gmax, lane - N_EXPERTS, big), axis=-1, keepdims=True)
    g_w = 1.0 / jnp.sum(jnp.where(gmask, jnp.exp(logits - gmax), 0.0), axis=-1, keepdims=True)
    emask = (lane >= gidx * EXPERTS_PER_GROUP) & (lane < (gidx + 1.0) * EXPERTS_PER_GROUP)
    el = jnp.where(emask, logits, neg)
    e1 = jnp.max(el, axis=-1, keepdims=True)
    i1 = jnp.min(jnp.where(el == e1, lane, big), axis=-1, keepdims=True)
    el2 = jnp.where(lane == i1, neg, el)
    e2 = jnp.max(el2, axis=-1, keepdims=True)
    i2 = jnp.min(jnp.where(el2 == e2, lane, big), axis=-1, keepdims=True)
    r = jnp.exp(e2 - e1)
    w1 = g_w / (1.0 + r)
    w2 = g_w * r / (1.0 + r)
    idx = jnp.where(lane == 0.0, i1, jnp.where(lane == 1.0, i2, 0.0)).astype(jnp.int32)
    gate = jnp.where(lane == 0.0, w1, jnp.where(lane == 1.0, w2, 0.0))
    return idx, gate


def _outproj_kernel(mixc_ref, og_ref, or_ref, gr_ref, rg_ref, gng_ref, rng_ref, wout_ref, x_ref, mod_ref, g2_ref,
                    wr_ref, br_ref, xo_ref, h_ref, idx_ref, gate_ref):
    j = pl.program_id(0)
    w = wr_ref[...]
    w_hi = w.astype(BF16)
    w_lo = (w - w_hi.astype(F32)).astype(BF16)
    batches = range(x_ref.shape[0])
    mods = [mod_ref[jnp.where(j == 0, mod_ref.shape[0] - 1, b)] for b in batches]
    glas = [_head_norm(og_ref[0, b] + og_ref[1, b], gng_ref[...], False) * _silu(gr_ref[b].astype(F32))
            for b in batches]
    rets = [_head_norm(or_ref[0, b] + or_ref[1, b], rng_ref[...], True) * _silu(rg_ref[b].astype(F32))
            for b in batches]
    ys = [_dot(mixc_ref[b], wout_ref[0, 0:2 * GROUP_W, :])
          + _dot(glas[b].astype(BF16), wout_ref[0, 2 * GROUP_W:3 * GROUP_W, :])
          + _dot(rets[b].astype(BF16), wout_ref[0, 3 * GROUP_W:4 * GROUP_W, :]) for b in batches]
    hs = []
    for b in batches:
        x = x_ref[b] + mods[b][2:3, :] * ys[b]
        xo_ref[b] = x
        hn = x * lax.rsqrt(jnp.mean(x * x, axis=-1, keepdims=True) + EPS) * g2_ref[...]
        h = hn * (1.0 + mods[b][4:5, :]) + mods[b][3:4, :]
        h_ref[b] = _pack_pairs(h)
        hs.append(h)
    logits = []
    for b in batches:
        hb = hs[b].astype(BF16)
        h_lo = (hs[b] - hb.astype(F32)).astype(BF16)
        logits.append(_dot(hb, w_hi) + _dot(hb, w_lo) + _dot(h_lo, w_hi) + br_ref[...])
    for b in batches:
        idx, gate = _route(logits[b])
        idx_ref[b] = idx
        gate_ref[b] = gate


def _outproj_call(mixc, og, orr, zg, zr, gng, rng, wout, l, x, mods, g2, wr, br, bsz, tpb):
    n, d = x.shape
    t = n // bsz

    def rows(width, col=0):
        return pl.BlockSpec((bsz, TM, width), lambda j: (0, j, col))

    scan = pl.BlockSpec((2, bsz, TM, GROUP_W), lambda j: (0, 0, j, 0))
    outs = pl.pallas_call(
        _outproj_kernel,
        grid=(tpb,),
        in_specs=[
            rows(2 * GROUP_W),
            scan,
            scan,
            rows(GROUP_W, 2),
            rows(GROUP_W, 2),
            pl.BlockSpec((1, GROUP_W), lambda j: (0, 0)),
            pl.BlockSpec((1, GROUP_W), lambda j: (0, 0)),
            pl.BlockSpec((1, d, d), lambda j: (l, 0, 0), pipeline_mode=pl.Buffered(1)),
            rows(d),
            pl.BlockSpec((bsz + 1, N_MOD, d), lambda j: (0, 0, 0)),
            pl.BlockSpec((1, d), lambda j: (0, 0)),
            pl.BlockSpec((d, LANES), lambda j: (0, 0)),
            pl.BlockSpec((1, LANES), lambda j: (0, 0)),
        ],
        out_specs=[rows(d), rows(d // 2), rows(LANES), rows(LANES)],
        out_shape=[
            jax.ShapeDtypeStruct((bsz, t, d), F32),
            jax.ShapeDtypeStruct((bsz, t, d // 2), jnp.uint32),
            jax.ShapeDtypeStruct((bsz, t, LANES), jnp.int32),
            jax.ShapeDtypeStruct((bsz, t, LANES), F32),
        ],
        compiler_params=_cparams(("arbitrary",)),
        name="outproj",
    )(mixc.reshape(bsz, t, -1), og.reshape(2, bsz, t, -1), orr.reshape(2, bsz, t, -1), zg.reshape(bsz, t, -1),
      zr.reshape(bsz, t, -1), gng, rng, wout, x.reshape(bsz, t, d), mods, g2, wr, br)
    return [o.reshape(n, -1) for o in outs]


def _sc_gather(table, idx):
    n_idx = idx.shape[0]
    width = table.shape[1]
    assert n_idx % (SC_WINDOW * SC_WORKERS) == 0
    per = n_idx // SC_WORKERS
    n_steps = per // SC_WINDOW
    assert n_steps % 2 == 0
    per_pad = (per + LANES - 1) // LANES * LANES
    idx_w = jnp.pad(idx.reshape(SC_WORKERS, per), ((0, 0), (0, per_pad - per)))
    mesh = plsc.VectorSubcoreMesh(core_axis_name="core", subcore_axis_name="subcore")
    n_cores = SC_WORKERS // 16

    @functools.partial(
        pl.kernel,
        out_type=jax.ShapeDtypeStruct((n_idx, width), table.dtype),
        mesh=mesh,
        scratch_types=[
            pltpu.VMEM((per_pad,), jnp.int32),
            pltpu.VMEM((2, SC_WINDOW, width), table.dtype),
            pltpu.SemaphoreType.DMA((2,)),
            pltpu.SemaphoreType.DMA((2,)),
        ],
    )
    def gather_kernel(table_hbm, idx_hbm, out_hbm, idx_v, rows_v, sem_g, sem_w):
        wid = lax.axis_index("subcore") * n_cores + lax.axis_index("core")
        base = wid * per
        pltpu.sync_copy(idx_hbm.at[wid], idx_v)

        def gather(step, buf):
            return pltpu.make_async_copy(table_hbm.at[idx_v.at[pl.ds(step * SC_WINDOW, SC_WINDOW)]],
                                         rows_v.at[buf], sem_g.at[buf])

        def write(step, buf):
            return pltpu.make_async_copy(rows_v.at[buf], out_hbm.at[pl.ds(base + step * SC_WINDOW, SC_WINDOW)],
                                         sem_w.at[buf])

        gather(0, 0).start()

        @pl.loop(0, n_steps, step=2)
        def _(s):
            for buf in range(2):
                step = s + buf
                other = 1 - buf
                gather(step, buf).wait()
                write(step, buf).start()

                @pl.when(step >= 1)
                def _():
                    write(step - 1, other).wait()

                @pl.when(step + 1 < n_steps)
                def _():
                    gather(step + 1, other).start()

        write(n_steps - 1, 1).wait()

    return gather_kernel(table, idx_w)


def _expert_up_kernel(blk_e_ref, nvalid_ref, x_ref, w1_ref, w3_ref, h_ref):
    @pl.when(nvalid_ref[pl.program_id(0)] > 0)
    def _compute():
        x = _unpack_pairs(x_ref[...]).astype(BF16)
        h1 = _dot(x, w1_ref[0, 0].astype(BF16))
        h3 = _dot(x, w3_ref[0, 0].astype(BF16))
        h_ref[...] = (_silu(h1) * h3).astype(BF16)


def _expert_down_kernel(blk_e_ref, nvalid_ref, h_ref, w2_ref, y_ref):
    @pl.when(nvalid_ref[pl.program_id(0)] > 0)
    def _compute():
        y_ref[...] = _pack_pairs(_dot(h_ref[...], w2_ref[0, 0].astype(BF16)))


def _experts_call(blk_e, nvalid, xs, w1, w3, w2, l):
    n_slots = xs.shape[0]
    d = w1.shape[-2]
    nb = n_slots // MOE_TB
    hidden = w1.shape[-1]
    up = pl.pallas_call(
        _expert_up_kernel,
        grid_spec=pltpu.PrefetchScalarGridSpec(
            num_scalar_prefetch=2,
            grid=(nb,),
            in_specs=[
                pl.BlockSpec((MOE_TB, d // 2), lambda i, be, nv: (i, 0)),
                pl.BlockSpec((1, 1, d, hidden), lambda i, be, nv: (l, be[i], 0, 0)),
                pl.BlockSpec((1, 1, d, hidden), lambda i, be, nv: (l, be[i], 0, 0)),
            ],
            out_specs=pl.BlockSpec((MOE_TB, hidden), lambda i, be, nv: (i, 0)),
        ),
        out_shape=jax.ShapeDtypeStruct((n_slots, hidden), BF16),
        compiler_params=_cparams(("arbitrary",)),
        name="expert_up",
    )(blk_e, nvalid, xs, w1, w3)
    return pl.pallas_call(
        _expert_down_kernel,
        grid_spec=pltpu.PrefetchScalarGridSpec(
            num_scalar_prefetch=2,
            grid=(nb,),
            in_specs=[
                pl.BlockSpec((MOE_TB, hidden), lambda i, be, nv: (i, 0)),
                pl.BlockSpec((1, 1, hidden, d), lambda i, be, nv: (l, be[i], 0, 0)),
            ],
            out_specs=pl.BlockSpec((MOE_TB, d // 2), lambda i, be, nv: (i, 0)),
        ),
        out_shape=jax.ShapeDtypeStruct((n_slots, d // 2), jnp.uint32),
        compiler_params=_cparams(("arbitrary",)),
        name="expert_down",
    )(blk_e, nvalid, up, w2)


def _slot_plan(idx):
    n = idx.shape[0]
    n_asg = n * TOP_K
    flat_e = idx[:, :TOP_K].reshape(n_asg)
    order = jnp.argsort(flat_e).astype(jnp.int32)
    experts = jnp.arange(N_EXPERTS, dtype=jnp.int32)
    counts = jnp.sum(flat_e[:, None] == experts[None, :], axis=0, dtype=jnp.int32)
    padded = (counts + MOE_TB - 1) // MOE_TB * MOE_TB
    pad_end = jnp.cumsum(padded)
    pad_start = pad_end - padded
    start = jnp.cumsum(counts) - counts
    n_slots = (n_asg + MOE_TB - 1) // MOE_TB * MOE_TB + N_EXPERTS * MOE_TB
    nb = n_slots // MOE_TB
    blk0 = jnp.arange(nb, dtype=jnp.int32) * MOE_TB
    blk_e = jnp.minimum(jnp.sum(blk0[:, None] >= pad_end[None, :], axis=1, dtype=jnp.int32), N_EXPERTS - 1)
    sel = (blk_e[:, None] == experts[None, :]).astype(jnp.int32)
    blk_cnt = jnp.sum(sel * counts[None, :], axis=1)
    blk_pad0 = jnp.sum(sel * pad_start[None, :], axis=1)
    blk_start = jnp.sum(sel * start[None, :], axis=1)
    nvalid = jnp.clip(blk_cnt - (blk0 - blk_pad0), 0, MOE_TB).astype(jnp.int32)
    blk_w = jnp.where(nvalid > 0, blk_e, jnp.max(jnp.where(nvalid > 0, blk_e, 0)))
    within = jnp.arange(MOE_TB, dtype=jnp.int32)[None, :]
    valid = within < nvalid[:, None]
    pos = jnp.clip((blk_start + blk0 - blk_pad0)[:, None] + within, 0, n_asg - 1)
    asg = order[pos.reshape(n_slots)]
    filler = jnp.arange(n_slots, dtype=jnp.int32) % n
    slot_tok = jnp.where(valid.reshape(n_slots), lax.shift_right_logical(asg, 1), filler).astype(jnp.int32)
    rank = jnp.argsort(order).astype(jnp.int32)
    sel_a = (flat_e[:, None] == experts[None, :]).astype(jnp.int32)
    dest = rank + jnp.sum(sel_a * (pad_start - start)[None, :], axis=1)
    dest_ct = dest.reshape(n, TOP_K).T.reshape(n_asg)
    return blk_w, nvalid, slot_tok, dest_ct


def _final_kernel(x_ref, y_ref, gate_ref, mod_ref, g_ref, o_ref):
    x = _combined(x_ref, y_ref, gate_ref, mod_ref)
    o_ref[0] = x * lax.rsqrt(jnp.mean(x * x, axis=-1, keepdims=True) + EPS) * g_ref[...]


def _final_call(x, y, gate, mods, gf, bsz, tpb):
    n, d = x.shape
    lat = tpb - 1
    return pl.pallas_call(
        _final_kernel,
        grid=(bsz, lat),
        in_specs=[
            pl.BlockSpec((TM, d), lambda b, j: (b * tpb + j + 1, 0)),
            pl.BlockSpec((TOP_K, TM, d // 2), lambda b, j: (0, b * tpb + j + 1, 0)),
            pl.BlockSpec((TM, LANES), lambda b, j: (b * tpb + j + 1, 0)),
            pl.BlockSpec((1, N_MOD, d), lambda b, j: (b, 0, 0)),
            pl.BlockSpec((1, d), lambda b, j: (0, 0)),
        ],
        out_specs=pl.BlockSpec((1, TM, d), lambda b, j: (b, j, 0)),
        out_shape=jax.ShapeDtypeStruct((bsz, lat * TM, d), F32),
        compiler_params=_cparams(("arbitrary", "arbitrary")),
        name="final_norm",
    )(x, y, gate, mods, gf)


def _rope_tables(seq):
    n_freq = HEAD_DK // 4
    t = jnp.arange(seq)
    inv = ROPE_BASE ** (-jnp.arange(n_freq, dtype=F32) / n_freq)
    ang = jnp.concatenate([(t // GRID_W).astype(F32)[:, None] * inv, (t % GRID_W).astype(F32)[:, None] * inv], axis=-1)
    cos = jnp.concatenate([jnp.ones((TM, HEAD_DK // 2), F32), jnp.cos(ang)], axis=0)
    sin = jnp.concatenate([jnp.zeros((TM, HEAD_DK // 2), F32), jnp.sin(ang)], axis=0)
    cos_t = jnp.tile(jnp.concatenate([cos, cos], axis=-1), (1, N_HEADS))
    sin_t = jnp.tile(jnp.concatenate([-sin, sin], axis=-1), (1, N_HEADS))
    return cos_t, sin_t


def kernel(x, c, ctx, c_ctx, norm1_g, norm2_g, ada_w, ada_b, w_in, cf_dw, cf_b, cf_ln_g, cf_ln_b, sc_dw, gla_w2,
           gla_b2, gla_ng, ret_logit, ret_ng, w_out, w_grp, b_grp, w_rt, b_rt, e_w1, e_w3, e_w2, final_g):
    bsz, seq, d = x.shape
    depth = w_in.shape[0]
    assert d == D_MODEL and ctx.shape[1] == TM and seq % TM == 0 and bsz == 2 and TOP_K == 2
    assert w_in.shape[-1] == IN_W
    tpb = 1 + seq // TM
    n = bsz * tpb * TM
    nk = N_HEADS * HEAD_DK

    s8 = jnp.concatenate([c, c_ctx[None, :], jnp.zeros((8 - bsz - 1, d), F32)], axis=0)
    mods_all = _ada_call(s8, ada_w, ada_b)[:, :bsz + 1, :].reshape(depth, bsz + 1, N_MOD, d)

    w_in_p = _pack_w_in(w_in)
    w_out_b = _cast_w_out(w_out)
    cos_t, sin_t = _rope_tables(seq)
    w2pad = jnp.zeros((depth, 2, LANES, nk), F32)
    w2pad = w2pad.at[:, 0, 0:GLA_RANK, :].set(gla_w2[:, 0]).at[:, 1, GLA_RANK:2 * GLA_RANK, :].set(gla_w2[:, 1])
    wr_all = jnp.concatenate([w_rt, w_grp, jnp.zeros((depth, d, LANES - N_EXPERTS - N_GROUPS), F32)], axis=-1)
    br_all = jnp.concatenate([b_rt, b_grp, jnp.zeros((depth, LANES - N_EXPERTS - N_GROUPS), F32)], axis=-1)

    out = None
    stream = (ctx.reshape(bsz * TM, d), x.reshape(bsz * seq, d))
    for l in range(depth):
        mods = mods_all[l]
        zc, zg, zr, xa = _inproj_call(stream, mods, norm1_g[l][None, :], w_in_p, l, tpb, n)
        mixc = _conv_call(zc, cf_dw[l], cf_b[l][None, :], cf_ln_g[l][None, :], cf_ln_b[l][None, :], sc_dw[l], tpb)
        og = _gla_call(zg, w2pad[l], gla_b2[l][:, None, :], bsz, tpb)
        lg = jnp.broadcast_to(ret_logit[l][:, :, None], (2, N_HEADS, LANES))
        orr = _ret_call(zr, cos_t, sin_t, lg, bsz, tpb)
        xa, h2, idx, gate = _outproj_call(mixc, og, orr, zg, zr, gla_ng[l][None, :], ret_ng[l][None, :], w_out_b, l,
                                          xa, mods, norm2_g[l][None, :], wr_all[l], br_all[l][None, :], bsz, tpb)
        blk_e, nvalid, slot_tok, dest_ct = _slot_plan(idx)
        xs = _sc_gather(h2, slot_tok)
        ys = _experts_call(blk_e, nvalid, xs, e_w1, e_w3, e_w2, l)
        y = _sc_gather(ys, dest_ct).reshape(TOP_K, n, d // 2)
        if l == depth - 1:
            out = _final_call(xa, y, gate, mods, final_g[None, :], bsz, tpb)
        else:
            stream = (xa, y, gate, mods)
    return out
```

```python
import functools

import jax
import jax.numpy as jnp
from jax import lax
from jax.experimental import pallas as pl
from jax.experimental.pallas import tpu as pltpu
from jax.experimental.pallas import tpu_sc as plsc

F32 = jnp.float32
BF16 = jnp.bfloat16

D_MODEL = 2048
GRID_W = 64
GROUP_W = D_MODEL // 4
CF_KERNEL = 31
SC_KERNEL = 3
N_HEADS = 4
HEAD_DK = 64
HEAD_DV = 128
GLA_RANK = 16
GLA_TAU = 16.0
GLA_CHUNK = 128
ROPE_BASE = 10000.0
N_GROUPS = 4
EXPERTS_PER_GROUP = 4
N_EXPERTS = N_GROUPS * EXPERTS_PER_GROUP
TOP_K = 2
EXPERT_HIDDEN = D_MODEL // 2
N_MOD = 6
EPS = 1e-6

TM = 256
LANES = 128
SUBLANES = 8
ADA_TN = 1024
MOE_TB = 384
VMEM_LIMIT = 56 * 1024 * 1024
SC_WORKERS = 32
SC_WINDOW = 16

ZC_W = 5 * GROUP_W
ZG_W = 3 * GROUP_W + LANES
ZR_W = 3 * GROUP_W
Z_W = ZC_W + ZG_W + ZR_W
IN_W = Z_W - (LANES - 2 * GLA_RANK)
GLR_END = ZC_W + 3 * GROUP_W + 2 * GLA_RANK
PACK_MOVES = (
    (3 * GROUP_W, 0, 2 * GROUP_W),
    (2 * GROUP_W, 2 * GROUP_W, GROUP_W),
    (0, 3 * GROUP_W, 2 * GROUP_W),
    (ZC_W, ZC_W, 3 * GROUP_W),
    (GLR_END, ZC_W + ZG_W, ZR_W),
)


def _cparams(sem):
    return pltpu.CompilerParams(dimension_semantics=sem, vmem_limit_bytes=VMEM_LIMIT)


def _sigmoid(x):
    return 1.0 / (1.0 + jnp.exp(-x))


def _silu(x):
    return x * _sigmoid(x)


def _log_sigmoid(x):
    return jnp.minimum(x, 0.0) - jnp.log1p(jnp.exp(-jnp.abs(x)))


def _dot(a, b):
    return jnp.dot(a, b, preferred_element_type=F32)


def _dot_nt(a, b):
    return lax.dot_general(a, b, (((1,), (1,)), ((), ())), preferred_element_type=F32)


def _dot_tn(a, b):
    return lax.dot_general(a, b, (((0,), (0,)), ((), ())), preferred_element_type=F32)


def _pack_pairs(x):
    w = x.shape[1] // 2
    xb = x.astype(BF16).astype(F32)
    hi = pltpu.bitcast(xb[:, :w], jnp.uint32)
    lo = pltpu.bitcast(xb[:, w:], jnp.uint32)
    return hi | lax.shift_right_logical(lo, jnp.uint32(16))


def _unpack_pairs(p):
    hi = pltpu.bitcast(p & jnp.uint32(0xFFFF0000), F32)
    lo = pltpu.bitcast(lax.shift_left(p, jnp.uint32(16)), F32)
    return jnp.concatenate([hi, lo], axis=1)


def _mod_row(i, tpb):
    return jnp.where(i % tpb == 0, 2, i // tpb)


def _pack_kernel(wt_ref, o_ref):
    def put(src, dst, width):
        for c in range(0, width, TM):
            step = min(TM, width - c)
            o_ref[0, :, dst + c:dst + c + step] = wt_ref[0, src + c:src + c + step, :].T.astype(BF16)

    for src, dst, width in PACK_MOVES:
        put(src, dst, width)
    glr0 = GLR_END - 2 * GLA_RANK
    tile = wt_ref[0, glr0:glr0 + LANES, :].T
    lane = lax.broadcasted_iota(jnp.int32, tile.shape, 1)
    o_ref[0, :, glr0:glr0 + LANES] = jnp.where(lane < 2 * GLA_RANK, tile, 0.0).astype(BF16)


def _pack_w_in(w_in):
    depth, d, _ = w_in.shape
    return pl.pallas_call(
        _pack_kernel,
        grid=(depth, d // TM),
        in_specs=[pl.BlockSpec((1, IN_W, TM), lambda l, i: (l, 0, i))],
        out_specs=pl.BlockSpec((1, TM, Z_W), lambda l, i: (l, i, 0)),
        out_shape=jax.ShapeDtypeStruct((depth, d, Z_W), BF16),
        compiler_params=_cparams(("arbitrary", "arbitrary")),
        name="pack_w_in",
    )(jnp.swapaxes(w_in, 1, 2))


def _cast_kernel(w_ref, o_ref):
    o_ref[...] = w_ref[...].astype(BF16)


def _cast_w_out(w_out):
    depth, k, d = w_out.shape
    return pl.pallas_call(
        _cast_kernel,
        grid=(depth, k // TM),
        in_specs=[pl.BlockSpec((1, TM, d), lambda l, i: (l, i, 0))],
        out_specs=pl.BlockSpec((1, TM, d), lambda l, i: (l, i, 0)),
        out_shape=jax.ShapeDtypeStruct((depth, k, d), BF16),
        compiler_params=_cparams(("arbitrary", "arbitrary")),
        name="cast_w_out",
    )(w_out)


def _ada_kernel(s_ref, w_ref, b_ref, o_ref):
    a = _silu(s_ref[...]).astype(BF16)
    o_ref[0] = _dot(a, w_ref[0].astype(BF16)) + b_ref[0]


def _ada_call(s8, ada_w, ada_b):
    depth, d, nm = ada_w.shape
    return pl.pallas_call(
        _ada_kernel,
        grid=(depth, nm // ADA_TN),
        in_specs=[
            pl.BlockSpec((8, d), lambda l, j: (0, 0)),
            pl.BlockSpec((1, d, ADA_TN), lambda l, j: (l, 0, j)),
            pl.BlockSpec((1, 1, ADA_TN), lambda l, j: (l, 0, j)),
        ],
        out_specs=pl.BlockSpec((1, 8, ADA_TN), lambda l, j: (l, 0, j)),
        out_shape=jax.ShapeDtypeStruct((depth, 8, nm), F32),
        compiler_params=_cparams(("arbitrary", "arbitrary")),
        name="adaln",
    )(s8, ada_w, ada_b.reshape(depth, 1, nm))


def _combined(x_ref, y_ref, gate_ref, mod_ref):
    f = gate_ref[:, 0:1] * _unpack_pairs(y_ref[0]) + gate_ref[:, 1:2] * _unpack_pairs(y_ref[1])
    return x_ref[...] + mod_ref[0, 5:6, :] * f


def _inproj_kernel(*refs, first, tpb):
    if first:
        ctx_ref, lat_ref, mod_ref, g_ref, w_ref, zc_ref, zg_ref, zr_ref, xo_ref = refs
        x = jnp.where(pl.program_id(0) % tpb == 0, ctx_ref[...], lat_ref[...])
    else:
        x_ref, y_ref, gate_ref, pmod_ref, mod_ref, g_ref, w_ref, zc_ref, zg_ref, zr_ref, xo_ref = refs
        x = _combined(x_ref, y_ref, gate_ref, pmod_ref)
    xo_ref[...] = x
    y = x * lax.rsqrt(jnp.mean(x * x, axis=-1, keepdims=True) + EPS) * g_ref[...]
    h = (y * (1.0 + mod_ref[0, 1:2, :]) + mod_ref[0, 0:1, :]).astype(BF16)
    off = 0
    for ref, width in ((zc_ref, ZC_W), (zg_ref, ZG_W), (zr_ref, ZR_W)):
        c = 0
        while c < width:
            step = min(512, width - c)
            ref[:, c:c + step] = _dot(h, w_ref[0, :, off + c:off + c + step]).astype(BF16)
            c += step
        off += width


def _inproj_call(stream, mods, g1, w_packed, l, tpb, n):
    d = D_MODEL
    nt = n // TM
    nlat = tpb - 1
    mod_spec = pl.BlockSpec((1, N_MOD, d), lambda i: (_mod_row(i, tpb), 0, 0))
    first = len(stream) == 2
    if first:
        in_specs = [
            pl.BlockSpec((TM, d), lambda i: (i // tpb, 0)),
            pl.BlockSpec((TM, d), lambda i: ((i // tpb) * nlat + jnp.maximum(i % tpb - 1, 0), 0)),
        ]
    else:
        in_specs = [
            pl.BlockSpec((TM, d), lambda i: (i, 0)),
            pl.BlockSpec((TOP_K, TM, d // 2), lambda i: (0, i, 0)),
            pl.BlockSpec((TM, LANES), lambda i: (i, 0)),
            mod_spec,
        ]
    in_specs += [
        mod_spec,
        pl.BlockSpec((1, d), lambda i: (0, 0)),
        pl.BlockSpec((1, d, Z_W), lambda i: (l, 0, 0), pipeline_mode=pl.Buffered(1)),
    ]
    return pl.pallas_call(
        functools.partial(_inproj_kernel, first=first, tpb=tpb),
        grid=(nt,),
        in_specs=in_specs,
        out_specs=[
            pl.BlockSpec((TM, ZC_W), lambda i: (i, 0)),
            pl.BlockSpec((TM, ZG_W), lambda i: (i, 0)),
            pl.BlockSpec((TM, ZR_W), lambda i: (i, 0)),
            pl.BlockSpec((TM, d), lambda i: (i, 0)),
        ],
        out_shape=[
            jax.ShapeDtypeStruct((n, ZC_W), BF16),
            jax.ShapeDtypeStruct((n, ZG_W), BF16),
            jax.ShapeDtypeStruct((n, ZR_W), BF16),
            jax.ShapeDtypeStruct((n, d), F32),
        ],
        compiler_params=_cparams(("arbitrary",)),
        name="inproj",
    )(*stream, mods, g1, w_packed)


PAD_LEAD = 16
SEG = GRID_W
LAT_STRIDE = SEG + PAD_LEAD
PAD_ROWS = (TM // SEG) * LAT_STRIDE + PAD_LEAD


def _conv_kernel(zc_ref, prev_ref, next_ref, cfw_ref, cfb_ref, lng_ref, lnb_ref, scw_ref, o_ref, pad_ref, shift_ref,
                 *, tpb):
    j = pl.program_id(0) % tpb
    nseg = TM // SEG
    half = CF_KERNEL // 2
    zeros_lead = jnp.zeros((PAD_LEAD, GROUP_W), F32)

    def glu():
        cfa = zc_ref[:, 3 * GROUP_W:4 * GROUP_W].astype(F32)
        cfg = zc_ref[:, 4 * GROUP_W:5 * GROUP_W].astype(F32)
        return cfa * _sigmoid(cfg)

    def finish_cf(acc, s):
        y = acc + cfb_ref[...]
        yc = y - jnp.mean(y, axis=-1, keepdims=True)
        yn = yc * lax.rsqrt(jnp.mean(yc * yc, axis=-1, keepdims=True) + EPS)
        o_ref[s * SEG:(s + 1) * SEG, 0:GROUP_W] = _silu(yn * lng_ref[...] + lnb_ref[...]).astype(BF16)

    def conformer(stride):
        span = PAD_ROWS - SUBLANES
        for r in range(1, SUBLANES):
            shift_ref[r, 0:span, :] = pad_ref[r:r + span, :]
        for s in range(nseg):
            base = s * stride + PAD_LEAD - half
            acc = jnp.zeros((SEG, GROUP_W), F32)
            for k in range(CF_KERNEL):
                r = (base + k) % SUBLANES
                a = base + k - r
                win = pad_ref[a:a + SEG, :] if r == 0 else shift_ref[r, a:a + SEG, :]
                acc = acc + cfw_ref[k:k + 1, :] * win
            finish_cf(acc, s)

    def sc_products():
        scc = zc_ref[:, 0:GROUP_W].astype(F32)
        scv = zc_ref[:, GROUP_W:2 * GROUP_W].astype(F32)
        scb = zc_ref[:, 2 * GROUP_W:3 * GROUP_W].astype(F32)
        return scc * scv, scb

    @pl.when(j != 0)
    def _latent():
        u = glu()
        for s in range(nseg):
            pad_ref[s * LAT_STRIDE:s * LAT_STRIDE + PAD_LEAD, :] = zeros_lead
            pad_ref[s * LAT_STRIDE + PAD_LEAD:(s + 1) * LAT_STRIDE, :] = u[s * SEG:(s + 1) * SEG]
        pad_ref[nseg * LAT_STRIDE:nseg * LAT_STRIDE + PAD_LEAD, :] = zeros_lead
        conformer(LAT_STRIDE)
        usc, scb = sc_products()
        up = prev_ref[:, 0:GROUP_W].astype(F32) * prev_ref[:, GROUP_W:2 * GROUP_W].astype(F32)
        un = next_ref[:, 0:GROUP_W].astype(F32) * next_ref[:, GROUP_W:2 * GROUP_W].astype(F32)
        up = jnp.where(j == 1, 0.0, up)
        un = jnp.where(j == tpb - 1, 0.0, un)
        above = jnp.concatenate([up, usc[:TM - GRID_W]], axis=0)
        below = jnp.concatenate([usc[GRID_W:], un], axis=0)
        y = scb * (scw_ref[0:1, :] * above + scw_ref[1:2, :] * usc + scw_ref[2:3, :] * below)
        o_ref[:, GROUP_W:2 * GROUP_W] = y.astype(BF16)

    @pl.when(j == 0)
    def _context():
        u = glu()
        pad_ref[0:PAD_LEAD, :] = zeros_lead
        pad_ref[PAD_LEAD:PAD_LEAD + TM, :] = u
        pad_ref[PAD_LEAD + TM:PAD_ROWS, :] = jnp.zeros((PAD_ROWS - PAD_LEAD - TM, GROUP_W), F32)
        conformer(SEG)
        usc, scb = sc_products()
        pad_ref[PAD_LEAD:PAD_LEAD + TM, :] = usc
        before = pad_ref[PAD_LEAD - 1:PAD_LEAD - 1 + TM, :]
        after = pad_ref[PAD_LEAD + 1:PAD_LEAD + 1 + TM, :]
        y = scb * (scw_ref[0:1, :] * before + scw_ref[1:2, :] * usc + scw_ref[2:3, :] * after)
        o_ref[:, GROUP_W:2 * GROUP_W] = y.astype(BF16)


def _conv_call(zc, cfw, cfb, lng, lnb, scw, tpb):
    n = zc.shape[0]
    nt = n // TM
    r = TM // GRID_W
    nhalo = n // GRID_W
    return pl.pallas_call(
        functools.partial(_conv_kernel, tpb=tpb),
        grid=(nt,),
        in_specs=[
            pl.BlockSpec((TM, ZC_W), lambda i: (i, 0)),
            pl.BlockSpec((GRID_W, 2 * GROUP_W), lambda i: (jnp.maximum(i * r - 1, 0), 0)),
            pl.BlockSpec((GRID_W, 2 * GROUP_W), lambda i: (jnp.minimum(i * r + r, nhalo - 1), 0)),
            pl.BlockSpec((CF_KERNEL, GROUP_W), lambda i: (0, 0)),
            pl.BlockSpec((1, GROUP_W), lambda i: (0, 0)),
            pl.BlockSpec((1, GROUP_W), lambda i: (0, 0)),
            pl.BlockSpec((1, GROUP_W), lambda i: (0, 0)),
            pl.BlockSpec((SC_KERNEL, GROUP_W), lambda i: (0, 0)),
        ],
        out_specs=pl.BlockSpec((TM, 2 * GROUP_W), lambda i: (i, 0)),
        out_shape=jax.ShapeDtypeStruct((n, 2 * GROUP_W), BF16),
        scratch_shapes=[pltpu.VMEM((PAD_ROWS, GROUP_W), F32), pltpu.VMEM((SUBLANES, PAD_ROWS, GROUP_W), F32)],
        compiler_params=_cparams(("arbitrary",)),
        name="convmix",
    )(zc, zc, zc, cfw, cfb, lng, lnb, scw)


def _scan_block(d, j, tpb):
    return jnp.where(d == 0, j, jnp.where(j == 0, 0, tpb - j))


def _tri(rev, size):
    ri = lax.broadcasted_iota(jnp.int32, (size, size), 0)
    ci = lax.broadcasted_iota(jnp.int32, (size, size), 1)
    return (ri <= ci) if rev else (ri >= ci)


def _gla_kernel(zg_ref, w2_ref, b2_ref, o_ref, st_ref):
    d = pl.program_id(0)

    @pl.when(pl.program_id(1) == 0)
    def _init():
        st_ref[...] = jnp.zeros_like(st_ref)

    def run(rev):
        nk = N_HEADS * HEAD_DK
        n_b = zg_ref.shape[0]
        tri = _tri(rev, GLA_CHUNK)
        trib = tri.astype(BF16)
        order = range(TM // GLA_CHUNK - 1, -1, -1) if rev else range(TM // GLA_CHUNK)
        w2b = w2_ref[0].astype(BF16)
        la = []
        for b in range(n_b):
            zz = _dot(zg_ref[b, :, 3 * GROUP_W:3 * GROUP_W + LANES], w2b) + b2_ref[0]
            la.append(_log_sigmoid(zz) * (1.0 / GLA_TAU))
        pre = {}
        for b in range(n_b):
            for c in order:
                rows = slice(c * GLA_CHUNK, (c + 1) * GLA_CHUNK)
                la_c = la[b][rows]
                hi = la_c.astype(BF16)
                lo = (la_c - hi.astype(F32)).astype(BF16)
                bc = _dot(trib, hi) + _dot(trib, lo)
                b_last = bc[0:1] if rev else bc[GLA_CHUNK - 1:GLA_CHUNK]
                b_mid = bc[GLA_CHUNK // 2:GLA_CHUNK // 2 + 1]
                q = zg_ref[b, rows, 0:nk].astype(F32) * (HEAD_DK ** -0.5)
                k = zg_ref[b, rows, nk:2 * nk].astype(F32)
                pre[b, c] = dict(
                    q_in=(q * jnp.exp(bc)).astype(BF16),
                    q_mid=(q * jnp.exp(bc - b_mid)).astype(BF16),
                    k_mid=(k * jnp.exp(b_mid - bc)).astype(BF16),
                    k_st=(k * jnp.exp(b_last - bc)).astype(BF16),
                    a_row=jnp.exp(b_last),
                )
        scores = {}
        for b in range(n_b):
            for c in order:
                p = pre[b, c]
                for h in range(N_HEADS):
                    hs = slice(h * HEAD_DK, (h + 1) * HEAD_DK)
                    scores[b, c, h] = jnp.where(tri, _dot_nt(p["q_mid"][:, hs], p["k_mid"][:, hs]), 0.0).astype(BF16)
        for b in range(n_b):
            states = [st_ref[b, h] for h in range(N_HEADS)]
            for c in order:
                rows = slice(c * GLA_CHUNK, (c + 1) * GLA_CHUNK)
                p = pre[b, c]
                for h in range(N_HEADS):
                    hs = slice(h * HEAD_DK, (h + 1) * HEAD_DK)
                    vs = slice(h * HEAD_DV, (h + 1) * HEAD_DV)
                    vh = zg_ref[b, rows, GROUP_W + h * HEAD_DV:GROUP_W + (h + 1) * HEAD_DV]
                    st = states[h]
                    o_ref[0, b, rows, vs] = _dot(scores[b, c, h], vh) + _dot_nt(p["q_in"][:, hs], st.astype(BF16))
                    states[h] = st * p["a_row"][:, hs] + _dot_tn(vh, p["k_st"][:, hs])
            for h in range(N_HEADS):
                st_ref[b, h] = states[h]

    @pl.when(d == 0)
    def _fwd():
        run(False)

    @pl.when(d == 1)
    def _bwd():
        run(True)


def _gla_call(zg, w2pad, b2, bsz, tpb):
    n = zg.shape[0]
    nk = N_HEADS * HEAD_DK
    return pl.pallas_call(
        _gla_kernel,
        grid=(2, tpb),
        in_specs=[
            pl.BlockSpec((bsz, TM, ZG_W), lambda d, j: (0, _scan_block(d, j, tpb), 0)),
            pl.BlockSpec((1, LANES, nk), lambda d, j: (d, 0, 0)),
            pl.BlockSpec((1, 1, nk), lambda d, j: (d, 0, 0)),
        ],
        out_specs=pl.BlockSpec((1, bsz, TM, GROUP_W), lambda d, j: (d, 0, _scan_block(d, j, tpb), 0)),
        out_shape=jax.ShapeDtypeStruct((2, bsz, n // bsz, GROUP_W), F32),
        scratch_shapes=[pltpu.VMEM((bsz, N_HEADS, HEAD_DV, HEAD_DK), F32)],
        compiler_params=_cparams(("arbitrary", "arbitrary")),
        name="gla",
    )(zg.reshape(bsz, n // bsz, ZG_W), w2pad, b2).reshape(2, n, GROUP_W)


def _ret_kernel(zr_ref, cos_ref, sin_ref, lg_ref, o_ref, st_ref):
    d = pl.program_id(0)

    @pl.when(pl.program_id(1) == 0)
    def _init():
        st_ref[...] = jnp.zeros_like(st_ref)

    def run(rev):
        nk = N_HEADS * HEAD_DK
        batches = range(zr_ref.shape[0])
        lane = lax.broadcasted_iota(jnp.int32, (TM, nk), 1)
        first_half = (lane % HEAD_DK) < (HEAD_DK // 2)
        cosf = cos_ref[...]
        sinf = sin_ref[...]

        def rope(x):
            swapped = jnp.where(first_half, pltpu.roll(x, nk - HEAD_DK // 2, 1), pltpu.roll(x, HEAD_DK // 2, 1))
            return x * cosf + swapped * sinf

        qs = [rope(zr_ref[b, :, 0:nk].astype(F32)).astype(BF16) for b in batches]
        ks = [rope(zr_ref[b, :, nk:2 * nk].astype(F32)) * (HEAD_DK ** -0.5) for b in batches]
        lg = _log_sigmoid(lg_ref[0])
        tri = _tri(rev, TM)
        ri = lax.broadcasted_iota(jnp.int32, (TM, TM), 0)
        ci = lax.broadcasted_iota(jnp.int32, (TM, TM), 1)
        rel = ((ci - ri) if rev else (ri - ci)).astype(F32)
        pos = lax.broadcasted_iota(jnp.int32, (TM, LANES), 0).astype(F32)
        q_steps = (TM - pos) if rev else (pos + 1.0)
        k_steps = pos if rev else (TM - 1.0 - pos)
        lg_wide = jnp.concatenate([lg, lg], axis=1)
        scores = {}
        for h in range(N_HEADS):
            hs = slice(h * HEAD_DK, (h + 1) * HEAD_DK)
            decay = jnp.where(tri, jnp.exp(rel * lg_wide[h:h + 1, :]), 0.0)
            for b in batches:
                scores[b, h] = (_dot_nt(qs[b][:, hs], ks[b][:, hs].astype(BF16)) * decay).astype(BF16)
        for h in range(N_HEADS):
            hs = slice(h * HEAD_DK, (h + 1) * HEAD_DK)
            vs = slice(h * HEAD_DV, (h + 1) * HEAD_DV)
            lgh = lg[h:h + 1, :]
            q_decay = jnp.exp(q_steps * lgh)
            k_decay = jnp.exp(k_steps[:, 0:HEAD_DK] * lgh[:, 0:HEAD_DK])
            chunk_decay = jnp.exp(float(TM) * lgh[:, 0:HEAD_DK])
            for b in batches:
                st = st_ref[b, h]
                vh = zr_ref[b, :, GROUP_W + h * HEAD_DV:GROUP_W + (h + 1) * HEAD_DV]
                o_ref[0, b, :, vs] = _dot(scores[b, h], vh) + _dot_nt(qs[b][:, hs], st.astype(BF16)) * q_decay
                st_ref[b, h] = st * chunk_decay + _dot_tn(vh, (ks[b][:, hs] * k_decay).astype(BF16))

    @pl.when(d == 0)
    def _fwd():
        run(False)

    @pl.when(d == 1)
    def _bwd():
        run(True)


def _ret_call(zr, cos_t, sin_t, lg, bsz, tpb):
    n = zr.shape[0]
    nk = N_HEADS * HEAD_DK
    return pl.pallas_call(
        _ret_kernel,
        grid=(2, tpb),
        in_specs=[
            pl.BlockSpec((bsz, TM, ZR_W), lambda d, j: (0, _scan_block(d, j, tpb), 0)),
            pl.BlockSpec((TM, nk), lambda d, j: (_scan_block(d, j, tpb), 0)),
            pl.BlockSpec((TM, nk), lambda d, j: (_scan_block(d, j, tpb), 0)),
            pl.BlockSpec((1, N_HEADS, LANES), lambda d, j: (d, 0, 0)),
        ],
        out_specs=pl.BlockSpec((1, bsz, TM, GROUP_W), lambda d, j: (d, 0, _scan_block(d, j, tpb), 0)),
        out_shape=jax.ShapeDtypeStruct((2, bsz, n // bsz, GROUP_W), F32),
        scratch_shapes=[pltpu.VMEM((bsz, N_HEADS, HEAD_DV, HEAD_DK), F32)],
        compiler_params=_cparams(("arbitrary", "arbitrary")),
        name="retention",
    )(zr.reshape(bsz, n // bsz, ZR_W), cos_t, sin_t, lg).reshape(2, n, GROUP_W)


def _head_norm(o, g, center):
    outs = []
    for h in range(N_HEADS):
        oh = o[:, h * HEAD_DV:(h + 1) * HEAD_DV]
        if center:
            oh = oh - jnp.mean(oh, axis=-1, keepdims=True)
        outs.append(oh * lax.rsqrt(jnp.mean(oh * oh, axis=-1, keepdims=True) + EPS))
    return jnp.concatenate(outs, axis=1) * g


def _route(logits):
    lane = lax.broadcasted_iota(jnp.int32, logits.shape, 1).astype(F32)
    neg = jnp.float32(-jnp.inf)
    big = jnp.float32(LANES)
    gmask = (lane >= N_EXPERTS) & (lane < N_EXPERTS + N_GROUPS)
    gl = jnp.where(gmask, logits, neg)
    gmax = jnp.max(gl, axis=-1, keepdims=True)
    gidx = jnp.min(jnp.where(gl == gmax, lane - N_EXPERTS, big), axis=-1, keepdims=True)
    g_w = 1.0 / jnp.sum(jnp.where(gmask, jnp.exp(logits - gmax), 0.0), axis=-1, keepdims=True)
    emask = (lane >= gidx * EXPERTS_PER_GROUP) & (lane < (gidx + 1.0) * EXPERTS_PER_GROUP)
    el = jnp.where(emask, logits, neg)
    e1 = jnp.max(el, axis=-1, keepdims=True)
    i1 = jnp.min(jnp.where(el == e1, lane, big), axis=-1, keepdims=True)
    el2 = jnp.where(lane == i1, neg, el)
    e2 = jnp.max(el2, axis=-1, keepdims=True)
    i2 = jnp.min(jnp.where(el2 == e2, lane, big), axis=-1, keepdims=True)
    r = jnp.exp(e2 - e1)
    w1 = g_w / (1.0 + r)
    w2 = g_w * r / (1.0 + r)
    idx = jnp.where(lane == 0.0, i1, jnp.where(lane == 1.0, i2, 0.0)).astype(jnp.int32)
    gate = jnp.where(lane == 0.0, w1, jnp.where(lane == 1.0, w2, 0.0))
    return idx, gate


def _outproj_kernel(mixc_ref, og_ref, or_ref, gr_ref, rg_ref, gng_ref, rng_ref, wout_ref, x_ref, mod_ref, g2_ref,
                    wr_ref, br_ref, xo_ref, h_ref, idx_ref, gate_ref):
    j = pl.program_id(0)
    w = wr_ref[...]
    w_hi = w.astype(BF16)
    w_lo = (w - w_hi.astype(F32)).astype(BF16)
    batches = range(x_ref.shape[0])
    mods = [mod_ref[jnp.where(j == 0, mod_ref.shape[0] - 1, b)] for b in batches]
    glas = [_head_norm(og_ref[0, b] + og_ref[1, b], gng_ref[...], False) * _silu(gr_ref[b].astype(F32))
            for b in batches]
    rets = [_head_norm(or_ref[0, b] + or_ref[1, b], rng_ref[...], True) * _silu(rg_ref[b].astype(F32))
            for b in batches]
    ys = [_dot(mixc_ref[b], wout_ref[0, 0:2 * GROUP_W, :])
          + _dot(glas[b].astype(BF16), wout_ref[0, 2 * GROUP_W:3 * GROUP_W, :])
          + _dot(rets[b].astype(BF16), wout_ref[0, 3 * GROUP_W:4 * GROUP_W, :]) for b in batches]
    hs = []
    for b in batches:
        x = x_ref[b] + mods[b][2:3, :] * ys[b]
        xo_ref[b] = x
        hn = x * lax.rsqrt(jnp.mean(x * x, axis=-1, keepdims=True) + EPS) * g2_ref[...]
        h = hn * (1.0 + mods[b][4:5, :]) + mods[b][3:4, :]
        h_ref[b] = _pack_pairs(h)
        hs.append(h)
    logits = []
    for b in batches:
        hb = hs[b].astype(BF16)
        h_lo = (hs[b] - hb.astype(F32)).astype(BF16)
        logits.append(_dot(hb, w_hi) + _dot(hb, w_lo) + _dot(h_lo, w_hi) + br_ref[...])
    for b in batches:
        idx, gate = _route(logits[b])
        idx_ref[b] = idx
        gate_ref[b] = gate


def _outproj_call(mixc, og, orr, zg, zr, gng, rng, wout, l, x, mods, g2, wr, br, bsz, tpb):
    n, d = x.shape
    t = n // bsz

    def rows(width, col=0):
        return pl.BlockSpec((bsz, TM, width), lambda j: (0, j, col))

    scan = pl.BlockSpec((2, bsz, TM, GROUP_W), lambda j: (0, 0, j, 0))
    outs = pl.pallas_call(
        _outproj_kernel,
        grid=(tpb,),
        in_specs=[
            rows(2 * GROUP_W),
            scan,
            scan,
            rows(GROUP_W, 2),
            rows(GROUP_W, 2),
            pl.BlockSpec((1, GROUP_W), lambda j: (0, 0)),
            pl.BlockSpec((1, GROUP_W), lambda j: (0, 0)),
            pl.BlockSpec((1, d, d), lambda j: (l, 0, 0), pipeline_mode=pl.Buffered(1)),
            rows(d),
            pl.BlockSpec((bsz + 1, N_MOD, d), lambda j: (0, 0, 0)),
            pl.BlockSpec((1, d), lambda j: (0, 0)),
            pl.BlockSpec((d, LANES), lambda j: (0, 0)),
            pl.BlockSpec((1, LANES), lambda j: (0, 0)),
        ],
        out_specs=[rows(d), rows(d // 2), rows(LANES), rows(LANES)],
        out_shape=[
            jax.ShapeDtypeStruct((bsz, t, d), F32),
            jax.ShapeDtypeStruct((bsz, t, d // 2), jnp.uint32),
            jax.ShapeDtypeStruct((bsz, t, LANES), jnp.int32),
            jax.ShapeDtypeStruct((bsz, t, LANES), F32),
        ],
        compiler_params=_cparams(("arbitrary",)),
        name="outproj",
    )(mixc.reshape(bsz, t, -1), og.reshape(2, bsz, t, -1), orr.reshape(2, bsz, t, -1), zg.reshape(bsz, t, -1),
      zr.reshape(bsz, t, -1), gng, rng, wout, x.reshape(bsz, t, d), mods, g2, wr, br)
    return [o.reshape(n, -1) for o in outs]


def _sc_gather(table, idx):
    n_idx = idx.shape[0]
    width = table.shape[1]
    assert n_idx % (SC_WINDOW * SC_WORKERS) == 0
    per = n_idx // SC_WORKERS
    n_steps = per // SC_WINDOW
    assert n_steps % 2 == 0
    per_pad = (per + LANES - 1) // LANES * LANES
    idx_w = jnp.pad(idx.reshape(SC_WORKERS, per), ((0, 0), (0, per_pad - per)))
    mesh = plsc.VectorSubcoreMesh(core_axis_name="core", subcore_axis_name="subcore")
    n_cores = SC_WORKERS // 16

    @functools.partial(
        pl.kernel,
        out_type=jax.ShapeDtypeStruct((n_idx, width), table.dtype),
        mesh=mesh,
        scratch_types=[
            pltpu.VMEM((per_pad,), jnp.int32),
            pltpu.VMEM((2, SC_WINDOW, width), table.dtype),
            pltpu.SemaphoreType.DMA((2,)),
            pltpu.SemaphoreType.DMA((2,)),
        ],
    )
    def gather_kernel(table_hbm, idx_hbm, out_hbm, idx_v, rows_v, sem_g, sem_w):
        wid = lax.axis_index("subcore") * n_cores + lax.axis_index("core")
        base = wid * per
        pltpu.sync_copy(idx_hbm.at[wid], idx_v)

        def gather(step, buf):
            return pltpu.make_async_copy(table_hbm.at[idx_v.at[pl.ds(step * SC_WINDOW, SC_WINDOW)]],
                                         rows_v.at[buf], sem_g.at[buf])

        def write(step, buf):
            return pltpu.make_async_copy(rows_v.at[buf], out_hbm.at[pl.ds(base + step * SC_WINDOW, SC_WINDOW)],
                                         sem_w.at[buf])

        gather(0, 0).start()

        @pl.loop(0, n_steps, step=2)
        def _(s):
            for buf in range(2):
                step = s + buf
                other = 1 - buf
                gather(step, buf).wait()
                write(step, buf).start()

                @pl.when(step >= 1)
                def _():
                    write(step - 1, other).wait()

                @pl.when(step + 1 < n_steps)
                def _():
                    gather(step + 1, other).start()

        write(n_steps - 1, 1).wait()

    return gather_kernel(table, idx_w)


def _expert_up_kernel(blk_e_ref, nvalid_ref, x_ref, w1_ref, w3_ref, h_ref):
    @pl.when(nvalid_ref[pl.program_id(0)] > 0)
    def _compute():
        x = _unpack_pairs(x_ref[...]).astype(BF16)
        h1 = _dot(x, w1_ref[0, 0].astype(BF16))
        h3 = _dot(x, w3_ref[0, 0].astype(BF16))
        h_ref[...] = (_silu(h1) * h3).astype(BF16)


def _expert_down_kernel(blk_e_ref, nvalid_ref, h_ref, w2_ref, y_ref):
    @pl.when(nvalid_ref[pl.program_id(0)] > 0)
    def _compute():
        y_ref[...] = _pack_pairs(_dot(h_ref[...], w2_ref[0, 0].astype(BF16)))


def _experts_call(blk_e, nvalid, xs, w1, w3, w2, l):
    n_slots = xs.shape[0]
    d = w1.shape[-2]
    nb = n_slots // MOE_TB
    hidden = w1.shape[-1]
    up = pl.pallas_call(
        _expert_up_kernel,
        grid_spec=pltpu.PrefetchScalarGridSpec(
            num_scalar_prefetch=2,
            grid=(nb,),
            in_specs=[
                pl.BlockSpec((MOE_TB, d // 2), lambda i, be, nv: (i, 0)),
                pl.BlockSpec((1, 1, d, hidden), lambda i, be, nv: (l, be[i], 0, 0)),
                pl.BlockSpec((1, 1, d, hidden), lambda i, be, nv: (l, be[i], 0, 0)),
            ],
            out_specs=pl.BlockSpec((MOE_TB, hidden), lambda i, be, nv: (i, 0)),
        ),
        out_shape=jax.ShapeDtypeStruct((n_slots, hidden), BF16),
        compiler_params=_cparams(("arbitrary",)),
        name="expert_up",
    )(blk_e, nvalid, xs, w1, w3)
    return pl.pallas_call(
        _expert_down_kernel,
        grid_spec=pltpu.PrefetchScalarGridSpec(
            num_scalar_prefetch=2,
            grid=(nb,),
            in_specs=[
                pl.BlockSpec((MOE_TB, hidden), lambda i, be, nv: (i, 0)),
                pl.BlockSpec((1, 1, hidden, d), lambda i, be, nv: (l, be[i], 0, 0)),
            ],
            out_specs=pl.BlockSpec((MOE_TB, d // 2), lambda i, be, nv: (i, 0)),
        ),
        out_shape=jax.ShapeDtypeStruct((n_slots, d // 2), jnp.uint32),
        compiler_params=_cparams(("arbitrary",)),
        name="expert_down",
    )(blk_e, nvalid, up, w2)


def _slot_plan(idx):
    n = idx.shape[0]
    n_asg = n * TOP_K
    flat_e = idx[:, :TOP_K].reshape(n_asg)
    order = jnp.argsort(flat_e).astype(jnp.int32)
    experts = jnp.arange(N_EXPERTS, dtype=jnp.int32)
    counts = jnp.sum(flat_e[:, None] == experts[None, :], axis=0, dtype=jnp.int32)
    padded = (counts + MOE_TB - 1) // MOE_TB * MOE_TB
    pad_end = jnp.cumsum(padded)
    pad_start = pad_end - padded
    start = jnp.cumsum(counts) - counts
    n_slots = (n_asg + MOE_TB - 1) // MOE_TB * MOE_TB + N_EXPERTS * MOE_TB
    nb = n_slots // MOE_TB
    blk0 = jnp.arange(nb, dtype=jnp.int32) * MOE_TB
    blk_e = jnp.minimum(jnp.sum(blk0[:, None] >= pad_end[None, :], axis=1, dtype=jnp.int32), N_EXPERTS - 1)
    sel = (blk_e[:, None] == experts[None, :]).astype(jnp.int32)
    blk_cnt = jnp.sum(sel * counts[None, :], axis=1)
    blk_pad0 = jnp.sum(sel * pad_start[None, :], axis=1)
    blk_start = jnp.sum(sel * start[None, :], axis=1)
    nvalid = jnp.clip(blk_cnt - (blk0 - blk_pad0), 0, MOE_TB).astype(jnp.int32)
    blk_w = jnp.where(nvalid > 0, blk_e, jnp.max(jnp.where(nvalid > 0, blk_e, 0)))
    within = jnp.arange(MOE_TB, dtype=jnp.int32)[None, :]
    valid = within < nvalid[:, None]
    pos = jnp.clip((blk_start + blk0 - blk_pad0)[:, None] + within, 0, n_asg - 1)
    asg = order[pos.reshape(n_slots)]
    filler = jnp.arange(n_slots, dtype=jnp.int32) % n
    slot_tok = jnp.where(valid.reshape(n_slots), lax.shift_right_logical(asg, 1), filler).astype(jnp.int32)
    rank = jnp.argsort(order).astype(jnp.int32)
    sel_a = (flat_e[:, None] == experts[None, :]).astype(jnp.int32)
    dest = rank + jnp.sum(sel_a * (pad_start - start)[None, :], axis=1)
    dest_ct = dest.reshape(n, TOP_K).T.reshape(n_asg)
    return blk_w, nvalid, slot_tok, dest_ct


def _final_kernel(x_ref, y_ref, gate_ref, mod_ref, g_ref, o_ref):
    x = _combined(x_ref, y_ref, gate_ref, mod_ref)
    o_ref[0] = x * lax.rsqrt(jnp.mean(x * x, axis=-1, keepdims=True) + EPS) * g_ref[...]


def _final_call(x, y, gate, mods, gf, bsz, tpb):
    n, d = x.shape
    lat = tpb - 1
    return pl.pallas_call(
        _final_kernel,
        grid=(bsz, lat),
        in_specs=[
            pl.BlockSpec((TM, d), lambda b, j: (b * tpb + j + 1, 0)),
            pl.BlockSpec((TOP_K, TM, d // 2), lambda b, j: (0, b * tpb + j + 1, 0)),
            pl.BlockSpec((TM, LANES), lambda b, j: (b * tpb + j + 1, 0)),
            pl.BlockSpec((1, N_MOD, d), lambda b, j: (b, 0, 0)),
            pl.BlockSpec((1, d), lambda b, j: (0, 0)),
        ],
        out_specs=pl.BlockSpec((1, TM, d), lambda b, j: (b, j, 0)),
        out_shape=jax.ShapeDtypeStruct((bsz, lat * TM, d), F32),
        compiler_params=_cparams(("arbitrary", "arbitrary")),
        name="final_norm",
    )(x, y, gate, mods, gf)


def _rope_tables(seq):
    n_freq = HEAD_DK // 4
    t = jnp.arange(seq)
    inv = ROPE_BASE ** (-jnp.arange(n_freq, dtype=F32) / n_freq)
    ang = jnp.concatenate([(t // GRID_W).astype(F32)[:, None] * inv, (t % GRID_W).astype(F32)[:, None] * inv], axis=-1)
    cos = jnp.concatenate([jnp.ones((TM, HEAD_DK // 2), F32), jnp.cos(ang)], axis=0)
    sin = jnp.concatenate([jnp.zeros((TM, HEAD_DK // 2), F32), jnp.sin(ang)], axis=0)
    cos_t = jnp.tile(jnp.concatenate([cos, cos], axis=-1), (1, N_HEADS))
    sin_t = jnp.tile(jnp.concatenate([-sin, sin], axis=-1), (1, N_HEADS))
    return cos_t, sin_t


def kernel(x, c, ctx, c_ctx, norm1_g, norm2_g, ada_w, ada_b, w_in, cf_dw, cf_b, cf_ln_g, cf_ln_b, sc_dw, gla_w2,
           gla_b2, gla_ng, ret_logit, ret_ng, w_out, w_grp, b_grp, w_rt, b_rt, e_w1, e_w3, e_w2, final_g):
    bsz, seq, d = x.shape
    depth = w_in.shape[0]
    assert d == D_MODEL and ctx.shape[1] == TM and seq % TM == 0 and bsz == 2 and TOP_K == 2
    assert w_in.shape[-1] == IN_W
    tpb = 1 + seq // TM
    n = bsz * tpb * TM
    nk = N_HEADS * HEAD_DK

    s8 = jnp.concatenate([c, c_ctx[None, :], jnp.zeros((8 - bsz - 1, d), F32)], axis=0)
    mods_all = _ada_call(s8, ada_w, ada_b)[:, :bsz + 1, :].reshape(depth, bsz + 1, N_MOD, d)

    w_in_p = _pack_w_in(w_in)
    w_out_b = _cast_w_out(w_out)
    cos_t, sin_t = _rope_tables(seq)
    w2pad = jnp.zeros((depth, 2, LANES, nk), F32)
    w2pad = w2pad.at[:, 0, 0:GLA_RANK, :].set(gla_w2[:, 0]).at[:, 1, GLA_RANK:2 * GLA_RANK, :].set(gla_w2[:, 1])
    wr_all = jnp.concatenate([w_rt, w_grp, jnp.zeros((depth, d, LANES - N_EXPERTS - N_GROUPS), F32)], axis=-1)
    br_all = jnp.concatenate([b_rt, b_grp, jnp.zeros((depth, LANES - N_EXPERTS - N_GROUPS), F32)], axis=-1)

    out = None
    stream = (ctx.reshape(bsz * TM, d), x.reshape(bsz * seq, d))
    for l in range(depth):
        mods = mods_all[l]
        zc, zg, zr, xa = _inproj_call(stream, mods, norm1_g[l][None, :], w_in_p, l, tpb, n)
        mixc = _conv_call(zc, cf_dw[l], cf_b[l][None, :], cf_ln_g[l][None, :], cf_ln_b[l][None, :], sc_dw[l], tpb)
        og = _gla_call(zg, w2pad[l], gla_b2[l][:, None, :], bsz, tpb)
        lg = jnp.broadcast_to(ret_logit[l][:, :, None], (2, N_HEADS, LANES))
        orr = _ret_call(zr, cos_t, sin_t, lg, bsz, tpb)
        xa, h2, idx, gate = _outproj_call(mixc, og, orr, zg, zr, gla_ng[l][None, :], ret_ng[l][None, :], w_out_b, l,
                                          xa, mods, norm2_g[l][None, :], wr_all[l], br_all[l][None, :], bsz, tpb)
        blk_e, nvalid, slot_tok, dest_ct = _slot_plan(idx)
        xs = _sc_gather(h2, slot_tok)
        ys = _experts_call(blk_e, nvalid, xs, e_w1, e_w3, e_w2, l)
        y = _sc_gather(ys, dest_ct).reshape(TOP_K, n, d // 2)
        if l == depth - 1:
            out = _final_call(xa, y, gate, mods, final_g[None, :], bsz, tpb)
        else:
            stream = (xa, y, gate, mods)
    return out
```

```python
import functools

import jax
import jax.numpy as jnp
from jax import lax
from jax.experimental import pallas as pl
from jax.experimental.pallas import tpu as pltpu
from jax.experimental.pallas import tpu_sc as plsc

F32 = jnp.float32
BF16 = jnp.bfloat16

D_MODEL = 2048
GRID_W = 64
GROUP_W = D_MODEL // 4
CF_KERNEL = 31
SC_KERNEL = 3
N_HEADS = 4
HEAD_DK = 64
HEAD_DV = 128
GLA_RANK = 16
GLA_TAU = 16.0
GLA_CHUNK = 128
ROPE_BASE = 10000.0
N_GROUPS = 4
EXPERTS_PER_GROUP = 4
N_EXPERTS = N_GROUPS * EXPERTS_PER_GROUP
TOP_K = 2
EXPERT_HIDDEN = D_MODEL // 2
N_MOD = 6
EPS = 1e-6

TM = 256
LANES = 128
SUBLANES = 8
ADA_TN = 1024
MOE_TB = 512
VMEM_LIMIT = 56 * 1024 * 1024
SC_WORKERS = 32
SC_WINDOW = 16

ZC_W = 5 * GROUP_W
ZG_W = 3 * GROUP_W + LANES
ZR_W = 3 * GROUP_W
Z_W = ZC_W + ZG_W + ZR_W
IN_W = Z_W - (LANES - 2 * GLA_RANK)
GLR_END = ZC_W + 3 * GROUP_W + 2 * GLA_RANK
PACK_MOVES = (
    (3 * GROUP_W, 0, 2 * GROUP_W),
    (2 * GROUP_W, 2 * GROUP_W, GROUP_W),
    (0, 3 * GROUP_W, 2 * GROUP_W),
    (ZC_W, ZC_W, 3 * GROUP_W),
    (GLR_END, ZC_W + ZG_W, ZR_W),
)


def _cparams(sem):
    return pltpu.CompilerParams(dimension_semantics=sem, vmem_limit_bytes=VMEM_LIMIT)


def _sigmoid(x):
    return 1.0 / (1.0 + jnp.exp(-x))


def _silu(x):
    return x * _sigmoid(x)


def _log_sigmoid(x):
    return jnp.minimum(x, 0.0) - jnp.log1p(jnp.exp(-jnp.abs(x)))


def _dot(a, b):
    return jnp.dot(a, b, preferred_element_type=F32)


def _dot_nt(a, b):
    return lax.dot_general(a, b, (((1,), (1,)), ((), ())), preferred_element_type=F32)


def _dot_tn(a, b):
    return lax.dot_general(a, b, (((0,), (0,)), ((), ())), preferred_element_type=F32)


def _pack_pairs(x):
    w = x.shape[1] // 2
    xb = x.astype(BF16).astype(F32)
    hi = pltpu.bitcast(xb[:, :w], jnp.uint32)
    lo = pltpu.bitcast(xb[:, w:], jnp.uint32)
    return hi | lax.shift_right_logical(lo, jnp.uint32(16))


def _unpack_pairs(p):
    hi = pltpu.bitcast(p & jnp.uint32(0xFFFF0000), F32)
    lo = pltpu.bitcast(lax.shift_left(p, jnp.uint32(16)), F32)
    return jnp.concatenate([hi, lo], axis=1)


def _mod_row(i, tpb):
    return jnp.where(i % tpb == 0, 2, i // tpb)


def _pack_kernel(wt_ref, o_ref):
    def put(src, dst, width):
        for c in range(0, width, TM):
            step = min(TM, width - c)
            o_ref[0, :, dst + c:dst + c + step] = wt_ref[0, src + c:src + c + step, :].T.astype(BF16)

    for src, dst, width in PACK_MOVES:
        put(src, dst, width)
    glr0 = GLR_END - 2 * GLA_RANK
    tile = wt_ref[0, glr0:glr0 + LANES, :].T
    lane = lax.broadcasted_iota(jnp.int32, tile.shape, 1)
    o_ref[0, :, glr0:glr0 + LANES] = jnp.where(lane < 2 * GLA_RANK, tile, 0.0).astype(BF16)


def _pack_w_in(w_in):
    depth, d, _ = w_in.shape
    return pl.pallas_call(
        _pack_kernel,
        grid=(depth, d // TM),
        in_specs=[pl.BlockSpec((1, IN_W, TM), lambda l, i: (l, 0, i))],
        out_specs=pl.BlockSpec((1, TM, Z_W), lambda l, i: (l, i, 0)),
        out_shape=jax.ShapeDtypeStruct((depth, d, Z_W), BF16),
        compiler_params=_cparams(("arbitrary", "arbitrary")),
        name="pack_w_in",
    )(jnp.swapaxes(w_in, 1, 2))


def _cast_kernel(w_ref, o_ref):
    o_ref[...] = w_ref[...].astype(BF16)


def _cast_w_out(w_out):
    depth, k, d = w_out.shape
    return pl.pallas_call(
        _cast_kernel,
        grid=(depth, k // TM),
        in_specs=[pl.BlockSpec((1, TM, d), lambda l, i: (l, i, 0))],
        out_specs=pl.BlockSpec((1, TM, d), lambda l, i: (l, i, 0)),
        out_shape=jax.ShapeDtypeStruct((depth, k, d), BF16),
        compiler_params=_cparams(("arbitrary", "arbitrary")),
        name="cast_w_out",
    )(w_out)


def _ada_kernel(s_ref, w_ref, b_ref, o_ref):
    a = _silu(s_ref[...]).astype(BF16)
    o_ref[0] = _dot(a, w_ref[0].astype(BF16)) + b_ref[0]


def _ada_call(s8, ada_w, ada_b):
    depth, d, nm = ada_w.shape
    return pl.pallas_call(
        _ada_kernel,
        grid=(depth, nm // ADA_TN),
        in_specs=[
            pl.BlockSpec((8, d), lambda l, j: (0, 0)),
            pl.BlockSpec((1, d, ADA_TN), lambda l, j: (l, 0, j)),
            pl.BlockSpec((1, 1, ADA_TN), lambda l, j: (l, 0, j)),
        ],
        out_specs=pl.BlockSpec((1, 8, ADA_TN), lambda l, j: (l, 0, j)),
        out_shape=jax.ShapeDtypeStruct((depth, 8, nm), F32),
        compiler_params=_cparams(("arbitrary", "arbitrary")),
        name="adaln",
    )(s8, ada_w, ada_b.reshape(depth, 1, nm))


def _combined(x_ref, y_ref, gate_ref, mod_ref):
    f = gate_ref[:, 0:1] * _unpack_pairs(y_ref[0]) + gate_ref[:, 1:2] * _unpack_pairs(y_ref[1])
    return x_ref[...] + mod_ref[0, 5:6, :] * f


def _inproj_kernel(*refs, first, tpb):
    if first:
        ctx_ref, lat_ref, mod_ref, g_ref, w_ref, zc_ref, zg_ref, zr_ref, xo_ref = refs
        x = jnp.where(pl.program_id(0) % tpb == 0, ctx_ref[...], lat_ref[...])
    else:
        x_ref, y_ref, gate_ref, pmod_ref, mod_ref, g_ref, w_ref, zc_ref, zg_ref, zr_ref, xo_ref = refs
        x = _combined(x_ref, y_ref, gate_ref, pmod_ref)
    xo_ref[...] = x
    y = x * lax.rsqrt(jnp.mean(x * x, axis=-1, keepdims=True) + EPS) * g_ref[...]
    h = (y * (1.0 + mod_ref[0, 1:2, :]) + mod_ref[0, 0:1, :]).astype(BF16)
    off = 0
    for ref, width in ((zc_ref, ZC_W), (zg_ref, ZG_W), (zr_ref, ZR_W)):
        c = 0
        while c < width:
            step = min(512, width - c)
            ref[:, c:c + step] = _dot(h, w_ref[0, :, off + c:off + c + step]).astype(BF16)
            c += step
        off += width


def _inproj_call(stream, mods, g1, w_packed, l, tpb, n):
    d = D_MODEL
    nt = n // TM
    nlat = tpb - 1
    mod_spec = pl.BlockSpec((1, N_MOD, d), lambda i: (_mod_row(i, tpb), 0, 0))
    first = len(stream) == 2
    if first:
        in_specs = [
            pl.BlockSpec((TM, d), lambda i: (i // tpb, 0)),
            pl.BlockSpec((TM, d), lambda i: ((i // tpb) * nlat + jnp.maximum(i % tpb - 1, 0), 0)),
        ]
    else:
        in_specs = [
            pl.BlockSpec((TM, d), lambda i: (i, 0)),
            pl.BlockSpec((TOP_K, TM, d // 2), lambda i: (0, i, 0)),
            pl.BlockSpec((TM, LANES), lambda i: (i, 0)),
            mod_spec,
        ]
    in_specs += [
        mod_spec,
        pl.BlockSpec((1, d), lambda i: (0, 0)),
        pl.BlockSpec((1, d, Z_W), lambda i: (l, 0, 0), pipeline_mode=pl.Buffered(1)),
    ]
    return pl.pallas_call(
        functools.partial(_inproj_kernel, first=first, tpb=tpb),
        grid=(nt,),
        in_specs=in_specs,
        out_specs=[
            pl.BlockSpec((TM, ZC_W), lambda i: (i, 0)),
            pl.BlockSpec((TM, ZG_W), lambda i: (i, 0)),
            pl.BlockSpec((TM, ZR_W), lambda i: (i, 0)),
            pl.BlockSpec((TM, d), lambda i: (i, 0)),
        ],
        out_shape=[
            jax.ShapeDtypeStruct((n, ZC_W), BF16),
            jax.ShapeDtypeStruct((n, ZG_W), BF16),
            jax.ShapeDtypeStruct((n, ZR_W), BF16),
            jax.ShapeDtypeStruct((n, d), F32),
        ],
        compiler_params=_cparams(("arbitrary",)),
        name="inproj",
    )(*stream, mods, g1, w_packed)


PAD_LEAD = 16
SEG = GRID_W
LAT_STRIDE = SEG + PAD_LEAD
PAD_ROWS = (TM // SEG) * LAT_STRIDE + PAD_LEAD


def _conv_tile(zc_ref, prev_ref, next_ref, cfw_ref, cfb_ref, lng_ref, lnb_ref, scw_ref, o_ref, pad_ref, shift_ref,
               ctx, first_lat, last_lat):
    nseg = TM // SEG
    half = CF_KERNEL // 2
    zeros_lead = jnp.zeros((PAD_LEAD, GROUP_W), F32)

    def glu():
        cfa = zc_ref[:, 3 * GROUP_W:4 * GROUP_W].astype(F32)
        cfg = zc_ref[:, 4 * GROUP_W:5 * GROUP_W].astype(F32)
        return cfa * _sigmoid(cfg)

    def finish_cf(acc, s):
        y = acc + cfb_ref[...]
        yc = y - jnp.mean(y, axis=-1, keepdims=True)
        yn = yc * lax.rsqrt(jnp.mean(yc * yc, axis=-1, keepdims=True) + EPS)
        o_ref[s * SEG:(s + 1) * SEG, 0:GROUP_W] = _silu(yn * lng_ref[...] + lnb_ref[...]).astype(BF16)

    def conformer(stride):
        span = PAD_ROWS - SUBLANES
        for r in range(1, SUBLANES):
            shift_ref[r, 0:span, :] = pad_ref[r:r + span, :]
        for s in range(nseg):
            base = s * stride + PAD_LEAD - half
            acc = jnp.zeros((SEG, GROUP_W), F32)
            for k in range(CF_KERNEL):
                r = (base + k) % SUBLANES
                a = base + k - r
                win = pad_ref[a:a + SEG, :] if r == 0 else shift_ref[r, a:a + SEG, :]
                acc = acc + cfw_ref[k:k + 1, :] * win
            finish_cf(acc, s)

    def sc_products():
        scc = zc_ref[:, 0:GROUP_W].astype(F32)
        scv = zc_ref[:, GROUP_W:2 * GROUP_W].astype(F32)
        scb = zc_ref[:, 2 * GROUP_W:3 * GROUP_W].astype(F32)
        return scc * scv, scb

    if not ctx:
        u = glu()
        for s in range(nseg):
            pad_ref[s * LAT_STRIDE:s * LAT_STRIDE + PAD_LEAD, :] = zeros_lead
            pad_ref[s * LAT_STRIDE + PAD_LEAD:(s + 1) * LAT_STRIDE, :] = u[s * SEG:(s + 1) * SEG]
        pad_ref[nseg * LAT_STRIDE:nseg * LAT_STRIDE + PAD_LEAD, :] = zeros_lead
        conformer(LAT_STRIDE)
        usc, scb = sc_products()
        up = prev_ref[:, 0:GROUP_W].astype(F32) * prev_ref[:, GROUP_W:2 * GROUP_W].astype(F32)
        un = next_ref[:, 0:GROUP_W].astype(F32) * next_ref[:, GROUP_W:2 * GROUP_W].astype(F32)
        up = jnp.where(first_lat, 0.0, up)
        un = jnp.where(last_lat, 0.0, un)
        above = jnp.concatenate([up, usc[:TM - GRID_W]], axis=0)
        below = jnp.concatenate([usc[GRID_W:], un], axis=0)
        y = scb * (scw_ref[0:1, :] * above + scw_ref[1:2, :] * usc + scw_ref[2:3, :] * below)
        o_ref[:, GROUP_W:2 * GROUP_W] = y.astype(BF16)
    else:
        u = glu()
        pad_ref[0:PAD_LEAD, :] = zeros_lead
        pad_ref[PAD_LEAD:PAD_LEAD + TM, :] = u
        pad_ref[PAD_LEAD + TM:PAD_ROWS, :] = jnp.zeros((PAD_ROWS - PAD_LEAD - TM, GROUP_W), F32)
        conformer(SEG)
        usc, scb = sc_products()
        pad_ref[PAD_LEAD:PAD_LEAD + TM, :] = usc
        before = pad_ref[PAD_LEAD - 1:PAD_LEAD - 1 + TM, :]
        after = pad_ref[PAD_LEAD + 1:PAD_LEAD + 1 + TM, :]
        y = scb * (scw_ref[0:1, :] * before + scw_ref[1:2, :] * usc + scw_ref[2:3, :] * after)
        o_ref[:, GROUP_W:2 * GROUP_W] = y.astype(BF16)


def _scan_block(d, j, tpb):
    return jnp.where(d == 0, j, jnp.where(j == 0, 0, tpb - j))


def _tri(rev, size):
    ri = lax.broadcasted_iota(jnp.int32, (size, size), 0)
    ci = lax.broadcasted_iota(jnp.int32, (size, size), 1)
    return (ri <= ci) if rev else (ri >= ci)


def _gla_run(rev, zg_ref, w2_ref, b2_ref, o_ref, st_ref):
    nk = N_HEADS * HEAD_DK
    n_b = zg_ref.shape[0]
    tri = _tri(rev, GLA_CHUNK)
    trib = tri.astype(BF16)
    order = range(TM // GLA_CHUNK - 1, -1, -1) if rev else range(TM // GLA_CHUNK)
    w2b = w2_ref[0].astype(BF16)
    la = []
    for b in range(n_b):
        zz = _dot(zg_ref[b, :, 3 * GROUP_W:3 * GROUP_W + LANES], w2b) + b2_ref[0]
        la.append(_log_sigmoid(zz) * (1.0 / GLA_TAU))
    pre = {}
    for b in range(n_b):
        for c in order:
            rows = slice(c * GLA_CHUNK, (c + 1) * GLA_CHUNK)
            la_c = la[b][rows]
            hi = la_c.astype(BF16)
            lo = (la_c - hi.astype(F32)).astype(BF16)
            bc = _dot(trib, hi) + _dot(trib, lo)
            b_last = bc[0:1] if rev else bc[GLA_CHUNK - 1:GLA_CHUNK]
            b_mid = bc[GLA_CHUNK // 2:GLA_CHUNK // 2 + 1]
            q = zg_ref[b, rows, 0:nk].astype(F32) * (HEAD_DK ** -0.5)
            k = zg_ref[b, rows, nk:2 * nk].astype(F32)
            pre[b, c] = dict(
                q_in=(q * jnp.exp(bc)).astype(BF16),
                q_mid=(q * jnp.exp(bc - b_mid)).astype(BF16),
                k_mid=(k * jnp.exp(b_mid - bc)).astype(BF16),
                k_st=(k * jnp.exp(b_last - bc)).astype(BF16),
                a_row=jnp.exp(b_last),
            )
    scores = {}
    for b in range(n_b):
        for c in order:
            p = pre[b, c]
            for h in range(N_HEADS):
                hs = slice(h * HEAD_DK, (h + 1) * HEAD_DK)
                scores[b, c, h] = jnp.where(tri, _dot_nt(p["q_mid"][:, hs], p["k_mid"][:, hs]), 0.0).astype(BF16)
    for b in range(n_b):
        states = [st_ref[b, h] for h in range(N_HEADS)]
        for c in order:
            rows = slice(c * GLA_CHUNK, (c + 1) * GLA_CHUNK)
            p = pre[b, c]
            for h in range(N_HEADS):
                hs = slice(h * HEAD_DK, (h + 1) * HEAD_DK)
                vs = slice(h * HEAD_DV, (h + 1) * HEAD_DV)
                vh = zg_ref[b, rows, GROUP_W + h * HEAD_DV:GROUP_W + (h + 1) * HEAD_DV]
                st = states[h]
                o_ref[b, rows, vs] = _dot(scores[b, c, h], vh) + _dot_nt(p["q_in"][:, hs], st.astype(BF16))
                states[h] = st * p["a_row"][:, hs] + _dot_tn(vh, p["k_st"][:, hs])
        for h in range(N_HEADS):
            st_ref[b, h] = states[h]


def _ret_run(rev, zr_ref, cos_ref, sin_ref, lg_ref, o_ref, st_ref):
    nk = N_HEADS * HEAD_DK
    batches = range(zr_ref.shape[0])
    lane = lax.broadcasted_iota(jnp.int32, (TM, nk), 1)
    first_half = (lane % HEAD_DK) < (HEAD_DK // 2)
    cosf = cos_ref[...]
    sinf = sin_ref[...]

    def rope(x):
        swapped = jnp.where(first_half, pltpu.roll(x, nk - HEAD_DK // 2, 1), pltpu.roll(x, HEAD_DK // 2, 1))
        return x * cosf + swapped * sinf

    qs = [rope(zr_ref[b, :, 0:nk].astype(F32)).astype(BF16) for b in batches]
    ks = [rope(zr_ref[b, :, nk:2 * nk].astype(F32)) * (HEAD_DK ** -0.5) for b in batches]
    lg = _log_sigmoid(lg_ref[0])
    tri = _tri(rev, TM)
    ri = lax.broadcasted_iota(jnp.int32, (TM, TM), 0)
    ci = lax.broadcasted_iota(jnp.int32, (TM, TM), 1)
    rel = ((ci - ri) if rev else (ri - ci)).astype(F32)
    pos = lax.broadcasted_iota(jnp.int32, (TM, LANES), 0).astype(F32)
    q_steps = (TM - pos) if rev else (pos + 1.0)
    k_steps = pos if rev else (TM - 1.0 - pos)
    lg_wide = jnp.concatenate([lg, lg], axis=1)
    scores = {}
    for h in range(N_HEADS):
        hs = slice(h * HEAD_DK, (h + 1) * HEAD_DK)
        decay = jnp.where(tri, jnp.exp(rel * lg_wide[h:h + 1, :]), 0.0)
        for b in batches:
            scores[b, h] = (_dot_nt(qs[b][:, hs], ks[b][:, hs].astype(BF16)) * decay).astype(BF16)
    for h in range(N_HEADS):
        hs = slice(h * HEAD_DK, (h + 1) * HEAD_DK)
        vs = slice(h * HEAD_DV, (h + 1) * HEAD_DV)
        lgh = lg[h:h + 1, :]
        q_decay = jnp.exp(q_steps * lgh)
        k_decay = jnp.exp(k_steps[:, 0:HEAD_DK] * lgh[:, 0:HEAD_DK])
        chunk_decay = jnp.exp(float(TM) * lgh[:, 0:HEAD_DK])
        for b in batches:
            st = st_ref[b, h]
            vh = zr_ref[b, :, GROUP_W + h * HEAD_DV:GROUP_W + (h + 1) * HEAD_DV]
            o_ref[b, :, vs] = _dot(scores[b, h], vh) + _dot_nt(qs[b][:, hs], st.astype(BF16)) * q_decay
            st_ref[b, h] = st * chunk_decay + _dot_tn(vh, (ks[b][:, hs] * k_decay).astype(BF16))


def _mixers_kernel(zc_ref, prev_ref, next_ref, cfw_ref, cfb_ref, lng_ref, lnb_ref, scw_ref, zg_ref, w2_ref, b2_ref,
                   zr_ref, cos_ref, sin_ref, lg_ref, mix_ref, og_ref, or_ref, pad_ref, shift_ref, stg_ref, str_ref,
                   *, tpb):
    d = pl.program_id(0)
    j = pl.program_id(1)
    blk = _scan_block(d, j, tpb)

    @pl.when(j == 0)
    def _init():
        stg_ref[...] = jnp.zeros_like(stg_ref)
        str_ref[...] = jnp.zeros_like(str_ref)

    def body(rev, ctx):
        _conv_tile(zc_ref.at[0], prev_ref.at[0], next_ref.at[0], cfw_ref, cfb_ref, lng_ref, lnb_ref, scw_ref,
                   mix_ref.at[0], pad_ref, shift_ref, ctx, blk == 1, blk == tpb - 1)
        _gla_run(rev, zg_ref, w2_ref, b2_ref, og_ref.at[0], stg_ref)
        _ret_run(rev, zr_ref, cos_ref, sin_ref, lg_ref, or_ref.at[0], str_ref)

    for rev in (False, True):
        for ctx in (False, True):
            pl.when((d == int(rev)) & ((j == 0) == ctx))(functools.partial(body, rev, ctx))


def _mixers_call(zc, zg, zr, cfw, cfb, lng, lnb, scw, w2pad, b2, cos_t, sin_t, lg, bsz, tpb):
    n = zc.shape[0]
    t = n // bsz
    nk = N_HEADS * HEAD_DK
    r = TM // GRID_W
    nhalo = t // GRID_W
    assert bsz == 2

    def blk(d, j):
        return _scan_block(d, j, tpb)

    def const(shape):
        return pl.BlockSpec(shape, lambda d, j: (0,) * len(shape))

    scan_out = pl.BlockSpec((1, bsz, TM, GROUP_W), lambda d, j: (d, 0, blk(d, j), 0))
    zc3 = zc.reshape(bsz, t, ZC_W)
    mix, og, orr = pl.pallas_call(
        functools.partial(_mixers_kernel, tpb=tpb),
        grid=(2, tpb),
        in_specs=[
            pl.BlockSpec((1, TM, ZC_W), lambda d, j: (d, blk(d, j), 0)),
            pl.BlockSpec((1, GRID_W, 2 * GROUP_W), lambda d, j: (d, jnp.maximum(blk(d, j) * r - 1, 0), 0)),
            pl.BlockSpec((1, GRID_W, 2 * GROUP_W), lambda d, j: (d, jnp.minimum(blk(d, j) * r + r, nhalo - 1), 0)),
            const((CF_KERNEL, GROUP_W)),
            const((1, GROUP_W)),
            const((1, GROUP_W)),
            const((1, GROUP_W)),
            const((SC_KERNEL, GROUP_W)),
            pl.BlockSpec((bsz, TM, ZG_W), lambda d, j: (0, blk(d, j), 0)),
            pl.BlockSpec((1, LANES, nk), lambda d, j: (d, 0, 0)),
            pl.BlockSpec((1, 1, nk), lambda d, j: (d, 0, 0)),
            pl.BlockSpec((bsz, TM, ZR_W), lambda d, j: (0, blk(d, j), 0)),
            pl.BlockSpec((TM, nk), lambda d, j: (blk(d, j), 0)),
            pl.BlockSpec((TM, nk), lambda d, j: (blk(d, j), 0)),
            pl.BlockSpec((1, N_HEADS, LANES), lambda d, j: (d, 0, 0)),
        ],
        out_specs=[
            pl.BlockSpec((1, TM, 2 * GROUP_W), lambda d, j: (d, blk(d, j), 0)),
            scan_out,
            scan_out,
        ],
        out_shape=[
            jax.ShapeDtypeStruct((bsz, t, 2 * GROUP_W), BF16),
            jax.ShapeDtypeStruct((2, bsz, t, GROUP_W), F32),
            jax.ShapeDtypeStruct((2, bsz, t, GROUP_W), F32),
        ],
        scratch_shapes=[
            pltpu.VMEM((PAD_ROWS, GROUP_W), F32),
            pltpu.VMEM((SUBLANES, PAD_ROWS, GROUP_W), F32),
            pltpu.VMEM((bsz, N_HEADS, HEAD_DV, HEAD_DK), F32),
            pltpu.VMEM((bsz, N_HEADS, HEAD_DV, HEAD_DK), F32),
        ],
        compiler_params=_cparams(("arbitrary", "arbitrary")),
        name="mixers",
    )(zc3, zc3, zc3, cfw, cfb, lng, lnb, scw, zg.reshape(bsz, t, ZG_W), w2pad, b2, zr.reshape(bsz, t, ZR_W),
      cos_t, sin_t, lg)
    return mix.reshape(n, 2 * GROUP_W), og.reshape(2, n, GROUP_W), orr.reshape(2, n, GROUP_W)


def _head_norm(o, g, center):
    outs = []
    for h in range(N_HEADS):
        oh = o[:, h * HEAD_DV:(h + 1) * HEAD_DV]
        if center:
            oh = oh - jnp.mean(oh, axis=-1, keepdims=True)
        outs.append(oh * lax.rsqrt(jnp.mean(oh * oh, axis=-1, keepdims=True) + EPS))
    return jnp.concatenate(outs, axis=1) * g


def _route(logits):
    lane = lax.broadcasted_iota(jnp.int32, logits.shape, 1).astype(F32)
    neg = jnp.float32(-jnp.inf)
    big = jnp.float32(LANES)
    gmask = (lane >= N_EXPERTS) & (lane < N_EXPERTS + N_GROUPS)
    gl = jnp.where(gmask, logits, neg)
    gmax = jnp.max(gl, axis=-1, keepdims=True)
    gidx = jnp.min(jnp.where(gl == gmax, lane - N_EXPERTS, big), axis=-1, keepdims=True)
    g_w = 1.0 / jnp.sum(jnp.where(gmask, jnp.exp(logits - gmax), 0.0), axis=-1, keepdims=True)
    emask = (lane >= gidx * EXPERTS_PER_GROUP) & (lane < (gidx + 1.0) * EXPERTS_PER_GROUP)
    el = jnp.where(emask, logits, neg)
    e1 = jnp.max(el, axis=-1, keepdims=True)
    i1 = jnp.min(jnp.where(el == e1, lane, big), axis=-1, keepdims=True)
    el2 = jnp.where(lane == i1, neg, el)
    e2 = jnp.max(el2, axis=-1, keepdims=True)
    i2 = jnp.min(jnp.where(el2 == e2, lane, big), axis=-1, keepdims=True)
    r = jnp.exp(e2 - e1)
    w1 = g_w / (1.0 + r)
    w2 = g_w * r / (1.0 + r)
    idx = jnp.where(lane == 0.0, i1, jnp.where(lane == 1.0, i2, 0.0)).astype(jnp.int32)
    gate = jnp.where(lane == 0.0, w1, jnp.where(lane == 1.0, w2, 0.0))
    return idx, gate


def _outproj_kernel(mixc_ref, og_ref, or_ref, gr_ref, rg_ref, gng_ref, rng_ref, wout_ref, x_ref, mod_ref, g2_ref,
                    wr_ref, br_ref, xo_ref, h_ref, idx_ref, gate_ref):
    j = pl.program_id(0)
    w = wr_ref[...]
    w_hi = w.astype(BF16)
    w_lo = (w - w_hi.astype(F32)).astype(BF16)
    batches = range(x_ref.shape[0])
    mods = [mod_ref[jnp.where(j == 0, mod_ref.shape[0] - 1, b)] for b in batches]
    glas = [_head_norm(og_ref[0, b] + og_ref[1, b], gng_ref[...], False) * _silu(gr_ref[b].astype(F32))
            for b in batches]
    rets = [_head_norm(or_ref[0, b] + or_ref[1, b], rng_ref[...], True) * _silu(rg_ref[b].astype(F32))
            for b in batches]
    ys = [_dot(mixc_ref[b], wout_ref[0, 0:2 * GROUP_W, :])
          + _dot(glas[b].astype(BF16), wout_ref[0, 2 * GROUP_W:3 * GROUP_W, :])
          + _dot(rets[b].astype(BF16), wout_ref[0, 3 * GROUP_W:4 * GROUP_W, :]) for b in batches]
    hs = []
    for b in batches:
        x = x_ref[b] + mods[b][2:3, :] * ys[b]
        xo_ref[b] = x
        hn = x * lax.rsqrt(jnp.mean(x * x, axis=-1, keepdims=True) + EPS) * g2_ref[...]
        h = hn * (1.0 + mods[b][4:5, :]) + mods[b][3:4, :]
        h_ref[b] = _pack_pairs(h)
        hs.append(h)
    logits = []
    for b in batches:
        hb = hs[b].astype(BF16)
        h_lo = (hs[b] - hb.astype(F32)).astype(BF16)
        logits.append(_dot(hb, w_hi) + _dot(hb, w_lo) + _dot(h_lo, w_hi) + br_ref[...])
    for b in batches:
        idx, gate = _route(logits[b])
        idx_ref[b] = idx
        gate_ref[b] = gate


def _outproj_call(mixc, og, orr, zg, zr, gng, rng, wout, l, x, mods, g2, wr, br, bsz, tpb):
    n, d = x.shape
    t = n // bsz

    def rows(width, col=0):
        return pl.BlockSpec((bsz, TM, width), lambda j: (0, j, col))

    scan = pl.BlockSpec((2, bsz, TM, GROUP_W), lambda j: (0, 0, j, 0))
    outs = pl.pallas_call(
        _outproj_kernel,
        grid=(tpb,),
        in_specs=[
            rows(2 * GROUP_W),
            scan,
            scan,
            rows(GROUP_W, 2),
            rows(GROUP_W, 2),
            pl.BlockSpec((1, GROUP_W), lambda j: (0, 0)),
            pl.BlockSpec((1, GROUP_W), lambda j: (0, 0)),
            pl.BlockSpec((1, d, d), lambda j: (l, 0, 0), pipeline_mode=pl.Buffered(1)),
            rows(d),
            pl.BlockSpec((bsz + 1, N_MOD, d), lambda j: (0, 0, 0)),
            pl.BlockSpec((1, d), lambda j: (0, 0)),
            pl.BlockSpec((d, LANES), lambda j: (0, 0)),
            pl.BlockSpec((1, LANES), lambda j: (0, 0)),
        ],
        out_specs=[rows(d), rows(d // 2), rows(LANES), rows(LANES)],
        out_shape=[
            jax.ShapeDtypeStruct((bsz, t, d), F32),
            jax.ShapeDtypeStruct((bsz, t, d // 2), jnp.uint32),
            jax.ShapeDtypeStruct((bsz, t, LANES), jnp.int32),
            jax.ShapeDtypeStruct((bsz, t, LANES), F32),
        ],
        compiler_params=_cparams(("arbitrary",)),
        name="outproj",
    )(mixc.reshape(bsz, t, -1), og.reshape(2, bsz, t, -1), orr.reshape(2, bsz, t, -1), zg.reshape(bsz, t, -1),
      zr.reshape(bsz, t, -1), gng, rng, wout, x.reshape(bsz, t, d), mods, g2, wr, br)
    return [o.reshape(n, -1) for o in outs]


def _sc_gather(table, idx):
    n_idx = idx.shape[0]
    width = table.shape[1]
    assert n_idx % (SC_WINDOW * SC_WORKERS) == 0
    per = n_idx // SC_WORKERS
    n_steps = per // SC_WINDOW
    assert n_steps % 2 == 0
    per_pad = (per + LANES - 1) // LANES * LANES
    idx_w = jnp.pad(idx.reshape(SC_WORKERS, per), ((0, 0), (0, per_pad - per)))
    mesh = plsc.VectorSubcoreMesh(core_axis_name="core", subcore_axis_name="subcore")
    n_cores = SC_WORKERS // 16

    @functools.partial(
        pl.kernel,
        out_type=jax.ShapeDtypeStruct((n_idx, width), table.dtype),
        mesh=mesh,
        scratch_types=[
            pltpu.VMEM((per_pad,), jnp.int32),
            pltpu.VMEM((2, SC_WINDOW, width), table.dtype),
            pltpu.SemaphoreType.DMA((2,)),
            pltpu.SemaphoreType.DMA((2,)),
        ],
    )
    def gather_kernel(table_hbm, idx_hbm, out_hbm, idx_v, rows_v, sem_g, sem_w):
        wid = lax.axis_index("subcore") * n_cores + lax.axis_index("core")
        base = wid * per
        pltpu.sync_copy(idx_hbm.at[wid], idx_v)

        def gather(step, buf):
            return pltpu.make_async_copy(table_hbm.at[idx_v.at[pl.ds(step * SC_WINDOW, SC_WINDOW)]],
                                         rows_v.at[buf], sem_g.at[buf])

        def write(step, buf):
            return pltpu.make_async_copy(rows_v.at[buf], out_hbm.at[pl.ds(base + step * SC_WINDOW, SC_WINDOW)],
                                         sem_w.at[buf])

        gather(0, 0).start()

        @pl.loop(0, n_steps, step=2)
        def _(s):
            for buf in range(2):
                step = s + buf
                other = 1 - buf
                gather(step, buf).wait()
                write(step, buf).start()

                @pl.when(step >= 1)
                def _():
                    write(step - 1, other).wait()

                @pl.when(step + 1 < n_steps)
                def _():
                    gather(step + 1, other).start()

        write(n_steps - 1, 1).wait()

    return gather_kernel(table, idx_w)


def _expert_up_kernel(blk_e_ref, nvalid_ref, x_ref, w1_ref, w3_ref, h_ref):
    @pl.when(nvalid_ref[pl.program_id(0)] > 0)
    def _compute():
        x = _unpack_pairs(x_ref[...]).astype(BF16)
        h1 = _dot(x, w1_ref[0, 0].astype(BF16))
        h3 = _dot(x, w3_ref[0, 0].astype(BF16))
        h_ref[...] = (_silu(h1) * h3).astype(BF16)


def _expert_down_kernel(blk_e_ref, nvalid_ref, h_ref, w2_ref, y_ref):
    @pl.when(nvalid_ref[pl.program_id(0)] > 0)
    def _compute():
        y_ref[...] = _pack_pairs(_dot(h_ref[...], w2_ref[0, 0].astype(BF16)))


def _experts_call(blk_e, nvalid, xs, w1, w3, w2, l):
    n_slots = xs.shape[0]
    d = w1.shape[-2]
    nb = n_slots // MOE_TB
    hidden = w1.shape[-1]
    up = pl.pallas_call(
        _expert_up_kernel,
        grid_spec=pltpu.PrefetchScalarGridSpec(
            num_scalar_prefetch=2,
            grid=(nb,),
            in_specs=[
                pl.BlockSpec((MOE_TB, d // 2), lambda i, be, nv: (i, 0)),
                pl.BlockSpec((1, 1, d, hidden), lambda i, be, nv: (l, be[i], 0, 0)),
                pl.BlockSpec((1, 1, d, hidden), lambda i, be, nv: (l, be[i], 0, 0)),
            ],
            out_specs=pl.BlockSpec((MOE_TB, hidden), lambda i, be, nv: (i, 0)),
        ),
        out_shape=jax.ShapeDtypeStruct((n_slots, hidden), BF16),
        compiler_params=_cparams(("arbitrary",)),
        name="expert_up",
    )(blk_e, nvalid, xs, w1, w3)
    return pl.pallas_call(
        _expert_down_kernel,
        grid_spec=pltpu.PrefetchScalarGridSpec(
            num_scalar_prefetch=2,
            grid=(nb,),
            in_specs=[
                pl.BlockSpec((MOE_TB, hidden), lambda i, be, nv: (i, 0)),
                pl.BlockSpec((1, 1, hidden, d), lambda i, be, nv: (l, be[i], 0, 0)),
            ],
            out_specs=pl.BlockSpec((MOE_TB, d // 2), lambda i, be, nv: (i, 0)),
        ),
        out_shape=jax.ShapeDtypeStruct((n_slots, d // 2), jnp.uint32),
        compiler_params=_cparams(("arbitrary",)),
        name="expert_down",
    )(blk_e, nvalid, up, w2)


def _slot_plan(idx):
    n = idx.shape[0]
    n_asg = n * TOP_K
    flat_e = idx[:, :TOP_K].reshape(n_asg)
    order = jnp.argsort(flat_e).astype(jnp.int32)
    experts = jnp.arange(N_EXPERTS, dtype=jnp.int32)
    counts = jnp.sum(flat_e[:, None] == experts[None, :], axis=0, dtype=jnp.int32)
    padded = (counts + MOE_TB - 1) // MOE_TB * MOE_TB
    pad_end = jnp.cumsum(padded)
    pad_start = pad_end - padded
    start = jnp.cumsum(counts) - counts
    n_slots = (n_asg + MOE_TB - 1) // MOE_TB * MOE_TB + N_EXPERTS * MOE_TB
    nb = n_slots // MOE_TB
    blk0 = jnp.arange(nb, dtype=jnp.int32) * MOE_TB
    blk_e = jnp.minimum(jnp.sum(blk0[:, None] >= pad_end[None, :], axis=1, dtype=jnp.int32), N_EXPERTS - 1)
    sel = (blk_e[:, None] == experts[None, :]).astype(jnp.int32)
    blk_cnt = jnp.sum(sel * counts[None, :], axis=1)
    blk_pad0 = jnp.sum(sel * pad_start[None, :], axis=1)
    blk_start = jnp.sum(sel * start[None, :], axis=1)
    nvalid = jnp.clip(blk_cnt - (blk0 - blk_pad0), 0, MOE_TB).astype(jnp.int32)
    blk_w = jnp.where(nvalid > 0, blk_e, jnp.max(jnp.where(nvalid > 0, blk_e, 0)))
    within = jnp.arange(MOE_TB, dtype=jnp.int32)[None, :]
    valid = within < nvalid[:, None]
    pos = jnp.clip((blk_start + blk0 - blk_pad0)[:, None] + within, 0, n_asg - 1)
    asg = order[pos.reshape(n_slots)]
    filler = jnp.arange(n_slots, dtype=jnp.int32) % n
    slot_tok = jnp.where(valid.reshape(n_slots), lax.shift_right_logical(asg, 1), filler).astype(jnp.int32)
    rank = jnp.argsort(order).astype(jnp.int32)
    sel_a = (flat_e[:, None] == experts[None, :]).astype(jnp.int32)
    dest = rank + jnp.sum(sel_a * (pad_start - start)[None, :], axis=1)
    dest_ct = dest.reshape(n, TOP_K).T.reshape(n_asg)
    return blk_w, nvalid, slot_tok, dest_ct


def _final_kernel(x_ref, y_ref, gate_ref, mod_ref, g_ref, o_ref):
    x = _combined(x_ref, y_ref, gate_ref, mod_ref)
    o_ref[0] = x * lax.rsqrt(jnp.mean(x * x, axis=-1, keepdims=True) + EPS) * g_ref[...]


def _final_call(x, y, gate, mods, gf, bsz, tpb):
    n, d = x.shape
    lat = tpb - 1
    return pl.pallas_call(
        _final_kernel,
        grid=(bsz, lat),
        in_specs=[
            pl.BlockSpec((TM, d), lambda b, j: (b * tpb + j + 1, 0)),
            pl.BlockSpec((TOP_K, TM, d // 2), lambda b, j: (0, b * tpb + j + 1, 0)),
            pl.BlockSpec((TM, LANES), lambda b, j: (b * tpb + j + 1, 0)),
            pl.BlockSpec((1, N_MOD, d), lambda b, j: (b, 0, 0)),
            pl.BlockSpec((1, d), lambda b, j: (0, 0)),
        ],
        out_specs=pl.BlockSpec((1, TM, d), lambda b, j: (b, j, 0)),
        out_shape=jax.ShapeDtypeStruct((bsz, lat * TM, d), F32),
        compiler_params=_cparams(("arbitrary", "arbitrary")),
        name="final_norm",
    )(x, y, gate, mods, gf)


def _rope_tables(seq):
    n_freq = HEAD_DK // 4
    t = jnp.arange(seq)
    inv = ROPE_BASE ** (-jnp.arange(n_freq, dtype=F32) / n_freq)
    ang = jnp.concatenate([(t // GRID_W).astype(F32)[:, None] * inv, (t % GRID_W).astype(F32)[:, None] * inv], axis=-1)
    cos = jnp.concatenate([jnp.ones((TM, HEAD_DK // 2), F32), jnp.cos(ang)], axis=0)
    sin = jnp.concatenate([jnp.zeros((TM, HEAD_DK // 2), F32), jnp.sin(ang)], axis=0)
    cos_t = jnp.tile(jnp.concatenate([cos, cos], axis=-1), (1, N_HEADS))
    sin_t = jnp.tile(jnp.concatenate([-sin, sin], axis=-1), (1, N_HEADS))
    return cos_t, sin_t


def kernel(x, c, ctx, c_ctx, norm1_g, norm2_g, ada_w, ada_b, w_in, cf_dw, cf_b, cf_ln_g, cf_ln_b, sc_dw, gla_w2,
           gla_b2, gla_ng, ret_logit, ret_ng, w_out, w_grp, b_grp, w_rt, b_rt, e_w1, e_w3, e_w2, final_g):
    bsz, seq, d = x.shape
    depth = w_in.shape[0]
    assert d == D_MODEL and ctx.shape[1] == TM and seq % TM == 0 and bsz == 2 and TOP_K == 2
    assert w_in.shape[-1] == IN_W
    tpb = 1 + seq // TM
    n = bsz * tpb * TM
    nk = N_HEADS * HEAD_DK

    s8 = jnp.concatenate([c, c_ctx[None, :], jnp.zeros((8 - bsz - 1, d), F32)], axis=0)
    mods_all = _ada_call(s8, ada_w, ada_b)[:, :bsz + 1, :].reshape(depth, bsz + 1, N_MOD, d)

    w_in_p = _pack_w_in(w_in)
    w_out_b = _cast_w_out(w_out)
    cos_t, sin_t = _rope_tables(seq)
    w2pad = jnp.zeros((depth, 2, LANES, nk), F32)
    w2pad = w2pad.at[:, 0, 0:GLA_RANK, :].set(gla_w2[:, 0]).at[:, 1, GLA_RANK:2 * GLA_RANK, :].set(gla_w2[:, 1])
    wr_all = jnp.concatenate([w_rt, w_grp, jnp.zeros((depth, d, LANES - N_EXPERTS - N_GROUPS), F32)], axis=-1)
    br_all = jnp.concatenate([b_rt, b_grp, jnp.zeros((depth, LANES - N_EXPERTS - N_GROUPS), F32)], axis=-1)

    out = None
    stream = (ctx.reshape(bsz * TM, d), x.reshape(bsz * seq, d))
    for l in range(depth):
        mods = mods_all[l]
        zc, zg, zr, xa = _inproj_call(stream, mods, norm1_g[l][None, :], w_in_p, l, tpb, n)
        lg = jnp.broadcast_to(ret_logit[l][:, :, None], (2, N_HEADS, LANES))
        mixc, og, orr = _mixers_call(zc, zg, zr, cf_dw[l], cf_b[l][None, :], cf_ln_g[l][None, :],
                                     cf_ln_b[l][None, :], sc_dw[l], w2pad[l], gla_b2[l][:, None, :], cos_t, sin_t,
                                     lg, bsz, tpb)
        xa, h2, idx, gate = _outproj_call(mixc, og, orr, zg, zr, gla_ng[l][None, :], ret_ng[l][None, :], w_out_b, l,
                                          xa, mods, norm2_g[l][None, :], wr_all[l], br_all[l][None, :], bsz, tpb)
        blk_e, nvalid, slot_tok, dest_ct = _slot_plan(idx)
        xs = _sc_gather(h2, slot_tok)
        ys = _experts_call(blk_e, nvalid, xs, e_w1, e_w3, e_w2, l)
        y = _sc_gather(ys, dest_ct).reshape(TOP_K, n, d // 2)
        if l == depth - 1:
            out = _final_call(xa, y, gate, mods, final_g[None, :], bsz, tpb)
        else:
            stream = (xa, y, gate, mods)
    return out
```

```python
import functools
import math

import jax
import jax.numpy as jnp
from jax import lax
from jax.experimental import pallas as pl
from jax.experimental.pallas import tpu as pltpu
from jax.experimental.pallas import tpu_sc as plsc

F32 = jnp.float32
BF16 = jnp.bfloat16

D_MODEL = 2048
GRID_W = 64
GROUP_W = D_MODEL // 4
CF_KERNEL = 31
SC_KERNEL = 3
N_HEADS = 4
HEAD_DK = 64
HEAD_DV = 128
GLA_RANK = 16
GLA_TAU = 16.0
GLA_CHUNK = 128
ROPE_BASE = 10000.0
N_GROUPS = 4
EXPERTS_PER_GROUP = 4
N_EXPERTS = N_GROUPS * EXPERTS_PER_GROUP
TOP_K = 2
EXPERT_HIDDEN = D_MODEL // 2
N_MOD = 6
EPS = 1e-6

TM = 256
LANES = 128
SUBLANES = 8
ADA_TN = 1024
MOE_TB = 576
VMEM_LIMIT = 56 * 1024 * 1024
SC_WORKERS = 32
SC_WINDOW = 16

ZC_W = 5 * GROUP_W
ZG_W = 3 * GROUP_W + LANES
ZR_W = 3 * GROUP_W
Z_W = ZC_W + ZG_W + ZR_W
IN_W = Z_W - (LANES - 2 * GLA_RANK)
GLR_END = ZC_W + 3 * GROUP_W + 2 * GLA_RANK
PACK_MOVES = (
    (3 * GROUP_W, 0, 2 * GROUP_W),
    (2 * GROUP_W, 2 * GROUP_W, GROUP_W),
    (0, 3 * GROUP_W, 2 * GROUP_W),
    (ZC_W, ZC_W, 3 * GROUP_W),
    (GLR_END, ZC_W + ZG_W, ZR_W),
)


def _cparams(sem):
    return pltpu.CompilerParams(dimension_semantics=sem, vmem_limit_bytes=VMEM_LIMIT)


def _sigmoid(x):
    return 1.0 / (1.0 + jnp.exp(-x))


def _silu(x):
    return x * _sigmoid(x)


def _log_sigmoid(x):
    return jnp.minimum(x, 0.0) - jnp.log1p(jnp.exp(-jnp.abs(x)))


def _dot(a, b):
    return jnp.dot(a, b, preferred_element_type=F32)


def _dot_nt(a, b):
    return lax.dot_general(a, b, (((1,), (1,)), ((), ())), preferred_element_type=F32)


def _dot_tn(a, b):
    return lax.dot_general(a, b, (((0,), (0,)), ((), ())), preferred_element_type=F32)


def _pack_pairs(x):
    w = x.shape[1] // 2
    xb = x.astype(BF16).astype(F32)
    hi = pltpu.bitcast(xb[:, :w], jnp.uint32)
    lo = pltpu.bitcast(xb[:, w:], jnp.uint32)
    return hi | lax.shift_right_logical(lo, jnp.uint32(16))


def _unpack_pairs(p):
    hi = pltpu.bitcast(p & jnp.uint32(0xFFFF0000), F32)
    lo = pltpu.bitcast(lax.shift_left(p, jnp.uint32(16)), F32)
    return jnp.concatenate([hi, lo], axis=1)


def _mod_row(i, tpb):
    return jnp.where(i % tpb == 0, 2, i // tpb)


def _pack_kernel(wt_ref, o_ref):
    def put(src, dst, width):
        for c in range(0, width, TM):
            step = min(TM, width - c)
            o_ref[0, :, dst + c:dst + c + step] = wt_ref[0, src + c:src + c + step, :].T.astype(BF16)

    for src, dst, width in PACK_MOVES:
        put(src, dst, width)
    glr0 = GLR_END - 2 * GLA_RANK
    tile = wt_ref[0, glr0:glr0 + LANES, :].T
    lane = lax.broadcasted_iota(jnp.int32, tile.shape, 1)
    o_ref[0, :, glr0:glr0 + LANES] = jnp.where(lane < 2 * GLA_RANK, tile, 0.0).astype(BF16)


def _pack_w_in(w_in):
    depth, d, _ = w_in.shape
    return pl.pallas_call(
        _pack_kernel,
        grid=(depth, d // TM),
        in_specs=[pl.BlockSpec((1, IN_W, TM), lambda l, i: (l, 0, i))],
        out_specs=pl.BlockSpec((1, TM, Z_W), lambda l, i: (l, i, 0)),
        out_shape=jax.ShapeDtypeStruct((depth, d, Z_W), BF16),
        compiler_params=_cparams(("arbitrary", "arbitrary")),
        name="pack_w_in",
    )(jnp.swapaxes(w_in, 1, 2))


def _cast_kernel(w_ref, o_ref):
    o_ref[...] = w_ref[...].astype(BF16)


def _cast_w_out(w_out):
    depth, k, d = w_out.shape
    return pl.pallas_call(
        _cast_kernel,
        grid=(depth, k // TM),
        in_specs=[pl.BlockSpec((1, TM, d), lambda l, i: (l, i, 0))],
        out_specs=pl.BlockSpec((1, TM, d), lambda l, i: (l, i, 0)),
        out_shape=jax.ShapeDtypeStruct((depth, k, d), BF16),
        compiler_params=_cparams(("arbitrary", "arbitrary")),
        name="cast_w_out",
    )(w_out)


def _ada_kernel(s_ref, w_ref, b_ref, o_ref):
    a = _silu(s_ref[...]).astype(BF16)
    o_ref[0] = _dot(a, w_ref[0].astype(BF16)) + b_ref[0]


def _ada_call(s8, ada_w, ada_b):
    depth, d, nm = ada_w.shape
    return pl.pallas_call(
        _ada_kernel,
        grid=(depth, nm // ADA_TN),
        in_specs=[
            pl.BlockSpec((8, d), lambda l, j: (0, 0)),
            pl.BlockSpec((1, d, ADA_TN), lambda l, j: (l, 0, j)),
            pl.BlockSpec((1, 1, ADA_TN), lambda l, j: (l, 0, j)),
        ],
        out_specs=pl.BlockSpec((1, 8, ADA_TN), lambda l, j: (l, 0, j)),
        out_shape=jax.ShapeDtypeStruct((depth, 8, nm), F32),
        compiler_params=_cparams(("arbitrary", "arbitrary")),
        name="adaln",
    )(s8, ada_w, ada_b.reshape(depth, 1, nm))


def _combined(x_ref, y_ref, gate_ref, mod_ref):
    f = gate_ref[:, 0:1] * _unpack_pairs(y_ref[0]) + gate_ref[:, 1:2] * _unpack_pairs(y_ref[1])
    return x_ref[...] + mod_ref[0, 5:6, :] * f


def _inproj_kernel(*refs, first, tpb):
    if first:
        ctx_ref, lat_ref, mod_ref, g_ref, w_ref, zc_ref, zg_ref, zr_ref, xo_ref = refs
        x = jnp.where(pl.program_id(0) % tpb == 0, ctx_ref[...], lat_ref[...])
    else:
        x_ref, y_ref, gate_ref, pmod_ref, mod_ref, g_ref, w_ref, zc_ref, zg_ref, zr_ref, xo_ref = refs
        x = _combined(x_ref, y_ref, gate_ref, pmod_ref)
    xo_ref[...] = x
    y = x * lax.rsqrt(jnp.mean(x * x, axis=-1, keepdims=True) + EPS) * g_ref[...]
    h = (y * (1.0 + mod_ref[0, 1:2, :]) + mod_ref[0, 0:1, :]).astype(BF16)
    off = 0
    for ref, width in ((zc_ref, ZC_W), (zg_ref, ZG_W), (zr_ref, ZR_W)):
        c = 0
        while c < width:
            step = min(512, width - c)
            ref[:, c:c + step] = _dot(h, w_ref[0, :, off + c:off + c + step]).astype(BF16)
            c += step
        off += width


def _inproj_call(stream, mods, g1, w_packed, l, tpb, n):
    d = D_MODEL
    nt = n // TM
    nlat = tpb - 1
    mod_spec = pl.BlockSpec((1, N_MOD, d), lambda i: (_mod_row(i, tpb), 0, 0))
    first = len(stream) == 2
    if first:
        in_specs = [
            pl.BlockSpec((TM, d), lambda i: (i // tpb, 0)),
            pl.BlockSpec((TM, d), lambda i: ((i // tpb) * nlat + jnp.maximum(i % tpb - 1, 0), 0)),
        ]
    else:
        in_specs = [
            pl.BlockSpec((TM, d), lambda i: (i, 0)),
            pl.BlockSpec((TOP_K, TM, d // 2), lambda i: (0, i, 0)),
            pl.BlockSpec((TM, LANES), lambda i: (i, 0)),
            mod_spec,
        ]
    in_specs += [
        mod_spec,
        pl.BlockSpec((1, d), lambda i: (0, 0)),
        pl.BlockSpec((1, d, Z_W), lambda i: (l, 0, 0), pipeline_mode=pl.Buffered(1)),
    ]
    return pl.pallas_call(
        functools.partial(_inproj_kernel, first=first, tpb=tpb),
        grid=(nt,),
        in_specs=in_specs,
        out_specs=[
            pl.BlockSpec((TM, ZC_W), lambda i: (i, 0)),
            pl.BlockSpec((TM, ZG_W), lambda i: (i, 0)),
            pl.BlockSpec((TM, ZR_W), lambda i: (i, 0)),
            pl.BlockSpec((TM, d), lambda i: (i, 0)),
        ],
        out_shape=[
            jax.ShapeDtypeStruct((n, ZC_W), BF16),
            jax.ShapeDtypeStruct((n, ZG_W), BF16),
            jax.ShapeDtypeStruct((n, ZR_W), BF16),
            jax.ShapeDtypeStruct((n, d), F32),
        ],
        compiler_params=_cparams(("arbitrary",)),
        name="inproj",
    )(*stream, mods, g1, w_packed)


PAD_LEAD = 16
SEG = GRID_W
LAT_STRIDE = SEG + PAD_LEAD
PAD_ROWS = (TM // SEG) * LAT_STRIDE + PAD_LEAD


def _conv_tile(zc_ref, prev_ref, next_ref, cfw_ref, cfb_ref, lng_ref, lnb_ref, scw_ref, o_ref, pad_ref, shift_ref,
               ctx, first_lat, last_lat):
    nseg = TM // SEG
    half = CF_KERNEL // 2
    zeros_lead = jnp.zeros((PAD_LEAD, GROUP_W), F32)

    def glu():
        cfa = zc_ref[:, 3 * GROUP_W:4 * GROUP_W].astype(F32)
        cfg = zc_ref[:, 4 * GROUP_W:5 * GROUP_W].astype(F32)
        return cfa * _sigmoid(cfg)

    def finish_cf(acc, s):
        y = acc + cfb_ref[...]
        yc = y - jnp.mean(y, axis=-1, keepdims=True)
        yn = yc * lax.rsqrt(jnp.mean(yc * yc, axis=-1, keepdims=True) + EPS)
        o_ref[s * SEG:(s + 1) * SEG, 0:GROUP_W] = _silu(yn * lng_ref[...] + lnb_ref[...]).astype(BF16)

    def conformer(stride):
        span = PAD_ROWS - SUBLANES
        for r in range(1, SUBLANES):
            shift_ref[r, 0:span, :] = pad_ref[r:r + span, :]
        for s in range(nseg):
            base = s * stride + PAD_LEAD - half
            acc = jnp.zeros((SEG, GROUP_W), F32)
            for k in range(CF_KERNEL):
                r = (base + k) % SUBLANES
                a = base + k - r
                win = pad_ref[a:a + SEG, :] if r == 0 else shift_ref[r, a:a + SEG, :]
                acc = acc + cfw_ref[k:k + 1, :] * win
            finish_cf(acc, s)

    def sc_products():
        scc = zc_ref[:, 0:GROUP_W].astype(F32)
        scv = zc_ref[:, GROUP_W:2 * GROUP_W].astype(F32)
        scb = zc_ref[:, 2 * GROUP_W:3 * GROUP_W].astype(F32)
        return scc * scv, scb

    if not ctx:
        u = glu()
        for s in range(nseg):
            pad_ref[s * LAT_STRIDE:s * LAT_STRIDE + PAD_LEAD, :] = zeros_lead
            pad_ref[s * LAT_STRIDE + PAD_LEAD:(s + 1) * LAT_STRIDE, :] = u[s * SEG:(s + 1) * SEG]
        pad_ref[nseg * LAT_STRIDE:nseg * LAT_STRIDE + PAD_LEAD, :] = zeros_lead
        conformer(LAT_STRIDE)
        usc, scb = sc_products()
        up = prev_ref[:, 0:GROUP_W].astype(F32) * prev_ref[:, GROUP_W:2 * GROUP_W].astype(F32)
        un = next_ref[:, 0:GROUP_W].astype(F32) * next_ref[:, GROUP_W:2 * GROUP_W].astype(F32)
        up = jnp.where(first_lat, 0.0, up)
        un = jnp.where(last_lat, 0.0, un)
        above = jnp.concatenate([up, usc[:TM - GRID_W]], axis=0)
        below = jnp.concatenate([usc[GRID_W:], un], axis=0)
        y = scb * (scw_ref[0:1, :] * above + scw_ref[1:2, :] * usc + scw_ref[2:3, :] * below)
        o_ref[:, GROUP_W:2 * GROUP_W] = y.astype(BF16)
    else:
        u = glu()
        pad_ref[0:PAD_LEAD, :] = zeros_lead
        pad_ref[PAD_LEAD:PAD_LEAD + TM, :] = u
        pad_ref[PAD_LEAD + TM:PAD_ROWS, :] = jnp.zeros((PAD_ROWS - PAD_LEAD - TM, GROUP_W), F32)
        conformer(SEG)
        usc, scb = sc_products()
        pad_ref[PAD_LEAD:PAD_LEAD + TM, :] = usc
        before = pad_ref[PAD_LEAD - 1:PAD_LEAD - 1 + TM, :]
        after = pad_ref[PAD_LEAD + 1:PAD_LEAD + 1 + TM, :]
        y = scb * (scw_ref[0:1, :] * before + scw_ref[1:2, :] * usc + scw_ref[2:3, :] * after)
        o_ref[:, GROUP_W:2 * GROUP_W] = y.astype(BF16)


def _scan_block(d, j, tpb):
    return jnp.where(d == 0, j, jnp.where(j == 0, 0, tpb - j))


def _tri(rev, size):
    ri = lax.broadcasted_iota(jnp.int32, (size, size), 0)
    ci = lax.broadcasted_iota(jnp.int32, (size, size), 1)
    return (ri <= ci) if rev else (ri >= ci)


def _gla_run(rev, zg_ref, w2_ref, b2_ref, o_ref, st_ref):
    nk = N_HEADS * HEAD_DK
    n_b = zg_ref.shape[0]
    tri = _tri(rev, GLA_CHUNK)
    trib = tri.astype(BF16)
    order = range(TM // GLA_CHUNK - 1, -1, -1) if rev else range(TM // GLA_CHUNK)
    w2b = w2_ref[0].astype(BF16)
    la = []
    for b in range(n_b):
        zz = _dot(zg_ref[b, :, 3 * GROUP_W:3 * GROUP_W + LANES], w2b) + b2_ref[0]
        la.append(_log_sigmoid(zz) * (1.0 / GLA_TAU))
    pre = {}
    for b in range(n_b):
        for c in order:
            rows = slice(c * GLA_CHUNK, (c + 1) * GLA_CHUNK)
            la_c = la[b][rows]
            hi = la_c.astype(BF16)
            lo = (la_c - hi.astype(F32)).astype(BF16)
            bc = _dot(trib, hi) + _dot(trib, lo)
            b_last = bc[0:1] if rev else bc[GLA_CHUNK - 1:GLA_CHUNK]
            b_mid = bc[GLA_CHUNK // 2:GLA_CHUNK // 2 + 1]
            q = zg_ref[b, rows, 0:nk].astype(F32) * (HEAD_DK ** -0.5)
            k = zg_ref[b, rows, nk:2 * nk].astype(F32)
            pre[b, c] = dict(
                q_in=(q * jnp.exp(bc)).astype(BF16),
                q_mid=(q * jnp.exp(bc - b_mid)).astype(BF16),
                k_mid=(k * jnp.exp(b_mid - bc)).astype(BF16),
                k_st=(k * jnp.exp(b_last - bc)).astype(BF16),
                a_row=jnp.exp(b_last),
            )
    scores = {}
    for b in range(n_b):
        for c in order:
            p = pre[b, c]
            for h in range(N_HEADS):
                hs = slice(h * HEAD_DK, (h + 1) * HEAD_DK)
                scores[b, c, h] = jnp.where(tri, _dot_nt(p["q_mid"][:, hs], p["k_mid"][:, hs]), 0.0).astype(BF16)
    for b in range(n_b):
        states = [st_ref[b, h] for h in range(N_HEADS)]
        for c in order:
            rows = slice(c * GLA_CHUNK, (c + 1) * GLA_CHUNK)
            p = pre[b, c]
            for h in range(N_HEADS):
                hs = slice(h * HEAD_DK, (h + 1) * HEAD_DK)
                vs = slice(h * HEAD_DV, (h + 1) * HEAD_DV)
                vh = zg_ref[b, rows, GROUP_W + h * HEAD_DV:GROUP_W + (h + 1) * HEAD_DV]
                st = states[h]
                o_ref[b, rows, vs] = _dot(scores[b, c, h], vh) + _dot_nt(p["q_in"][:, hs], st.astype(BF16))
                states[h] = st * p["a_row"][:, hs] + _dot_tn(vh, p["k_st"][:, hs])
        for h in range(N_HEADS):
            st_ref[b, h] = states[h]


def _ret_run(rev, zr_ref, cos_ref, sin_ref, lg_ref, o_ref, st_ref):
    nk = N_HEADS * HEAD_DK
    batches = range(zr_ref.shape[0])
    lane = lax.broadcasted_iota(jnp.int32, (TM, nk), 1)
    first_half = (lane % HEAD_DK) < (HEAD_DK // 2)
    cosf = cos_ref[...]
    sinf = sin_ref[...]

    def rope(x):
        swapped = jnp.where(first_half, pltpu.roll(x, nk - HEAD_DK // 2, 1), pltpu.roll(x, HEAD_DK // 2, 1))
        return x * cosf + swapped * sinf

    qs = [rope(zr_ref[b, :, 0:nk].astype(F32)).astype(BF16) for b in batches]
    ks = [rope(zr_ref[b, :, nk:2 * nk].astype(F32)) * (HEAD_DK ** -0.5) for b in batches]
    lg = _log_sigmoid(lg_ref[0])
    tri = _tri(rev, TM)
    ri = lax.broadcasted_iota(jnp.int32, (TM, TM), 0)
    ci = lax.broadcasted_iota(jnp.int32, (TM, TM), 1)
    rel = ((ci - ri) if rev else (ri - ci)).astype(F32)
    pos = lax.broadcasted_iota(jnp.int32, (TM, LANES), 0).astype(F32)
    q_steps = (TM - pos) if rev else (pos + 1.0)
    k_steps = pos if rev else (TM - 1.0 - pos)
    lg_wide = jnp.concatenate([lg, lg], axis=1)
    scores = {}
    for h in range(N_HEADS):
        hs = slice(h * HEAD_DK, (h + 1) * HEAD_DK)
        decay = jnp.where(tri, jnp.exp(rel * lg_wide[h:h + 1, :]), 0.0)
        for b in batches:
            scores[b, h] = (_dot_nt(qs[b][:, hs], ks[b][:, hs].astype(BF16)) * decay).astype(BF16)
    for h in range(N_HEADS):
        hs = slice(h * HEAD_DK, (h + 1) * HEAD_DK)
        vs = slice(h * HEAD_DV, (h + 1) * HEAD_DV)
        lgh = lg[h:h + 1, :]
        q_decay = jnp.exp(q_steps * lgh)
        k_decay = jnp.exp(k_steps[:, 0:HEAD_DK] * lgh[:, 0:HEAD_DK])
        chunk_decay = jnp.exp(float(TM) * lgh[:, 0:HEAD_DK])
        for b in batches:
            st = st_ref[b, h]
            vh = zr_ref[b, :, GROUP_W + h * HEAD_DV:GROUP_W + (h + 1) * HEAD_DV]
            o_ref[b, :, vs] = _dot(scores[b, h], vh) + _dot_nt(qs[b][:, hs], st.astype(BF16)) * q_decay
            st_ref[b, h] = st * chunk_decay + _dot_tn(vh, (ks[b][:, hs] * k_decay).astype(BF16))


def _mixers_kernel(zc_ref, prev_ref, next_ref, cfw_ref, cfb_ref, lng_ref, lnb_ref, scw_ref, zg_ref, w2_ref, b2_ref,
                   zr_ref, cos_ref, sin_ref, lg_ref, mix_ref, og_ref, or_ref, pad_ref, shift_ref, stg_ref, str_ref,
                   *, tpb):
    d = pl.program_id(0)
    j = pl.program_id(1)
    blk = _scan_block(d, j, tpb)

    @pl.when(j == 0)
    def _init():
        stg_ref[...] = jnp.zeros_like(stg_ref)
        str_ref[...] = jnp.zeros_like(str_ref)

    def body(rev, ctx):
        _conv_tile(zc_ref.at[0], prev_ref.at[0], next_ref.at[0], cfw_ref, cfb_ref, lng_ref, lnb_ref, scw_ref,
                   mix_ref.at[0], pad_ref, shift_ref, ctx, blk == 1, blk == tpb - 1)
        _gla_run(rev, zg_ref, w2_ref, b2_ref, og_ref.at[0], stg_ref)
        _ret_run(rev, zr_ref, cos_ref, sin_ref, lg_ref, or_ref.at[0], str_ref)

    for rev in (False, True):
        for ctx in (False, True):
            pl.when((d == int(rev)) & ((j == 0) == ctx))(functools.partial(body, rev, ctx))


def _mixers_call(zc, zg, zr, cfw, cfb, lng, lnb, scw, w2pad, b2, cos_t, sin_t, lg, bsz, tpb):
    n = zc.shape[0]
    t = n // bsz
    nk = N_HEADS * HEAD_DK
    r = TM // GRID_W
    nhalo = t // GRID_W
    assert bsz == 2

    def blk(d, j):
        return _scan_block(d, j, tpb)

    def const(shape):
        return pl.BlockSpec(shape, lambda d, j: (0,) * len(shape))

    scan_out = pl.BlockSpec((1, bsz, TM, GROUP_W), lambda d, j: (d, 0, blk(d, j), 0))
    zc3 = zc.reshape(bsz, t, ZC_W)
    mix, og, orr = pl.pallas_call(
        functools.partial(_mixers_kernel, tpb=tpb),
        grid=(2, tpb),
        in_specs=[
            pl.BlockSpec((1, TM, ZC_W), lambda d, j: (d, blk(d, j), 0)),
            pl.BlockSpec((1, GRID_W, 2 * GROUP_W), lambda d, j: (d, jnp.maximum(blk(d, j) * r - 1, 0), 0)),
            pl.BlockSpec((1, GRID_W, 2 * GROUP_W), lambda d, j: (d, jnp.minimum(blk(d, j) * r + r, nhalo - 1), 0)),
            const((CF_KERNEL, GROUP_W)),
            const((1, GROUP_W)),
            const((1, GROUP_W)),
            const((1, GROUP_W)),
            const((SC_KERNEL, GROUP_W)),
            pl.BlockSpec((bsz, TM, ZG_W), lambda d, j: (0, blk(d, j), 0)),
            pl.BlockSpec((1, LANES, nk), lambda d, j: (d, 0, 0)),
            pl.BlockSpec((1, 1, nk), lambda d, j: (d, 0, 0)),
            pl.BlockSpec((bsz, TM, ZR_W), lambda d, j: (0, blk(d, j), 0)),
            pl.BlockSpec((TM, nk), lambda d, j: (blk(d, j), 0)),
            pl.BlockSpec((TM, nk), lambda d, j: (blk(d, j), 0)),
            pl.BlockSpec((1, N_HEADS, LANES), lambda d, j: (d, 0, 0)),
        ],
        out_specs=[
            pl.BlockSpec((1, TM, 2 * GROUP_W), lambda d, j: (d, blk(d, j), 0)),
            scan_out,
            scan_out,
        ],
        out_shape=[
            jax.ShapeDtypeStruct((bsz, t, 2 * GROUP_W), BF16),
            jax.ShapeDtypeStruct((2, bsz, t, GROUP_W), F32),
            jax.ShapeDtypeStruct((2, bsz, t, GROUP_W), F32),
        ],
        scratch_shapes=[
            pltpu.VMEM((PAD_ROWS, GROUP_W), F32),
            pltpu.VMEM((SUBLANES, PAD_ROWS, GROUP_W), F32),
            pltpu.VMEM((bsz, N_HEADS, HEAD_DV, HEAD_DK), F32),
            pltpu.VMEM((bsz, N_HEADS, HEAD_DV, HEAD_DK), F32),
        ],
        compiler_params=_cparams(("arbitrary", "arbitrary")),
        name="mixers",
    )(zc3, zc3, zc3, cfw, cfb, lng, lnb, scw, zg.reshape(bsz, t, ZG_W), w2pad, b2, zr.reshape(bsz, t, ZR_W),
      cos_t, sin_t, lg)
    return mix.reshape(n, 2 * GROUP_W), og.reshape(2, n, GROUP_W), orr.reshape(2, n, GROUP_W)


def _head_norm(o, g, center):
    outs = []
    for h in range(N_HEADS):
        oh = o[:, h * HEAD_DV:(h + 1) * HEAD_DV]
        if center:
            oh = oh - jnp.mean(oh, axis=-1, keepdims=True)
        outs.append(oh * lax.rsqrt(jnp.mean(oh * oh, axis=-1, keepdims=True) + EPS))
    return jnp.concatenate(outs, axis=1) * g


def _route(logits):
    lane = lax.broadcasted_iota(jnp.int32, logits.shape, 1).astype(F32)
    neg = jnp.float32(-jnp.inf)
    big = jnp.float32(LANES)
    gmask = (lane >= N_EXPERTS) & (lane < N_EXPERTS + N_GROUPS)
    gl = jnp.where(gmask, logits, neg)
    gmax = jnp.max(gl, axis=-1, keepdims=True)
    gidx = jnp.min(jnp.where(gl == gmax, lane - N_EXPERTS, big), axis=-1, keepdims=True)
    g_w = 1.0 / jnp.sum(jnp.where(gmask, jnp.exp(logits - gmax), 0.0), axis=-1, keepdims=True)
    emask = (lane >= gidx * EXPERTS_PER_GROUP) & (lane < (gidx + 1.0) * EXPERTS_PER_GROUP)
    el = jnp.where(emask, logits, neg)
    e1 = jnp.max(el, axis=-1, keepdims=True)
    i1 = jnp.min(jnp.where(el == e1, lane, big), axis=-1, keepdims=True)
    el2 = jnp.where(lane == i1, neg, el)
    e2 = jnp.max(el2, axis=-1, keepdims=True)
    i2 = jnp.min(jnp.where(el2 == e2, lane, big), axis=-1, keepdims=True)
    r = jnp.exp(e2 - e1)
    w1 = g_w / (1.0 + r)
    w2 = g_w * r / (1.0 + r)
    idx = jnp.where(lane == 0.0, i1, jnp.where(lane == 1.0, i2, 0.0)).astype(jnp.int32)
    gate = jnp.where(lane == 0.0, w1, jnp.where(lane == 1.0, w2, 0.0))
    return idx, gate


def _outproj_kernel(mixc_ref, og_ref, or_ref, gr_ref, rg_ref, gng_ref, rng_ref, wout_ref, x_ref, mod_ref, g2_ref,
                    wr_ref, br_ref, xo_ref, h_ref, idx_ref, gate_ref):
    j = pl.program_id(0)
    w = wr_ref[...]
    w_hi = w.astype(BF16)
    w_lo = (w - w_hi.astype(F32)).astype(BF16)
    batches = range(x_ref.shape[0])
    mods = [mod_ref[jnp.where(j == 0, mod_ref.shape[0] - 1, b)] for b in batches]
    glas = [_head_norm(og_ref[0, b] + og_ref[1, b], gng_ref[...], False) * _silu(gr_ref[b].astype(F32))
            for b in batches]
    rets = [_head_norm(or_ref[0, b] + or_ref[1, b], rng_ref[...], True) * _silu(rg_ref[b].astype(F32))
            for b in batches]
    ys = [_dot(mixc_ref[b], wout_ref[0, 0:2 * GROUP_W, :])
          + _dot(glas[b].astype(BF16), wout_ref[0, 2 * GROUP_W:3 * GROUP_W, :])
          + _dot(rets[b].astype(BF16), wout_ref[0, 3 * GROUP_W:4 * GROUP_W, :]) for b in batches]
    hs = []
    for b in batches:
        x = x_ref[b] + mods[b][2:3, :] * ys[b]
        xo_ref[b] = x
        hn = x * lax.rsqrt(jnp.mean(x * x, axis=-1, keepdims=True) + EPS) * g2_ref[...]
        h = hn * (1.0 + mods[b][4:5, :]) + mods[b][3:4, :]
        h_ref[b] = _pack_pairs(h)
        hs.append(h)
    logits = []
    for b in batches:
        hb = hs[b].astype(BF16)
        h_lo = (hs[b] - hb.astype(F32)).astype(BF16)
        logits.append(_dot(hb, w_hi) + _dot(hb, w_lo) + _dot(h_lo, w_hi) + br_ref[...])
    for b in batches:
        idx, gate = _route(logits[b])
        idx_ref[b] = idx
        gate_ref[b] = gate


def _outproj_call(mixc, og, orr, zg, zr, gng, rng, wout, l, x, mods, g2, wr, br, bsz, tpb):
    n, d = x.shape
    t = n // bsz

    def rows(width, col=0):
        return pl.BlockSpec((bsz, TM, width), lambda j: (0, j, col))

    scan = pl.BlockSpec((2, bsz, TM, GROUP_W), lambda j: (0, 0, j, 0))
    outs = pl.pallas_call(
        _outproj_kernel,
        grid=(tpb,),
        in_specs=[
            rows(2 * GROUP_W),
            scan,
            scan,
            rows(GROUP_W, 2),
            rows(GROUP_W, 2),
            pl.BlockSpec((1, GROUP_W), lambda j: (0, 0)),
            pl.BlockSpec((1, GROUP_W), lambda j: (0, 0)),
            pl.BlockSpec((1, d, d), lambda j: (l, 0, 0), pipeline_mode=pl.Buffered(1)),
            rows(d),
            pl.BlockSpec((bsz + 1, N_MOD, d), lambda j: (0, 0, 0)),
            pl.BlockSpec((1, d), lambda j: (0, 0)),
            pl.BlockSpec((d, LANES), lambda j: (0, 0)),
            pl.BlockSpec((1, LANES), lambda j: (0, 0)),
        ],
        out_specs=[rows(d), rows(d // 2), rows(LANES), rows(LANES)],
        out_shape=[
            jax.ShapeDtypeStruct((bsz, t, d), F32),
            jax.ShapeDtypeStruct((bsz, t, d // 2), jnp.uint32),
            jax.ShapeDtypeStruct((bsz, t, LANES), jnp.int32),
            jax.ShapeDtypeStruct((bsz, t, LANES), F32),
        ],
        compiler_params=_cparams(("arbitrary",)),
        name="outproj",
    )(mixc.reshape(bsz, t, -1), og.reshape(2, bsz, t, -1), orr.reshape(2, bsz, t, -1), zg.reshape(bsz, t, -1),
      zr.reshape(bsz, t, -1), gng, rng, wout, x.reshape(bsz, t, d), mods, g2, wr, br)
    return [o.reshape(n, -1) for o in outs]


def _sc_gather(table, idx):
    n_idx = idx.shape[0]
    width = table.shape[1]
    assert n_idx % (SC_WINDOW * SC_WORKERS) == 0
    per = n_idx // SC_WORKERS
    n_steps = per // SC_WINDOW
    assert n_steps % 2 == 0
    per_pad = (per + LANES - 1) // LANES * LANES
    idx_w = jnp.pad(idx.reshape(SC_WORKERS, per), ((0, 0), (0, per_pad - per)))
    mesh = plsc.VectorSubcoreMesh(core_axis_name="core", subcore_axis_name="subcore")
    n_cores = SC_WORKERS // 16

    @functools.partial(
        pl.kernel,
        out_type=jax.ShapeDtypeStruct((n_idx, width), table.dtype),
        mesh=mesh,
        scratch_types=[
            pltpu.VMEM((per_pad,), jnp.int32),
            pltpu.VMEM((2, SC_WINDOW, width), table.dtype),
            pltpu.SemaphoreType.DMA((2,)),
            pltpu.SemaphoreType.DMA((2,)),
        ],
    )
    def gather_kernel(table_hbm, idx_hbm, out_hbm, idx_v, rows_v, sem_g, sem_w):
        wid = lax.axis_index("subcore") * n_cores + lax.axis_index("core")
        base = wid * per
        pltpu.sync_copy(idx_hbm.at[wid], idx_v)

        def gather(step, buf):
            return pltpu.make_async_copy(table_hbm.at[idx_v.at[pl.ds(step * SC_WINDOW, SC_WINDOW)]],
                                         rows_v.at[buf], sem_g.at[buf])

        def write(step, buf):
            return pltpu.make_async_copy(rows_v.at[buf], out_hbm.at[pl.ds(base + step * SC_WINDOW, SC_WINDOW)],
                                         sem_w.at[buf])

        gather(0, 0).start()

        @pl.loop(0, n_steps, step=2)
        def _(s):
            for buf in range(2):
                step = s + buf
                other = 1 - buf
                gather(step, buf).wait()
                write(step, buf).start()

                @pl.when(step >= 1)
                def _():
                    write(step - 1, other).wait()

                @pl.when(step + 1 < n_steps)
                def _():
                    gather(step + 1, other).start()

        write(n_steps - 1, 1).wait()

    return gather_kernel(table, idx_w)


def _expert_up_kernel(blk_e_ref, nvalid_ref, x_ref, w1_ref, w3_ref, h_ref):
    @pl.when(nvalid_ref[pl.program_id(0)] > 0)
    def _compute():
        x = _unpack_pairs(x_ref[...]).astype(BF16)
        h1 = _dot(x, w1_ref[0, 0].astype(BF16))
        h3 = _dot(x, w3_ref[0, 0].astype(BF16))
        h_ref[...] = (_silu(h1) * h3).astype(BF16)


def _expert_down_kernel(blk_e_ref, nvalid_ref, h_ref, w2_ref, y_ref):
    @pl.when(nvalid_ref[pl.program_id(0)] > 0)
    def _compute():
        y_ref[...] = _pack_pairs(_dot(h_ref[...], w2_ref[0, 0].astype(BF16)))


def _experts_call(blk_e, nvalid, xs, w1, w3, w2, l):
    n_slots = xs.shape[0]
    d = w1.shape[-2]
    nb = n_slots // MOE_TB
    hidden = w1.shape[-1]
    up = pl.pallas_call(
        _expert_up_kernel,
        grid_spec=pltpu.PrefetchScalarGridSpec(
            num_scalar_prefetch=2,
            grid=(nb,),
            in_specs=[
                pl.BlockSpec((MOE_TB, d // 2), lambda i, be, nv: (i, 0)),
                pl.BlockSpec((1, 1, d, hidden), lambda i, be, nv: (l, be[i], 0, 0)),
                pl.BlockSpec((1, 1, d, hidden), lambda i, be, nv: (l, be[i], 0, 0)),
            ],
            out_specs=pl.BlockSpec((MOE_TB, hidden), lambda i, be, nv: (i, 0)),
        ),
        out_shape=jax.ShapeDtypeStruct((n_slots, hidden), BF16),
        compiler_params=_cparams(("arbitrary",)),
        name="expert_up",
    )(blk_e, nvalid, xs, w1, w3)
    return pl.pallas_call(
        _expert_down_kernel,
        grid_spec=pltpu.PrefetchScalarGridSpec(
            num_scalar_prefetch=2,
            grid=(nb,),
            in_specs=[
                pl.BlockSpec((MOE_TB, hidden), lambda i, be, nv: (i, 0)),
                pl.BlockSpec((1, 1, hidden, d), lambda i, be, nv: (l, be[i], 0, 0)),
            ],
            out_specs=pl.BlockSpec((MOE_TB, d // 2), lambda i, be, nv: (i, 0)),
        ),
        out_shape=jax.ShapeDtypeStruct((n_slots, d // 2), jnp.uint32),
        compiler_params=_cparams(("arbitrary",)),
        name="expert_down",
    )(blk_e, nvalid, up, w2)


def _slot_plan(idx):
    n = idx.shape[0]
    n_asg = n * TOP_K
    flat_e = idx[:, :TOP_K].reshape(n_asg)
    order = jnp.argsort(flat_e).astype(jnp.int32)
    experts = jnp.arange(N_EXPERTS, dtype=jnp.int32)
    counts = jnp.sum(flat_e[:, None] == experts[None, :], axis=0, dtype=jnp.int32)
    padded = (counts + MOE_TB - 1) // MOE_TB * MOE_TB
    pad_end = jnp.cumsum(padded)
    pad_start = pad_end - padded
    start = jnp.cumsum(counts) - counts
    n_slots = (n_asg + MOE_TB - 1) // MOE_TB * MOE_TB + N_EXPERTS * MOE_TB
    quantum = math.lcm(MOE_TB, 2 * SC_WINDOW * SC_WORKERS)
    n_slots = (n_slots + quantum - 1) // quantum * quantum
    nb = n_slots // MOE_TB
    blk0 = jnp.arange(nb, dtype=jnp.int32) * MOE_TB
    blk_e = jnp.minimum(jnp.sum(blk0[:, None] >= pad_end[None, :], axis=1, dtype=jnp.int32), N_EXPERTS - 1)
    sel = (blk_e[:, None] == experts[None, :]).astype(jnp.int32)
    blk_cnt = jnp.sum(sel * counts[None, :], axis=1)
    blk_pad0 = jnp.sum(sel * pad_start[None, :], axis=1)
    blk_start = jnp.sum(sel * start[None, :], axis=1)
    nvalid = jnp.clip(blk_cnt - (blk0 - blk_pad0), 0, MOE_TB).astype(jnp.int32)
    blk_w = jnp.where(nvalid > 0, blk_e, jnp.max(jnp.where(nvalid > 0, blk_e, 0)))
    within = jnp.arange(MOE_TB, dtype=jnp.int32)[None, :]
    valid = within < nvalid[:, None]
    pos = jnp.clip((blk_start + blk0 - blk_pad0)[:, None] + within, 0, n_asg - 1)
    asg = order[pos.reshape(n_slots)]
    filler = jnp.arange(n_slots, dtype=jnp.int32) % n
    slot_tok = jnp.where(valid.reshape(n_slots), lax.shift_right_logical(asg, 1), filler).astype(jnp.int32)
    rank = jnp.argsort(order).astype(jnp.int32)
    sel_a = (flat_e[:, None] == experts[None, :]).astype(jnp.int32)
    dest = rank + jnp.sum(sel_a * (pad_start - start)[None, :], axis=1)
    dest_ct = dest.reshape(n, TOP_K).T.reshape(n_asg)
    return blk_w, nvalid, slot_tok, dest_ct


def _final_kernel(x_ref, y_ref, gate_ref, mod_ref, g_ref, o_ref):
    x = _combined(x_ref, y_ref, gate_ref, mod_ref)
    o_ref[0] = x * lax.rsqrt(jnp.mean(x * x, axis=-1, keepdims=True) + EPS) * g_ref[...]


def _final_call(x, y, gate, mods, gf, bsz, tpb):
    n, d = x.shape
    lat = tpb - 1
    return pl.pallas_call(
        _final_kernel,
        grid=(bsz, lat),
        in_specs=[
            pl.BlockSpec((TM, d), lambda b, j: (b * tpb + j + 1, 0)),
            pl.BlockSpec((TOP_K, TM, d // 2), lambda b, j: (0, b * tpb + j + 1, 0)),
            pl.BlockSpec((TM, LANES), lambda b, j: (b * tpb + j + 1, 0)),
            pl.BlockSpec((1, N_MOD, d), lambda b, j: (b, 0, 0)),
            pl.BlockSpec((1, d), lambda b, j: (0, 0)),
        ],
        out_specs=pl.BlockSpec((1, TM, d), lambda b, j: (b, j, 0)),
        out_shape=jax.ShapeDtypeStruct((bsz, lat * TM, d), F32),
        compiler_params=_cparams(("arbitrary", "arbitrary")),
        name="final_norm",
    )(x, y, gate, mods, gf)


def _rope_tables(seq):
    n_freq = HEAD_DK // 4
    t = jnp.arange(seq)
    inv = ROPE_BASE ** (-jnp.arange(n_freq, dtype=F32) / n_freq)
    ang = jnp.concatenate([(t // GRID_W).astype(F32)[:, None] * inv, (t % GRID_W).astype(F32)[:, None] * inv], axis=-1)
    cos = jnp.concatenate([jnp.ones((TM, HEAD_DK // 2), F32), jnp.cos(ang)], axis=0)
    sin = jnp.concatenate([jnp.zeros((TM, HEAD_DK // 2), F32), jnp.sin(ang)], axis=0)
    cos_t = jnp.tile(jnp.concatenate([cos, cos], axis=-1), (1, N_HEADS))
    sin_t = jnp.tile(jnp.concatenate([-sin, sin], axis=-1), (1, N_HEADS))
    return cos_t, sin_t


def kernel(x, c, ctx, c_ctx, norm1_g, norm2_g, ada_w, ada_b, w_in, cf_dw, cf_b, cf_ln_g, cf_ln_b, sc_dw, gla_w2,
           gla_b2, gla_ng, ret_logit, ret_ng, w_out, w_grp, b_grp, w_rt, b_rt, e_w1, e_w3, e_w2, final_g):
    bsz, seq, d = x.shape
    depth = w_in.shape[0]
    assert d == D_MODEL and ctx.shape[1] == TM and seq % TM == 0 and bsz == 2 and TOP_K == 2
    assert w_in.shape[-1] == IN_W
    tpb = 1 + seq // TM
    n = bsz * tpb * TM
    nk = N_HEADS * HEAD_DK

    s8 = jnp.concatenate([c, c_ctx[None, :], jnp.zeros((8 - bsz - 1, d), F32)], axis=0)
    mods_all = _ada_call(s8, ada_w, ada_b)[:, :bsz + 1, :].reshape(depth, bsz + 1, N_MOD, d)

    w_in_p = _pack_w_in(w_in)
    w_out_b = _cast_w_out(w_out)
    cos_t, sin_t = _rope_tables(seq)
    w2pad = jnp.zeros((depth, 2, LANES, nk), F32)
    w2pad = w2pad.at[:, 0, 0:GLA_RANK, :].set(gla_w2[:, 0]).at[:, 1, GLA_RANK:2 * GLA_RANK, :].set(gla_w2[:, 1])
    wr_all = jnp.concatenate([w_rt, w_grp, jnp.zeros((depth, d, LANES - N_EXPERTS - N_GROUPS), F32)], axis=-1)
    br_all = jnp.concatenate([b_rt, b_grp, jnp.zeros((depth, LANES - N_EXPERTS - N_GROUPS), F32)], axis=-1)

    out = None
    stream = (ctx.reshape(bsz * TM, d), x.reshape(bsz * seq, d))
    for l in range(depth):
        mods = mods_all[l]
        zc, zg, zr, xa = _inproj_call(stream, mods, norm1_g[l][None, :], w_in_p, l, tpb, n)
        lg = jnp.broadcast_to(ret_logit[l][:, :, None], (2, N_HEADS, LANES))
        mixc, og, orr = _mixers_call(zc, zg, zr, cf_dw[l], cf_b[l][None, :], cf_ln_g[l][None, :],
                                     cf_ln_b[l][None, :], sc_dw[l], w2pad[l], gla_b2[l][:, None, :], cos_t, sin_t,
                                     lg, bsz, tpb)
        xa, h2, idx, gate = _outproj_call(mixc, og, orr, zg, zr, gla_ng[l][None, :], ret_ng[l][None, :], w_out_b, l,
                                          xa, mods, norm2_g[l][None, :], wr_all[l], br_all[l][None, :], bsz, tpb)
        blk_e, nvalid, slot_tok, dest_ct = _slot_plan(idx)
        xs = _sc_gather(h2, slot_tok)
        ys = _experts_call(blk_e, nvalid, xs, e_w1, e_w3, e_w2, l)
        y = _sc_gather(ys, dest_ct).reshape(TOP_K, n, d // 2)
        if l == depth - 1:
            out = _final_call(xa, y, gate, mods, final_g[None, :], bsz, tpb)
        else:
            stream = (xa, y, gate, mods)
    return out
```

```python
import functools

import jax
import jax.numpy as jnp
from jax import lax
from jax.experimental import pallas as pl
from jax.experimental.pallas import tpu as pltpu
from jax.experimental.pallas import tpu_sc as plsc

F32 = jnp.float32
BF16 = jnp.bfloat16

D_MODEL = 2048
GRID_W = 64
GROUP_W = D_MODEL // 4
CF_KERNEL = 31
SC_KERNEL = 3
N_HEADS = 4
HEAD_DK = 64
HEAD_DV = 128
GLA_RANK = 16
GLA_TAU = 16.0
GLA_CHUNK = 128
ROPE_BASE = 10000.0
N_GROUPS = 4
EXPERTS_PER_GROUP = 4
N_EXPERTS = N_GROUPS * EXPERTS_PER_GROUP
TOP_K = 2
EXPERT_HIDDEN = D_MODEL // 2
N_MOD = 6
EPS = 1e-6

TM = 256
LANES = 128
SUBLANES = 8
ADA_TN = 1024
MOE_TB = 512
VMEM_LIMIT = 56 * 1024 * 1024
SC_WORKERS = 32
SC_WINDOW = 16

ZC_W = 5 * GROUP_W
ZG_W = 3 * GROUP_W + LANES
ZR_W = 3 * GROUP_W
Z_W = ZC_W + ZG_W + ZR_W
IN_W = Z_W - (LANES - 2 * GLA_RANK)
GLR_END = ZC_W + 3 * GROUP_W + 2 * GLA_RANK
PACK_MOVES = (
    (3 * GROUP_W, 0, 2 * GROUP_W),
    (2 * GROUP_W, 2 * GROUP_W, GROUP_W),
    (0, 3 * GROUP_W, 2 * GROUP_W),
    (ZC_W, ZC_W, 3 * GROUP_W),
    (GLR_END, ZC_W + ZG_W, ZR_W),
)


def _cparams(sem):
    return pltpu.CompilerParams(dimension_semantics=sem, vmem_limit_bytes=VMEM_LIMIT)


def _sigmoid(x):
    return 1.0 / (1.0 + jnp.exp(-x))


def _silu(x):
    return x * _sigmoid(x)


def _log_sigmoid(x):
    return jnp.minimum(x, 0.0) - jnp.log1p(jnp.exp(-jnp.abs(x)))


def _dot(a, b):
    return jnp.dot(a, b, preferred_element_type=F32)


def _dot_nt(a, b):
    return lax.dot_general(a, b, (((1,), (1,)), ((), ())), preferred_element_type=F32)


def _dot_tn(a, b):
    return lax.dot_general(a, b, (((0,), (0,)), ((), ())), preferred_element_type=F32)


def _pack_pairs(x):
    w = x.shape[1] // 2
    xb = x.astype(BF16).astype(F32)
    hi = pltpu.bitcast(xb[:, :w], jnp.uint32)
    lo = pltpu.bitcast(xb[:, w:], jnp.uint32)
    return hi | lax.shift_right_logical(lo, jnp.uint32(16))


def _unpack_pairs(p):
    hi = pltpu.bitcast(p & jnp.uint32(0xFFFF0000), F32)
    lo = pltpu.bitcast(lax.shift_left(p, jnp.uint32(16)), F32)
    return jnp.concatenate([hi, lo], axis=1)


def _mod_row(i, tpb):
    return jnp.where(i % tpb == 0, 2, i // tpb)


def _pack_kernel(wt_ref, o_ref):
    def put(src, dst, width):
        for c in range(0, width, TM):
            step = min(TM, width - c)
            o_ref[0, :, dst + c:dst + c + step] = wt_ref[0, src + c:src + c + step, :].T.astype(BF16)

    for src, dst, width in PACK_MOVES:
        put(src, dst, width)
    glr0 = GLR_END - 2 * GLA_RANK
    tile = wt_ref[0, glr0:glr0 + LANES, :].T
    lane = lax.broadcasted_iota(jnp.int32, tile.shape, 1)
    o_ref[0, :, glr0:glr0 + LANES] = jnp.where(lane < 2 * GLA_RANK, tile, 0.0).astype(BF16)


def _pack_w_in(w_in):
    depth, d, _ = w_in.shape
    return pl.pallas_call(
        _pack_kernel,
        grid=(depth, d // TM),
        in_specs=[pl.BlockSpec((1, IN_W, TM), lambda l, i: (l, 0, i))],
        out_specs=pl.BlockSpec((1, TM, Z_W), lambda l, i: (l, i, 0)),
        out_shape=jax.ShapeDtypeStruct((depth, d, Z_W), BF16),
        compiler_params=_cparams(("arbitrary", "arbitrary")),
        name="pack_w_in",
    )(jnp.swapaxes(w_in, 1, 2))


def _cast_kernel(w_ref, o_ref):
    o_ref[...] = w_ref[...].astype(BF16)


def _cast_w_out(w_out):
    depth, k, d = w_out.shape
    return pl.pallas_call(
        _cast_kernel,
        grid=(depth, k // TM),
        in_specs=[pl.BlockSpec((1, TM, d), lambda l, i: (l, i, 0))],
        out_specs=pl.BlockSpec((1, TM, d), lambda l, i: (l, i, 0)),
        out_shape=jax.ShapeDtypeStruct((depth, k, d), BF16),
        compiler_params=_cparams(("arbitrary", "arbitrary")),
        name="cast_w_out",
    )(w_out)


def _ada_kernel(s_ref, w_ref, b_ref, o_ref):
    a = _silu(s_ref[...]).astype(BF16)
    o_ref[0] = _dot(a, w_ref[0].astype(BF16)) + b_ref[0]


def _ada_call(s8, ada_w, ada_b):
    depth, d, nm = ada_w.shape
    return pl.pallas_call(
        _ada_kernel,
        grid=(depth, nm // ADA_TN),
        in_specs=[
            pl.BlockSpec((8, d), lambda l, j: (0, 0)),
            pl.BlockSpec((1, d, ADA_TN), lambda l, j: (l, 0, j)),
            pl.BlockSpec((1, 1, ADA_TN), lambda l, j: (l, 0, j)),
        ],
        out_specs=pl.BlockSpec((1, 8, ADA_TN), lambda l, j: (l, 0, j)),
        out_shape=jax.ShapeDtypeStruct((depth, 8, nm), F32),
        compiler_params=_cparams(("arbitrary", "arbitrary")),
        name="adaln",
    )(s8, ada_w, ada_b.reshape(depth, 1, nm))


def _combined(x_ref, y_ref, gate_ref, mod_ref):
    f = gate_ref[:, 0:1] * _unpack_pairs(y_ref[0]) + gate_ref[:, 1:2] * _unpack_pairs(y_ref[1])
    return x_ref[...] + mod_ref[0, 5:6, :] * f


def _inproj_kernel(*refs, first, tpb):
    if first:
        ctx_ref, lat_ref, mod_ref, g_ref, w_ref, zc_ref, zg_ref, zr_ref, xo_ref = refs
        x = jnp.where(pl.program_id(0) % tpb == 0, ctx_ref[...], lat_ref[...])
    else:
        x_ref, y_ref, gate_ref, pmod_ref, mod_ref, g_ref, w_ref, zc_ref, zg_ref, zr_ref, xo_ref = refs
        x = _combined(x_ref, y_ref, gate_ref, pmod_ref)
    xo_ref[...] = x
    y = x * lax.rsqrt(jnp.mean(x * x, axis=-1, keepdims=True) + EPS) * g_ref[...]
    h = (y * (1.0 + mod_ref[0, 1:2, :]) + mod_ref[0, 0:1, :]).astype(BF16)
    off = 0
    for ref, width in ((zc_ref, ZC_W), (zg_ref, ZG_W), (zr_ref, ZR_W)):
        c = 0
        while c < width:
            step = min(512, width - c)
            ref[:, c:c + step] = _dot(h, w_ref[0, :, off + c:off + c + step]).astype(BF16)
            c += step
        off += width


def _inproj_call(stream, mods, g1, w_packed, l, tpb, n):
    d = D_MODEL
    nt = n // TM
    nlat = tpb - 1
    mod_spec = pl.BlockSpec((1, N_MOD, d), lambda i: (_mod_row(i, tpb), 0, 0))
    first = len(stream) == 2
    if first:
        in_specs = [
            pl.BlockSpec((TM, d), lambda i: (i // tpb, 0)),
            pl.BlockSpec((TM, d), lambda i: ((i // tpb) * nlat + jnp.maximum(i % tpb - 1, 0), 0)),
        ]
    else:
        in_specs = [
            pl.BlockSpec((TM, d), lambda i: (i, 0)),
            pl.BlockSpec((TOP_K, TM, d // 2), lambda i: (0, i, 0)),
            pl.BlockSpec((TM, LANES), lambda i: (i, 0)),
            mod_spec,
        ]
    in_specs += [
        mod_spec,
        pl.BlockSpec((1, d), lambda i: (0, 0)),
        pl.BlockSpec((1, d, Z_W), lambda i: (l, 0, 0), pipeline_mode=pl.Buffered(1)),
    ]
    return pl.pallas_call(
        functools.partial(_inproj_kernel, first=first, tpb=tpb),
        grid=(nt,),
        in_specs=in_specs,
        out_specs=[
            pl.BlockSpec((TM, ZC_W), lambda i: (i, 0)),
            pl.BlockSpec((TM, ZG_W), lambda i: (i, 0)),
            pl.BlockSpec((TM, ZR_W), lambda i: (i, 0)),
            pl.BlockSpec((TM, d), lambda i: (i, 0)),
        ],
        out_shape=[
            jax.ShapeDtypeStruct((n, ZC_W), BF16),
            jax.ShapeDtypeStruct((n, ZG_W), BF16),
            jax.ShapeDtypeStruct((n, ZR_W), BF16),
            jax.ShapeDtypeStruct((n, d), F32),
        ],
        compiler_params=_cparams(("arbitrary",)),
        name="inproj",
    )(*stream, mods, g1, w_packed)


PAD_LEAD = 16
SEG = GRID_W
LAT_STRIDE = SEG + PAD_LEAD
PAD_ROWS = (TM // SEG) * LAT_STRIDE + PAD_LEAD


def _conv_tile(zc_ref, prev_ref, next_ref, cfw_ref, cfb_ref, lng_ref, lnb_ref, scw_ref, o_ref, pad_ref, shift_ref,
               ctx, first_lat, last_lat):
    nseg = TM // SEG
    half = CF_KERNEL // 2
    zeros_lead = jnp.zeros((PAD_LEAD, GROUP_W), F32)

    def glu():
        cfa = zc_ref[:, 3 * GROUP_W:4 * GROUP_W].astype(F32)
        cfg = zc_ref[:, 4 * GROUP_W:5 * GROUP_W].astype(F32)
        return cfa * _sigmoid(cfg)

    def finish_cf(acc, s):
        y = acc + cfb_ref[...]
        yc = y - jnp.mean(y, axis=-1, keepdims=True)
        yn = yc * lax.rsqrt(jnp.mean(yc * yc, axis=-1, keepdims=True) + EPS)
        o_ref[s * SEG:(s + 1) * SEG, 0:GROUP_W] = _silu(yn * lng_ref[...] + lnb_ref[...]).astype(BF16)

    def conformer(stride):
        span = PAD_ROWS - SUBLANES
        for r in range(1, SUBLANES):
            shift_ref[r, 0:span, :] = pad_ref[r:r + span, :]
        for s in range(nseg):
            base = s * stride + PAD_LEAD - half
            acc = jnp.zeros((SEG, GROUP_W), F32)
            for k in range(CF_KERNEL):
                r = (base + k) % SUBLANES
                a = base + k - r
                win = pad_ref[a:a + SEG, :] if r == 0 else shift_ref[r, a:a + SEG, :]
                acc = acc + cfw_ref[k:k + 1, :] * win
            finish_cf(acc, s)

    def sc_products():
        scc = zc_ref[:, 0:GROUP_W].astype(F32)
        scv = zc_ref[:, GROUP_W:2 * GROUP_W].astype(F32)
        scb = zc_ref[:, 2 * GROUP_W:3 * GROUP_W].astype(F32)
        return scc * scv, scb

    if not ctx:
        u = glu()
        for s in range(nseg):
            pad_ref[s * LAT_STRIDE:s * LAT_STRIDE + PAD_LEAD, :] = zeros_lead
            pad_ref[s * LAT_STRIDE + PAD_LEAD:(s + 1) * LAT_STRIDE, :] = u[s * SEG:(s + 1) * SEG]
        pad_ref[nseg * LAT_STRIDE:nseg * LAT_STRIDE + PAD_LEAD, :] = zeros_lead
        conformer(LAT_STRIDE)
        usc, scb = sc_products()
        up = prev_ref[:, 0:GROUP_W].astype(F32) * prev_ref[:, GROUP_W:2 * GROUP_W].astype(F32)
        un = next_ref[:, 0:GROUP_W].astype(F32) * next_ref[:, GROUP_W:2 * GROUP_W].astype(F32)
        up = jnp.where(first_lat, 0.0, up)
        un = jnp.where(last_lat, 0.0, un)
        above = jnp.concatenate([up, usc[:TM - GRID_W]], axis=0)
        below = jnp.concatenate([usc[GRID_W:], un], axis=0)
        y = scb * (scw_ref[0:1, :] * above + scw_ref[1:2, :] * usc + scw_ref[2:3, :] * below)
        o_ref[:, GROUP_W:2 * GROUP_W] = y.astype(BF16)
    else:
        u = glu()
        pad_ref[0:PAD_LEAD, :] = zeros_lead
        pad_ref[PAD_LEAD:PAD_LEAD + TM, :] = u
        pad_ref[PAD_LEAD + TM:PAD_ROWS, :] = jnp.zeros((PAD_ROWS - PAD_LEAD - TM, GROUP_W), F32)
        conformer(SEG)
        usc, scb = sc_products()
        pad_ref[PAD_LEAD:PAD_LEAD + TM, :] = usc
        before = pad_ref[PAD_LEAD - 1:PAD_LEAD - 1 + TM, :]
        after = pad_ref[PAD_LEAD + 1:PAD_LEAD + 1 + TM, :]
        y = scb * (scw_ref[0:1, :] * before + scw_ref[1:2, :] * usc + scw_ref[2:3, :] * after)
        o_ref[:, GROUP_W:2 * GROUP_W] = y.astype(BF16)


def _scan_block(d, j, tpb):
    return jnp.where(d == 0, j, jnp.where(j == 0, 0, tpb - j))


def _tri(rev, size):
    ri = lax.broadcasted_iota(jnp.int32, (size, size), 0)
    ci = lax.broadcasted_iota(jnp.int32, (size, size), 1)
    return (ri <= ci) if rev else (ri >= ci)


def _gla_run(rev, zg_ref, w2_ref, b2_ref, o_ref, st_ref):
    nk = N_HEADS * HEAD_DK
    n_b = zg_ref.shape[0]
    tri = _tri(rev, GLA_CHUNK)
    trib = tri.astype(BF16)
    order = range(TM // GLA_CHUNK - 1, -1, -1) if rev else range(TM // GLA_CHUNK)
    w2b = w2_ref[0].astype(BF16)
    la = []
    for b in range(n_b):
        zz = _dot(zg_ref[b, :, 3 * GROUP_W:3 * GROUP_W + LANES], w2b) + b2_ref[0]
        la.append(_log_sigmoid(zz) * (1.0 / GLA_TAU))
    pre = {}
    for b in range(n_b):
        for c in order:
            rows = slice(c * GLA_CHUNK, (c + 1) * GLA_CHUNK)
            la_c = la[b][rows]
            hi = la_c.astype(BF16)
            lo = (la_c - hi.astype(F32)).astype(BF16)
            bc = _dot(trib, hi) + _dot(trib, lo)
            b_last = bc[0:1] if rev else bc[GLA_CHUNK - 1:GLA_CHUNK]
            b_mid = bc[GLA_CHUNK // 2:GLA_CHUNK // 2 + 1]
            q = zg_ref[b, rows, 0:nk].astype(F32) * (HEAD_DK ** -0.5)
            k = zg_ref[b, rows, nk:2 * nk].astype(F32)
            pre[b, c] = dict(
                q_in=(q * jnp.exp(bc)).astype(BF16),
                q_mid=(q * jnp.exp(bc - b_mid)).astype(BF16),
                k_mid=(k * jnp.exp(b_mid - bc)).astype(BF16),
                k_st=(k * jnp.exp(b_last - bc)).astype(BF16),
                a_row=jnp.exp(b_last),
            )
    scores = {}
    for b in range(n_b):
        for c in order:
            p = pre[b, c]
            for h in range(N_HEADS):
                hs = slice(h * HEAD_DK, (h + 1) * HEAD_DK)
                scores[b, c, h] = jnp.where(tri, _dot_nt(p["q_mid"][:, hs], p["k_mid"][:, hs]), 0.0).astype(BF16)
    for b in range(n_b):
        states = [st_ref[b, h] for h in range(N_HEADS)]
        for c in order:
            rows = slice(c * GLA_CHUNK, (c + 1) * GLA_CHUNK)
            p = pre[b, c]
            for h in range(N_HEADS):
                hs = slice(h * HEAD_DK, (h + 1) * HEAD_DK)
                vs = slice(h * HEAD_DV, (h + 1) * HEAD_DV)
                vh = zg_ref[b, rows, GROUP_W + h * HEAD_DV:GROUP_W + (h + 1) * HEAD_DV]
                st = states[h]
                o_ref[b, rows, vs] = _dot(scores[b, c, h], vh) + _dot_nt(p["q_in"][:, hs], st.astype(BF16))
                states[h] = st * p["a_row"][:, hs] + _dot_tn(vh, p["k_st"][:, hs])
        for h in range(N_HEADS):
            st_ref[b, h] = states[h]


def _ret_run(rev, zr_ref, cos_ref, sin_ref, lg_ref, o_ref, st_ref):
    nk = N_HEADS * HEAD_DK
    batches = range(zr_ref.shape[0])
    lane = lax.broadcasted_iota(jnp.int32, (TM, nk), 1)
    first_half = (lane % HEAD_DK) < (HEAD_DK // 2)
    cosf = cos_ref[...]
    sinf = sin_ref[...]

    def rope(x):
        swapped = jnp.where(first_half, pltpu.roll(x, nk - HEAD_DK // 2, 1), pltpu.roll(x, HEAD_DK // 2, 1))
        return x * cosf + swapped * sinf

    qs = [rope(zr_ref[b, :, 0:nk].astype(F32)).astype(BF16) for b in batches]
    ks = [rope(zr_ref[b, :, nk:2 * nk].astype(F32)) * (HEAD_DK ** -0.5) for b in batches]
    lg = _log_sigmoid(lg_ref[0])
    tri = _tri(rev, TM)
    ri = lax.broadcasted_iota(jnp.int32, (TM, TM), 0)
    ci = lax.broadcasted_iota(jnp.int32, (TM, TM), 1)
    rel = ((ci - ri) if rev else (ri - ci)).astype(F32)
    pos = lax.broadcasted_iota(jnp.int32, (TM, LANES), 0).astype(F32)
    q_steps = (TM - pos) if rev else (pos + 1.0)
    k_steps = pos if rev else (TM - 1.0 - pos)
    lg_wide = jnp.concatenate([lg, lg], axis=1)
    scores = {}
    for h in range(N_HEADS):
        hs = slice(h * HEAD_DK, (h + 1) * HEAD_DK)
        decay = jnp.where(tri, jnp.exp(rel * lg_wide[h:h + 1, :]), 0.0)
        for b in batches:
            scores[b, h] = (_dot_nt(qs[b][:, hs], ks[b][:, hs].astype(BF16)) * decay).astype(BF16)
    for h in range(N_HEADS):
        hs = slice(h * HEAD_DK, (h + 1) * HEAD_DK)
        vs = slice(h * HEAD_DV, (h + 1) * HEAD_DV)
        lgh = lg[h:h + 1, :]
        q_decay = jnp.exp(q_steps * lgh)
        k_decay = jnp.exp(k_steps[:, 0:HEAD_DK] * lgh[:, 0:HEAD_DK])
        chunk_decay = jnp.exp(float(TM) * lgh[:, 0:HEAD_DK])
        for b in batches:
            st = st_ref[b, h]
            vh = zr_ref[b, :, GROUP_W + h * HEAD_DV:GROUP_W + (h + 1) * HEAD_DV]
            o_ref[b, :, vs] = _dot(scores[b, h], vh) + _dot_nt(qs[b][:, hs], st.astype(BF16)) * q_decay
            st_ref[b, h] = st * chunk_decay + _dot_tn(vh, (ks[b][:, hs] * k_decay).astype(BF16))


def _mixers_kernel(zc_ref, prev_ref, next_ref, cfw_ref, cfb_ref, lng_ref, lnb_ref, scw_ref, zg_ref, w2_ref, b2_ref,
                   zr_ref, cos_ref, sin_ref, lg_ref, mix_ref, og_ref, or_ref, pad_ref, shift_ref, stg_ref, str_ref,
                   *, tpb):
    d = pl.program_id(0)
    j = pl.program_id(1)
    blk = _scan_block(d, j, tpb)

    @pl.when(j == 0)
    def _init():
        stg_ref[...] = jnp.zeros_like(stg_ref)
        str_ref[...] = jnp.zeros_like(str_ref)

    def body(rev, ctx):
        _conv_tile(zc_ref.at[0], prev_ref.at[0], next_ref.at[0], cfw_ref, cfb_ref, lng_ref, lnb_ref, scw_ref,
                   mix_ref.at[0], pad_ref, shift_ref, ctx, blk == 1, blk == tpb - 1)
        _gla_run(rev, zg_ref, w2_ref, b2_ref, og_ref.at[0], stg_ref)
        _ret_run(rev, zr_ref, cos_ref, sin_ref, lg_ref, or_ref.at[0], str_ref)

    for rev in (False, True):
        for ctx in (False, True):
            pl.when((d == int(rev)) & ((j == 0) == ctx))(functools.partial(body, rev, ctx))


def _mixers_call(zc, zg, zr, cfw, cfb, lng, lnb, scw, w2pad, b2, cos_t, sin_t, lg, bsz, tpb):
    n = zc.shape[0]
    t = n // bsz
    nk = N_HEADS * HEAD_DK
    r = TM // GRID_W
    nhalo = t // GRID_W
    assert bsz == 2

    def blk(d, j):
        return _scan_block(d, j, tpb)

    def const(shape):
        return pl.BlockSpec(shape, lambda d, j: (0,) * len(shape))

    scan_out = pl.BlockSpec((1, bsz, TM, GROUP_W), lambda d, j: (d, 0, blk(d, j), 0))
    zc3 = zc.reshape(bsz, t, ZC_W)
    mix, og, orr = pl.pallas_call(
        functools.partial(_mixers_kernel, tpb=tpb),
        grid=(2, tpb),
        in_specs=[
            pl.BlockSpec((1, TM, ZC_W), lambda d, j: (d, blk(d, j), 0)),
            pl.BlockSpec((1, GRID_W, 2 * GROUP_W), lambda d, j: (d, jnp.maximum(blk(d, j) * r - 1, 0), 0)),
            pl.BlockSpec((1, GRID_W, 2 * GROUP_W), lambda d, j: (d, jnp.minimum(blk(d, j) * r + r, nhalo - 1), 0)),
            const((CF_KERNEL, GROUP_W)),
            const((1, GROUP_W)),
            const((1, GROUP_W)),
            const((1, GROUP_W)),
            const((SC_KERNEL, GROUP_W)),
            pl.BlockSpec((bsz, TM, ZG_W), lambda d, j: (0, blk(d, j), 0)),
            pl.BlockSpec((1, LANES, nk), lambda d, j: (d, 0, 0)),
            pl.BlockSpec((1, 1, nk), lambda d, j: (d, 0, 0)),
            pl.BlockSpec((bsz, TM, ZR_W), lambda d, j: (0, blk(d, j), 0)),
            pl.BlockSpec((TM, nk), lambda d, j: (blk(d, j), 0)),
            pl.BlockSpec((TM, nk), lambda d, j: (blk(d, j), 0)),
            pl.BlockSpec((1, N_HEADS, LANES), lambda d, j: (d, 0, 0)),
        ],
        out_specs=[
            pl.BlockSpec((1, TM, 2 * GROUP_W), lambda d, j: (d, blk(d, j), 0)),
            scan_out,
            scan_out,
        ],
        out_shape=[
            jax.ShapeDtypeStruct((bsz, t, 2 * GROUP_W), BF16),
            jax.ShapeDtypeStruct((2, bsz, t, GROUP_W), F32),
            jax.ShapeDtypeStruct((2, bsz, t, GROUP_W), F32),
        ],
        scratch_shapes=[
            pltpu.VMEM((PAD_ROWS, GROUP_W), F32),
            pltpu.VMEM((SUBLANES, PAD_ROWS, GROUP_W), F32),
            pltpu.VMEM((bsz, N_HEADS, HEAD_DV, HEAD_DK), F32),
            pltpu.VMEM((bsz, N_HEADS, HEAD_DV, HEAD_DK), F32),
        ],
        compiler_params=_cparams(("arbitrary", "arbitrary")),
        name="mixers",
    )(zc3, zc3, zc3, cfw, cfb, lng, lnb, scw, zg.reshape(bsz, t, ZG_W), w2pad, b2, zr.reshape(bsz, t, ZR_W),
      cos_t, sin_t, lg)
    return mix.reshape(n, 2 * GROUP_W), og.reshape(2, n, GROUP_W), orr.reshape(2, n, GROUP_W)


def _head_norm(o, g, center):
    outs = []
    for h in range(N_HEADS):
        oh = o[:, h * HEAD_DV:(h + 1) * HEAD_DV]
        if center:
            oh = oh - jnp.mean(oh, axis=-1, keepdims=True)
        outs.append(oh * lax.rsqrt(jnp.mean(oh * oh, axis=-1, keepdims=True) + EPS))
    return jnp.concatenate(outs, axis=1) * g


def _route(logits):
    lane = lax.broadcasted_iota(jnp.int32, logits.shape, 1).astype(F32)
    neg = jnp.float32(-jnp.inf)
    big = jnp.float32(LANES)
    gmask = (lane >= N_EXPERTS) & (lane < N_EXPERTS + N_GROUPS)
    gl = jnp.where(gmask, logits, neg)
    gmax = jnp.max(gl, axis=-1, keepdims=True)
    gidx = jnp.min(jnp.where(gl == gmax, lane - N_EXPERTS, big), axis=-1, keepdims=True)
    g_w = 1.0 / jnp.sum(jnp.where(gmask, jnp.exp(logits - gmax), 0.0), axis=-1, keepdims=True)
    emask = (lane >= gidx * EXPERTS_PER_GROUP) & (lane < (gidx + 1.0) * EXPERTS_PER_GROUP)
    el = jnp.where(emask, logits, neg)
    e1 = jnp.max(el, axis=-1, keepdims=True)
    i1 = jnp.min(jnp.where(el == e1, lane, big), axis=-1, keepdims=True)
    el2 = jnp.where(lane == i1, neg, el)
    e2 = jnp.max(el2, axis=-1, keepdims=True)
    i2 = jnp.min(jnp.where(el2 == e2, lane, big), axis=-1, keepdims=True)
    r = jnp.exp(e2 - e1)
    w1 = g_w / (1.0 + r)
    w2 = g_w * r / (1.0 + r)
    idx = jnp.where(lane == 0.0, i1, jnp.where(lane == 1.0, i2, 0.0)).astype(jnp.int32)
    gate = jnp.where(lane == 0.0, w1, jnp.where(lane == 1.0, w2, 0.0))
    return idx, gate


def _outproj_kernel(mixc_ref, og_ref, or_ref, gr_ref, rg_ref, gng_ref, rng_ref, wout_ref, x_ref, mod_ref, g2_ref,
                    wr_ref, br_ref, xo_ref, h_ref, idx_ref, gate_ref):
    j = pl.program_id(0)
    w = wr_ref[...]
    w_hi = w.astype(BF16)
    w_lo = (w - w_hi.astype(F32)).astype(BF16)
    batches = range(x_ref.shape[0])
    mods = [mod_ref[jnp.where(j == 0, mod_ref.shape[0] - 1, b)] for b in batches]
    glas = [_head_norm(og_ref[0, b] + og_ref[1, b], gng_ref[...], False) * _silu(gr_ref[b].astype(F32))
            for b in batches]
    rets = [_head_norm(or_ref[0, b] + or_ref[1, b], rng_ref[...], True) * _silu(rg_ref[b].astype(F32))
            for b in batches]
    ys = [_dot(mixc_ref[b], wout_ref[0, 0:2 * GROUP_W, :])
          + _dot(glas[b].astype(BF16), wout_ref[0, 2 * GROUP_W:3 * GROUP_W, :])
          + _dot(rets[b].astype(BF16), wout_ref[0, 3 * GROUP_W:4 * GROUP_W, :]) for b in batches]
    hs = []
    for b in batches:
        x = x_ref[b] + mods[b][2:3, :] * ys[b]
        xo_ref[b] = x
        hn = x * lax.rsqrt(jnp.mean(x * x, axis=-1, keepdims=True) + EPS) * g2_ref[...]
        h = hn * (1.0 + mods[b][4:5, :]) + mods[b][3:4, :]
        h_ref[b] = _pack_pairs(h)
        hs.append(h)
    w_cat = jnp.concatenate([w_hi, w_lo], axis=1)
    logits = []
    for b in batches:
        hb = hs[b].astype(BF16)
        h_lo = (hs[b] - hb.astype(F32)).astype(BF16)
        both = _dot(hb, w_cat)
        logits.append(both[:, :LANES] + both[:, LANES:] + _dot(h_lo, w_hi) + br_ref[...])
    for b in batches:
        idx, gate = _route(logits[b])
        idx_ref[b] = idx
        gate_ref[b] = gate


def _outproj_call(mixc, og, orr, zg, zr, gng, rng, wout, l, x, mods, g2, wr, br, bsz, tpb):
    n, d = x.shape
    t = n // bsz

    def rows(width, col=0):
        return pl.BlockSpec((bsz, TM, width), lambda j: (0, j, col))

    scan = pl.BlockSpec((2, bsz, TM, GROUP_W), lambda j: (0, 0, j, 0))
    outs = pl.pallas_call(
        _outproj_kernel,
        grid=(tpb,),
        in_specs=[
            rows(2 * GROUP_W),
            scan,
            scan,
            rows(GROUP_W, 2),
            rows(GROUP_W, 2),
            pl.BlockSpec((1, GROUP_W), lambda j: (0, 0)),
            pl.BlockSpec((1, GROUP_W), lambda j: (0, 0)),
            pl.BlockSpec((1, d, d), lambda j: (l, 0, 0), pipeline_mode=pl.Buffered(1)),
            rows(d),
            pl.BlockSpec((bsz + 1, N_MOD, d), lambda j: (0, 0, 0)),
            pl.BlockSpec((1, d), lambda j: (0, 0)),
            pl.BlockSpec((d, LANES), lambda j: (0, 0)),
            pl.BlockSpec((1, LANES), lambda j: (0, 0)),
        ],
        out_specs=[rows(d), rows(d // 2), rows(LANES), rows(LANES)],
        out_shape=[
            jax.ShapeDtypeStruct((bsz, t, d), F32),
            jax.ShapeDtypeStruct((bsz, t, d // 2), jnp.uint32),
            jax.ShapeDtypeStruct((bsz, t, LANES), jnp.int32),
            jax.ShapeDtypeStruct((bsz, t, LANES), F32),
        ],
        compiler_params=_cparams(("arbitrary",)),
        name="outproj",
    )(mixc.reshape(bsz, t, -1), og.reshape(2, bsz, t, -1), orr.reshape(2, bsz, t, -1), zg.reshape(bsz, t, -1),
      zr.reshape(bsz, t, -1), gng, rng, wout, x.reshape(bsz, t, d), mods, g2, wr, br)
    return [o.reshape(n, -1) for o in outs]


def _sc_gather(table, idx):
    n_idx = idx.shape[0]
    width = table.shape[1]
    assert n_idx % (SC_WINDOW * SC_WORKERS) == 0
    per = n_idx // SC_WORKERS
    n_steps = per // SC_WINDOW
    assert n_steps % 2 == 0
    per_pad = (per + LANES - 1) // LANES * LANES
    idx_w = jnp.pad(idx.reshape(SC_WORKERS, per), ((0, 0), (0, per_pad - per)))
    mesh = plsc.VectorSubcoreMesh(core_axis_name="core", subcore_axis_name="subcore")
    n_cores = SC_WORKERS // 16

    @functools.partial(
        pl.kernel,
        out_type=jax.ShapeDtypeStruct((n_idx, width), table.dtype),
        mesh=mesh,
        scratch_types=[
            pltpu.VMEM((per_pad,), jnp.int32),
            pltpu.VMEM((2, SC_WINDOW, width), table.dtype),
            pltpu.SemaphoreType.DMA((2,)),
            pltpu.SemaphoreType.DMA((2,)),
        ],
    )
    def gather_kernel(table_hbm, idx_hbm, out_hbm, idx_v, rows_v, sem_g, sem_w):
        wid = lax.axis_index("subcore") * n_cores + lax.axis_index("core")
        base = wid * per
        pltpu.sync_copy(idx_hbm.at[wid], idx_v)

        def gather(step, buf):
            return pltpu.make_async_copy(table_hbm.at[idx_v.at[pl.ds(step * SC_WINDOW, SC_WINDOW)]],
                                         rows_v.at[buf], sem_g.at[buf])

        def write(step, buf):
            return pltpu.make_async_copy(rows_v.at[buf], out_hbm.at[pl.ds(base + step * SC_WINDOW, SC_WINDOW)],
                                         sem_w.at[buf])

        gather(0, 0).start()

        @pl.loop(0, n_steps, step=2)
        def _(s):
            for buf in range(2):
                step = s + buf
                other = 1 - buf
                gather(step, buf).wait()
                write(step, buf).start()

                @pl.when(step >= 1)
                def _():
                    write(step - 1, other).wait()

                @pl.when(step + 1 < n_steps)
                def _():
                    gather(step + 1, other).start()

        write(n_steps - 1, 1).wait()

    return gather_kernel(table, idx_w)


def _expert_up_kernel(blk_e_ref, nvalid_ref, x_ref, w1_ref, w3_ref, h_ref):
    @pl.when(nvalid_ref[pl.program_id(0)] > 0)
    def _compute():
        x = _unpack_pairs(x_ref[...]).astype(BF16)
        h1 = _dot(x, w1_ref[0, 0].astype(BF16))
        h3 = _dot(x, w3_ref[0, 0].astype(BF16))
        h_ref[...] = (_silu(h1) * h3).astype(BF16)


def _expert_down_kernel(blk_e_ref, nvalid_ref, h_ref, w2_ref, y_ref):
    @pl.when(nvalid_ref[pl.program_id(0)] > 0)
    def _compute():
        y_ref[...] = _pack_pairs(_dot(h_ref[...], w2_ref[0, 0].astype(BF16)))


def _experts_call(blk_e, nvalid, xs, w1, w3, w2, l):
    n_slots = xs.shape[0]
    d = w1.shape[-2]
    nb = n_slots // MOE_TB
    hidden = w1.shape[-1]
    up = pl.pallas_call(
        _expert_up_kernel,
        grid_spec=pltpu.PrefetchScalarGridSpec(
            num_scalar_prefetch=2,
            grid=(nb,),
            in_specs=[
                pl.BlockSpec((MOE_TB, d // 2), lambda i, be, nv: (i, 0)),
                pl.BlockSpec((1, 1, d, hidden), lambda i, be, nv: (l, be[i], 0, 0)),
                pl.BlockSpec((1, 1, d, hidden), lambda i, be, nv: (l, be[i], 0, 0)),
            ],
            out_specs=pl.BlockSpec((MOE_TB, hidden), lambda i, be, nv: (i, 0)),
        ),
        out_shape=jax.ShapeDtypeStruct((n_slots, hidden), BF16),
        compiler_params=_cparams(("arbitrary",)),
        name="expert_up",
    )(blk_e, nvalid, xs, w1, w3)
    return pl.pallas_call(
        _expert_down_kernel,
        grid_spec=pltpu.PrefetchScalarGridSpec(
            num_scalar_prefetch=2,
            grid=(nb,),
            in_specs=[
                pl.BlockSpec((MOE_TB, hidden), lambda i, be, nv: (i, 0)),
                pl.BlockSpec((1, 1, hidden, d), lambda i, be, nv: (l, be[i], 0, 0)),
            ],
            out_specs=pl.BlockSpec((MOE_TB, d // 2), lambda i, be, nv: (i, 0)),
        ),
        out_shape=jax.ShapeDtypeStruct((n_slots, d // 2), jnp.uint32),
        compiler_params=_cparams(("arbitrary",)),
        name="expert_down",
    )(blk_e, nvalid, up, w2)


def _slot_plan(idx):
    n = idx.shape[0]
    n_asg = n * TOP_K
    flat_e = idx[:, :TOP_K].reshape(n_asg)
    order = jnp.argsort(flat_e).astype(jnp.int32)
    experts = jnp.arange(N_EXPERTS, dtype=jnp.int32)
    counts = jnp.sum(flat_e[:, None] == experts[None, :], axis=0, dtype=jnp.int32)
    padded = (counts + MOE_TB - 1) // MOE_TB * MOE_TB
    pad_end = jnp.cumsum(padded)
    pad_start = pad_end - padded
    start = jnp.cumsum(counts) - counts
    n_slots = (n_asg + MOE_TB - 1) // MOE_TB * MOE_TB + N_EXPERTS * MOE_TB
    nb = n_slots // MOE_TB
    blk0 = jnp.arange(nb, dtype=jnp.int32) * MOE_TB
    blk_e = jnp.minimum(jnp.sum(blk0[:, None] >= pad_end[None, :], axis=1, dtype=jnp.int32), N_EXPERTS - 1)
    sel = (blk_e[:, None] == experts[None, :]).astype(jnp.int32)
    blk_cnt = jnp.sum(sel * counts[None, :], axis=1)
    blk_pad0 = jnp.sum(sel * pad_start[None, :], axis=1)
    blk_start = jnp.sum(sel * start[None, :], axis=1)
    nvalid = jnp.clip(blk_cnt - (blk0 - blk_pad0), 0, MOE_TB).astype(jnp.int32)
    blk_w = jnp.where(nvalid > 0, blk_e, jnp.max(jnp.where(nvalid > 0, blk_e, 0)))
    within = jnp.arange(MOE_TB, dtype=jnp.int32)[None, :]
    valid = within < nvalid[:, None]
    pos = jnp.clip((blk_start + blk0 - blk_pad0)[:, None] + within, 0, n_asg - 1)
    asg = order[pos.reshape(n_slots)]
    filler = jnp.arange(n_slots, dtype=jnp.int32) % n
    slot_tok = jnp.where(valid.reshape(n_slots), lax.shift_right_logical(asg, 1), filler).astype(jnp.int32)
    rank = jnp.argsort(order).astype(jnp.int32)
    sel_a = (flat_e[:, None] == experts[None, :]).astype(jnp.int32)
    dest = rank + jnp.sum(sel_a * (pad_start - start)[None, :], axis=1)
    dest_ct = dest.reshape(n, TOP_K).T.reshape(n_asg)
    return blk_w, nvalid, slot_tok, dest_ct


def _final_kernel(x_ref, y_ref, gate_ref, mod_ref, g_ref, o_ref):
    x = _combined(x_ref, y_ref, gate_ref, mod_ref)
    o_ref[0] = x * lax.rsqrt(jnp.mean(x * x, axis=-1, keepdims=True) + EPS) * g_ref[...]


def _final_call(x, y, gate, mods, gf, bsz, tpb):
    n, d = x.shape
    lat = tpb - 1
    return pl.pallas_call(
        _final_kernel,
        grid=(bsz, lat),
        in_specs=[
            pl.BlockSpec((TM, d), lambda b, j: (b * tpb + j + 1, 0)),
            pl.BlockSpec((TOP_K, TM, d // 2), lambda b, j: (0, b * tpb + j + 1, 0)),
            pl.BlockSpec((TM, LANES), lambda b, j: (b * tpb + j + 1, 0)),
            pl.BlockSpec((1, N_MOD, d), lambda b, j: (b, 0, 0)),
            pl.BlockSpec((1, d), lambda b, j: (0, 0)),
        ],
        out_specs=pl.BlockSpec((1, TM, d), lambda b, j: (b, j, 0)),
        out_shape=jax.ShapeDtypeStruct((bsz, lat * TM, d), F32),
        compiler_params=_cparams(("arbitrary", "arbitrary")),
        name="final_norm",
    )(x, y, gate, mods, gf)


def _rope_tables(seq):
    n_freq = HEAD_DK // 4
    t = jnp.arange(seq)
    inv = ROPE_BASE ** (-jnp.arange(n_freq, dtype=F32) / n_freq)
    ang = jnp.concatenate([(t // GRID_W).astype(F32)[:, None] * inv, (t % GRID_W).astype(F32)[:, None] * inv], axis=-1)
    cos = jnp.concatenate([jnp.ones((TM, HEAD_DK // 2), F32), jnp.cos(ang)], axis=0)
    sin = jnp.concatenate([jnp.zeros((TM, HEAD_DK // 2), F32), jnp.sin(ang)], axis=0)
    cos_t = jnp.tile(jnp.concatenate([cos, cos], axis=-1), (1, N_HEADS))
    sin_t = jnp.tile(jnp.concatenate([-sin, sin], axis=-1), (1, N_HEADS))
    return cos_t, sin_t


def kernel(x, c, ctx, c_ctx, norm1_g, norm2_g, ada_w, ada_b, w_in, cf_dw, cf_b, cf_ln_g, cf_ln_b, sc_dw, gla_w2,
           gla_b2, gla_ng, ret_logit, ret_ng, w_out, w_grp, b_grp, w_rt, b_rt, e_w1, e_w3, e_w2, final_g):
    bsz, seq, d = x.shape
    depth = w_in.shape[0]
    assert d == D_MODEL and ctx.shape[1] == TM and seq % TM == 0 and bsz == 2 and TOP_K == 2
    assert w_in.shape[-1] == IN_W
    tpb = 1 + seq // TM
    n = bsz * tpb * TM
    nk = N_HEADS * HEAD_DK

    s8 = jnp.concatenate([c, c_ctx[None, :], jnp.zeros((8 - bsz - 1, d), F32)], axis=0)
    mods_all = _ada_call(s8, ada_w, ada_b)[:, :bsz + 1, :].reshape(depth, bsz + 1, N_MOD, d)

    w_in_p = _pack_w_in(w_in)
    w_out_b = _cast_w_out(w_out)
    cos_t, sin_t = _rope_tables(seq)
    w2pad = jnp.zeros((depth, 2, LANES, nk), F32)
    w2pad = w2pad.at[:, 0, 0:GLA_RANK, :].set(gla_w2[:, 0]).at[:, 1, GLA_RANK:2 * GLA_RANK, :].set(gla_w2[:, 1])
    wr_all = jnp.concatenate([w_rt, w_grp, jnp.zeros((depth, d, LANES - N_EXPERTS - N_GROUPS), F32)], axis=-1)
    br_all = jnp.concatenate([b_rt, b_grp, jnp.zeros((depth, LANES - N_EXPERTS - N_GROUPS), F32)], axis=-1)

    out = None
    stream = (ctx.reshape(bsz * TM, d), x.reshape(bsz * seq, d))
    for l in range(depth):
        mods = mods_all[l]
        zc, zg, zr, xa = _inproj_call(stream, mods, norm1_g[l][None, :], w_in_p, l, tpb, n)
        lg = jnp.broadcast_to(ret_logit[l][:, :, None], (2, N_HEADS, LANES))
        mixc, og, orr = _mixers_call(zc, zg, zr, cf_dw[l], cf_b[l][None, :], cf_ln_g[l][None, :],
                                     cf_ln_b[l][None, :], sc_dw[l], w2pad[l], gla_b2[l][:, None, :], cos_t, sin_t,
                                     lg, bsz, tpb)
        xa, h2, idx, gate = _outproj_call(mixc, og, orr, zg, zr, gla_ng[l][None, :], ret_ng[l][None, :], w_out_b, l,
                                          xa, mods, norm2_g[l][None, :], wr_all[l], br_all[l][None, :], bsz, tpb)
        blk_e, nvalid, slot_tok, dest_ct = _slot_plan(idx)
        xs = _sc_gather(h2, slot_tok)
        ys = _experts_call(blk_e, nvalid, xs, e_w1, e_w3, e_w2, l)
        y = _sc_gather(ys, dest_ct).reshape(TOP_K, n, d // 2)
        if l == depth - 1:
            out = _final_call(xa, y, gate, mods, final_g[None, :], bsz, tpb)
        else:
            stream = (xa, y, gate, mods)
    return out
```
